```python
import jax, jax.numpy as jnp
from jax import lax
import numpy as np

D_MODEL = 2048
BATCH = 2
SEQ = 4096
DEPTH = 1

GRID_W = 64
ROPE_THETA = 10000.0
RMS_EPS = 1e-6
Q_BLOCK = 128

N_HEADS_A = 8
N_KV_A = 2
HEAD_DIM_A = 128
GROUPS_A = N_HEADS_A // N_KV_A

N_HEADS_B = 8
Q_LORA = 512
KV_LORA = 512
NOPE_DIM_B = 128
ROPE_DIM_B = 64
V_DIM_B = 128
QK_DIM_B = NOPE_DIM_B + ROPE_DIM_B

D_FF = ((8 * D_MODEL // 3 + 255) // 256) * 256

PLE_DIM = 256

COL_SIZES = (
    N_HEADS_A * HEAD_DIM_A,
    N_KV_A * HEAD_DIM_A,
    N_KV_A * HEAD_DIM_A,
    Q_LORA,
    KV_LORA,
    ROPE_DIM_B,
    D_MODEL,
    D_MODEL,
)
IN_COLS = sum(COL_SIZES)
SPLITS = tuple(int(s) for s in np.cumsum(COL_SIZES)[:-1])

kernel_name = "hybrid_gqa_mla_gated_encoder_block"


def rmsnorm(x, g):
    x32 = x.astype(jnp.float32)
    y = x32 * lax.rsqrt(jnp.mean(x32 * x32, axis=-1, keepdims=True) + RMS_EPS)
    return (y * g.astype(jnp.float32)).astype(x.dtype)


def axial_angles(row, col, d):
    half = d // 2
    inv = 1.0 / (ROPE_THETA ** (jnp.arange(0, half, 2, dtype=jnp.float32) / half))
    return row[:, None] * inv[None, :], col[:, None] * inv[None, :]


def rotate(x, ang):
    n2 = x.shape[-1] // 2
    cos = jnp.cos(ang)[:, None, :].astype(x.dtype)
    sin = jnp.sin(ang)[:, None, :].astype(x.dtype)
    x1, x2 = x[..., :n2], x[..., n2:]
    return jnp.concatenate([x1 * cos - x2 * sin, x2 * cos + x1 * sin], axis=-1)


def axial_rope(x, ang_r, ang_c):
    half = x.shape[-1] // 2
    return jnp.concatenate([rotate(x[..., :half], ang_r), rotate(x[..., half:], ang_c)], axis=-1)


def block_attention(q, k, v, scale):
    b, s, hk, g, dk = q.shape
    nb = s // Q_BLOCK
    qb = q.reshape(b, nb, Q_BLOCK, hk, g, dk).transpose(1, 0, 2, 3, 4, 5)

    def one_block(qi):
        sc = jnp.einsum('bqhgd,bkhd->bhgqk', qi, k).astype(jnp.float32) * scale
        pr = jax.nn.softmax(sc, axis=-1).astype(v.dtype)
        return jnp.einsum('bhgqk,bkhd->bqhgd', pr, v)

    o = lax.map(one_block, qb)
    return o.transpose(1, 0, 2, 3, 4, 5).reshape(b, s, hk * g, v.shape[-1])


def setup_inputs(seed: int = 0) -> dict:
    key = jax.random.key(seed)
    ks = jax.random.split(key, 32)
    f32 = jnp.float32

    def w(k, shape, fan_in):
        return jax.random.normal(k, shape, f32) * (fan_in ** -0.5)

    def gain(k, n):
        return 1.0 + 0.02 * jax.random.normal(k, (DEPTH, n), f32)

    return {
        "x": jax.random.normal(ks[0], (BATCH, SEQ, D_MODEL), f32),
        "p": jax.random.normal(ks[1], (DEPTH, BATCH, SEQ, PLE_DIM), f32),
        "g_mix": gain(ks[2], D_MODEL),
        "w_in": w(ks[3], (DEPTH, D_MODEL, IN_COLS), D_MODEL),
        "q_norm_a": gain(ks[4], HEAD_DIM_A),
        "k_norm_a": gain(ks[5], HEAD_DIM_A),
        "q_a_norm": gain(ks[6], Q_LORA),
        "w_q_b": w(ks[7], (DEPTH, Q_LORA, N_HEADS_B * QK_DIM_B), Q_LORA),
        "kv_a_norm": gain(ks[8], KV_LORA),
        "w_kv_b": w(ks[9], (DEPTH, KV_LORA, N_HEADS_B * (NOPE_DIM_B + V_DIM_B)), KV_LORA),
        "w_a_out": w(ks[10], (DEPTH, N_HEADS_A * HEAD_DIM_A, D_MODEL), N_HEADS_A * HEAD_DIM_A),
        "w_b_out": w(ks[11], (DEPTH, N_HEADS_B * V_DIM_B, D_MODEL), N_HEADS_B * V_DIM_B),
        "w_o": w(ks[12], (DEPTH, D_MODEL, D_MODEL), D_MODEL),
        "g_ffn": gain(ks[13], D_MODEL),
        "w_gate_up": w(ks[14], (DEPTH, D_MODEL, 2 * D_FF), D_MODEL),
        "w_down": w(ks[15], (DEPTH, D_FF, D_MODEL), D_FF),
        "g_ple": gain(ks[16], D_MODEL),
        "w_ple_gate": w(ks[17], (DEPTH, D_MODEL, D_MODEL), D_MODEL),
        "w_ple_proj": w(ks[18], (DEPTH, PLE_DIM, D_MODEL), PLE_DIM),
        "g_ple_post": gain(ks[19], D_MODEL),
        "g_final": 1.0 + 0.02 * jax.random.normal(ks[20], (D_MODEL,), f32),
    }


def reference(x, p, g_mix, w_in, q_norm_a, k_norm_a, q_a_norm, w_q_b, kv_a_norm, w_kv_b,
              w_a_out, w_b_out, w_o, g_ffn, w_gate_up, w_down, g_ple, w_ple_gate,
              w_ple_proj, g_ple_post, g_final):
    b, s, _ = x.shape
    rows = s // GRID_W
    row = jnp.repeat(jnp.arange(rows, dtype=jnp.float32), GRID_W)
    col = jnp.tile(jnp.arange(GRID_W, dtype=jnp.float32), rows)
    ang_ra, ang_ca = axial_angles(row, col, HEAD_DIM_A)
    ang_rb, ang_cb = axial_angles(row, col, ROPE_DIM_B)

    h = x
    for i in range(DEPTH):
        u = rmsnorm(h, g_mix[i])
        z = u @ w_in[i]
        qa, ka, va, cq, ckv, kr, ga, gb = jnp.split(z, SPLITS, axis=-1)

        qa = rmsnorm(qa.reshape(b, s, N_HEADS_A, HEAD_DIM_A), q_norm_a[i])
        ka = rmsnorm(ka.reshape(b, s, N_KV_A, HEAD_DIM_A), k_norm_a[i])
        qa = axial_rope(qa, ang_ra, ang_ca)
        ka = axial_rope(ka, ang_ra, ang_ca)
        va = va.reshape(b, s, N_KV_A, HEAD_DIM_A)
        oa = block_attention(qa.reshape(b, s, N_KV_A, GROUPS_A, HEAD_DIM_A), ka, va,
                             HEAD_DIM_A ** -0.5)
        ya = oa.reshape(b, s, N_HEADS_A * HEAD_DIM_A) @ w_a_out[i]

        qb = (rmsnorm(cq, q_a_norm[i]) @ w_q_b[i]).reshape(b, s, N_HEADS_B, QK_DIM_B)
        q_nope, q_pe = qb[..., :NOPE_DIM_B], qb[..., NOPE_DIM_B:]
        q_pe = axial_rope(q_pe, ang_rb, ang_cb)
        kv = (rmsnorm(ckv, kv_a_norm[i]) @ w_kv_b[i]).reshape(b, s, N_HEADS_B, NOPE_DIM_B + V_DIM_B)
        k_nope, vb = kv[..., :NOPE_DIM_B], kv[..., NOPE_DIM_B:]
        k_pe = axial_rope(kr[:, :, None, :], ang_rb, ang_cb)
        k_pe = jnp.broadcast_to(k_pe, (b, s, N_HEADS_B, ROPE_DIM_B))
        q_full = jnp.concatenate([q_nope, q_pe], axis=-1)[:, :, :, None, :]
        k_full = jnp.concatenate([k_nope, k_pe], axis=-1)
        ob = block_attention(q_full, k_full, vb, QK_DIM_B ** -0.5)
        yb = ob.reshape(b, s, N_HEADS_B * V_DIM_B) @ w_b_out[i]

        merged = jax.nn.sigmoid(ga) * ya + jax.nn.sigmoid(gb) * yb
        h = h + merged @ w_o[i]

        gu = rmsnorm(h, g_ffn[i]) @ w_gate_up[i]
        g_, u_ = gu[..., :D_FF], gu[..., D_FF:]
        h = h + (jax.nn.silu(g_) * u_) @ w_down[i]

        e = rmsnorm(p[i] @ w_ple_proj[i], g_ple_post[i])
        gate = jax.nn.sigmoid(rmsnorm(h, g_ple[i]) @ w_ple_gate[i])
        h = h + gate * e

    return rmsnorm(h, g_final)
```

```python
import functools
import math

import jax
import jax.numpy as jnp
from jax import lax
from jax.experimental import pallas as pl
from jax.experimental.pallas import tpu as pltpu

F32 = jnp.float32
BF16 = jnp.bfloat16

GRID_W = 64
ROPE_THETA = 10000.0
RMS_EPS = 1e-6

N_HEADS_A = 8
N_KV_A = 2
HEAD_DIM_A = 128
N_HEADS_B = 8
NOPE_DIM_B = 128
ROPE_DIM_B = 64
V_DIM_B = 128
QK_DIM_B = NOPE_DIM_B + ROPE_DIM_B
QK_PAD_B = 256

LANES = 128
V7X_VMEM_BYTES = 64 * 1024 * 1024
LOG2E = math.log2(math.e)


def _vmem_limit(*block_bytes, scratch=0, temps=0):
    need = 2 * sum(block_bytes) + scratch + temps + (4 << 20)
    return int(min(need, V7X_VMEM_BYTES - (6 << 20)))


def _nbytes(shape, dtype):
    return math.prod(shape) * jnp.dtype(dtype).itemsize


def _rms(x32, gain_row):
    ms = jnp.mean(x32 * x32, axis=-1, keepdims=True)
    return x32 * lax.rsqrt(ms + RMS_EPS) * gain_row


def _swap_chunks(y, chunk):
    lane = lax.broadcasted_iota(jnp.int32, y.shape, 1)
    first = (lane % (2 * chunk)) < chunk
    return jnp.where(first, pltpu.roll(y, LANES - chunk, 1), pltpu.roll(y, chunk, 1))


def _rope(y, cos, sin_signed, chunk):
    return y * cos + _swap_chunks(y, chunk) * sin_signed


def _rope_tables(seq, rot_dim):
    rows = seq // GRID_W
    row = jnp.repeat(jnp.arange(rows, dtype=F32), GRID_W)
    col = jnp.tile(jnp.arange(GRID_W, dtype=F32), rows)
    half = rot_dim // 2
    inv = 1.0 / (ROPE_THETA ** (jnp.arange(0, half, 2, dtype=F32) / half))
    ar = row[:, None] * inv[None, :]
    ac = col[:, None] * inv[None, :]
    cos = jnp.concatenate([jnp.cos(ar), jnp.cos(ar), jnp.cos(ac), jnp.cos(ac)], axis=-1)
    sin = jnp.concatenate([-jnp.sin(ar), jnp.sin(ar), -jnp.sin(ac), jnp.sin(ac)], axis=-1)
    pad = LANES - rot_dim
    if pad:
        cos = jnp.concatenate([cos, jnp.ones((seq, pad), F32)], axis=-1)
        sin = jnp.concatenate([sin, jnp.zeros((seq, pad), F32)], axis=-1)
    return cos, sin


def _qkv_a_kernel(x_ref, g_ref, w_ref, qn_ref, kn_ref, cos_ref, sin_ref,
                  o_ref, u_ref, *, q_scale):
    j = pl.program_id(1)

    @pl.when(j == 0)
    def _():
        u_ref[...] = _rms(x_ref[...], g_ref[...]).astype(BF16)

    z = jnp.dot(u_ref[...], w_ref[...].astype(BF16), preferred_element_type=F32)
    cos = cos_ref[...]
    sin = sin_ref[...]
    hd = HEAD_DIM_A
    n_heads_tile = z.shape[1] // hd
    q_tiles = (N_HEADS_A * hd) // z.shape[1]

    @pl.when(j < q_tiles)
    def _():
        for h in range(n_heads_tile):
            y = _rms(z[:, h * hd:(h + 1) * hd], qn_ref[...])
            o_ref[:, h * hd:(h + 1) * hd] = (_rope(y, cos, sin, hd // 4) * q_scale).astype(BF16)

    @pl.when(j == q_tiles)
    def _():
        for h in range(N_KV_A):
            y = _rms(z[:, h * hd:(h + 1) * hd], kn_ref[...])
            o_ref[:, h * hd:(h + 1) * hd] = _rope(y, cos, sin, hd // 4).astype(BF16)
        o_ref[:, N_KV_A * hd:] = z[:, N_KV_A * hd:].astype(BF16)


def _qkv_a(x2, g_mix, w_in, q_norm, k_norm, cos, sin, *, seq, tm, tn):
    t, d = x2.shape
    n_out = (N_HEADS_A + 2 * N_KV_A) * HEAD_DIM_A
    assert tn == 2 * N_KV_A * HEAD_DIM_A and t % tm == 0 and seq % tm == 0
    sblocks = seq // tm
    q_scale = (HEAD_DIM_A ** -0.5) * LOG2E
    return pl.pallas_call(
        functools.partial(_qkv_a_kernel, q_scale=q_scale),
        grid=(t // tm, n_out // tn),
        in_specs=[
            pl.BlockSpec((tm, d), lambda i, j: (i, 0)),
            pl.BlockSpec((1, d), lambda i, j: (0, 0)),
            pl.BlockSpec((d, tn), lambda i, j: (0, j)),
            pl.BlockSpec((1, HEAD_DIM_A), lambda i, j: (0, 0)),
            pl.BlockSpec((1, HEAD_DIM_A), lambda i, j: (0, 0)),
            pl.BlockSpec((tm, LANES), lambda i, j: (i % sblocks, 0)),
            pl.BlockSpec((tm, LANES), lambda i, j: (i % sblocks, 0)),
        ],
        out_specs=[
            pl.BlockSpec((tm, tn), lambda i, j: (i, j)),
            pl.BlockSpec((tm, d), lambda i, j: (i, 0)),
        ],
        out_shape=[
            jax.ShapeDtypeStruct((t, n_out), BF16),
            jax.ShapeDtypeStruct((t, d), BF16),
        ],
        compiler_params=pltpu.CompilerParams(
            dimension_semantics=("parallel", "arbitrary"),
            vmem_limit_bytes=_vmem_limit(
                _nbytes((tm, d), F32), _nbytes((d, tn), F32), _nbytes((tm, tn), BF16),
                _nbytes((tm, d), BF16), 2 * _nbytes((tm, LANES), F32),
                temps=_nbytes((d, tn), BF16) + 3 * _nbytes((tm, tn), F32)),
        ),
        name="qkv_a",
    )(x2, g_mix, w_in, q_norm, k_norm, cos, sin)


def _lat_b_kernel(u_ref, w_ref, wkr_ref, gains_ref, cos_ref, sin_ref, lat_ref, kpe_ref):
    j = pl.program_id(1)
    u = u_ref[...]
    z = jnp.dot(u, w_ref[...].astype(BF16), preferred_element_type=F32)
    lat_ref[...] = _rms(z, gains_ref[...]).astype(BF16)

    @pl.when(j == 0)
    def _():
        kr = jnp.dot(u, wkr_ref[...].astype(BF16), preferred_element_type=F32)
        kpe_ref[...] = _rope(kr, cos_ref[...], sin_ref[...], ROPE_DIM_B // 4).astype(BF16)


def _lat_b(u, w_in, w_kr_pad, gains, cos, sin, *, seq, tm, col0):
    t, d = u.shape
    lora = gains.shape[-1]
    sblocks = seq // tm
    first = col0 // lora
    return pl.pallas_call(
        _lat_b_kernel,
        grid=(t // tm, 2),
        in_specs=[
            pl.BlockSpec((tm, d), lambda i, j: (i, 0)),
            pl.BlockSpec((d, lora), lambda i, j: (0, first + j)),
            pl.BlockSpec((d, LANES), lambda i, j: (0, 0)),
            pl.BlockSpec((None, 1, lora), lambda i, j: (j, 0, 0)),
            pl.BlockSpec((tm, LANES), lambda i, j: (i % sblocks, 0)),
            pl.BlockSpec((tm, LANES), lambda i, j: (i % sblocks, 0)),
        ],
        out_specs=[
            pl.BlockSpec((tm, lora), lambda i, j: (i, j)),
            pl.BlockSpec((tm, LANES), lambda i, j: (i, 0)),
        ],
        out_shape=[
            jax.ShapeDtypeStruct((t, 2 * lora), BF16),
            jax.ShapeDtypeStruct((t, LANES), BF16),
        ],
        compiler_params=pltpu.CompilerParams(
            dimension_semantics=("parallel", "arbitrary"),
            vmem_limit_bytes=_vmem_limit(
                _nbytes((tm, d), BF16), _nbytes((d, lora), F32), _nbytes((d, LANES), F32),
                _nbytes((tm, lora), BF16), 3 * _nbytes((tm, LANES), F32),
                temps=_nbytes((d, lora), BF16) + 3 * _nbytes((tm, lora), F32)),
        ),
        name="lat_b",
    )(u, w_in, w_kr_pad, gains, cos, sin)


def _qkv_b_kernel(lat_ref, wq_ref, wkv_ref, kpe_ref, cos_ref, sin_ref,
                  q_ref, k_ref, v_ref, *, q_scale):
    lora = wq_ref.shape[0]
    zq = jnp.dot(lat_ref[:, :lora], wq_ref[...].astype(BF16), preferred_element_type=F32)
    zkv = jnp.dot(lat_ref[:, lora:], wkv_ref[...].astype(BF16), preferred_element_type=F32)
    cos = cos_ref[...]
    sin = sin_ref[...]
    kpe = kpe_ref[...]
    for h in range(N_HEADS_B):
        c0 = h * QK_PAD_B
        q_ref[:, c0:c0 + LANES] = (zq[:, c0:c0 + LANES] * q_scale).astype(BF16)
        pe = _rope(zq[:, c0 + LANES:c0 + 2 * LANES], cos, sin, ROPE_DIM_B // 4)
        q_ref[:, c0 + LANES:c0 + 2 * LANES] = (pe * q_scale).astype(BF16)
        k_ref[:, c0:c0 + LANES] = zkv[:, c0:c0 + LANES].astype(BF16)
        k_ref[:, c0 + LANES:c0 + 2 * LANES] = kpe
        v_ref[:, h * V_DIM_B:(h + 1) * V_DIM_B] = zkv[:, c0 + LANES:c0 + 2 * LANES].astype(BF16)


def _qkv_b(lat, wq_pad, w_kv_b, kpe, cos, sin, *, seq, tm):
    t = lat.shape[0]
    lora = wq_pad.shape[0]
    nq = wq_pad.shape[1]
    nkv = w_kv_b.shape[1]
    nv = N_HEADS_B * V_DIM_B
    sblocks = seq // tm
    q_scale = (QK_DIM_B ** -0.5) * LOG2E
    return pl.pallas_call(
        functools.partial(_qkv_b_kernel, q_scale=q_scale),
        grid=(t // tm,),
        in_specs=[
            pl.BlockSpec((tm, 2 * lora), lambda i: (i, 0)),
            pl.BlockSpec((lora, nq), lambda i: (0, 0)),
            pl.BlockSpec((lora, nkv), lambda i: (0, 0)),
            pl.BlockSpec((tm, LANES), lambda i: (i, 0)),
            pl.BlockSpec((tm, LANES), lambda i: (i % sblocks, 0)),
            pl.BlockSpec((tm, LANES), lambda i: (i % sblocks, 0)),
        ],
        out_specs=[
            pl.BlockSpec((tm, nq), lambda i: (i, 0)),
            pl.BlockSpec((tm, nq), lambda i: (i, 0)),
            pl.BlockSpec((tm, nv), lambda i: (i, 0)),
        ],
        out_shape=[
            jax.ShapeDtypeStruct((t, nq), BF16),
            jax.ShapeDtypeStruct((t, nq), BF16),
            jax.ShapeDtypeStruct((t, nv), BF16),
        ],
        compiler_params=pltpu.CompilerParams(
            dimension_semantics=("parallel",),
            vmem_limit_bytes=_vmem_limit(
                _nbytes((tm, 2 * lora), BF16), _nbytes((lora, nq), F32), _nbytes((lora, nkv), F32),
                2 * _nbytes((tm, nq), BF16), _nbytes((tm, nv), BF16), 3 * _nbytes((tm, LANES), F32),
                temps=2 * _nbytes((lora, nq), BF16) + 3 * _nbytes((tm, nq), F32)),
        ),
        name="qkv_b",
    )(lat, wq_pad, w_kv_b, kpe, cos, sin)


def _attn_kernel(q_ref, k_ref, v_ref, o_ref, vt_ref, *, groups, dk, dv, kc):
    seq = k_ref.shape[0]
    tq = q_ref.shape[0]

    @pl.when(pl.program_id(2) == 0)
    def _():
        vt_ref[...] = v_ref[...].astype(F32).T.astype(BF16)

    for g in range(groups):
        q = q_ref[:, g * dk:(g + 1) * dk]

        def body(c, carry):
            m, l, acc = carry
            start = pl.multiple_of(c * kc, kc)
            ks = k_ref[pl.ds(start, kc), :]
            s = lax.dot_general(ks, q, (((1,), (1,)), ((), ())),
                                preferred_element_type=F32)
            m_new = jnp.maximum(m, jnp.max(s, axis=0, keepdims=True))
            alpha = jnp.exp2(m - m_new)
            p = jnp.exp2(s - m_new)
            l_new = alpha * l + jnp.sum(p, axis=0, keepdims=True)
            vt = vt_ref[:, pl.ds(start, kc)]
            acc_new = alpha * acc + jnp.dot(vt, p.astype(BF16), preferred_element_type=F32)
            return m_new, l_new, acc_new

        init = (jnp.full((1, tq), -jnp.inf, F32), jnp.zeros((1, tq), F32),
                jnp.zeros((dv, tq), F32))
        _, l, acc = lax.fori_loop(0, seq // kc, body, init)
        o_ref[:, g * dv:(g + 1) * dv] = (acc * (1.0 / l)).T.astype(BF16)


def _attention(q_arr, k_arr, v_arr, *, batch, seq, n_kv, groups, dk, dv,
               q_col0, k_col0, v_col0, tq, kc):
    t = batch * seq
    nq = seq // tq
    qb0 = q_col0 // (groups * dk)
    kb0 = k_col0 // dk
    vb0 = v_col0 // dv
    return pl.pallas_call(
        functools.partial(_attn_kernel, groups=groups, dk=dk, dv=dv, kc=kc),
        grid=(batch, n_kv, nq),
        in_specs=[
            pl.BlockSpec((tq, groups * dk), lambda b, h, i: (b * nq + i, qb0 + h)),
            pl.BlockSpec((seq, dk), lambda b, h, i: (b, kb0 + h)),
            pl.BlockSpec((seq, dv), lambda b, h, i: (b, vb0 + h)),
        ],
        out_specs=pl.BlockSpec((tq, groups * dv), lambda b, h, i: (b * nq + i, h)),
        out_shape=jax.ShapeDtypeStruct((t, n_kv * groups * dv), BF16),
        scratch_shapes=[pltpu.VMEM((dv, seq), BF16)],
        compiler_params=pltpu.CompilerParams(
            dimension_semantics=("parallel", "parallel", "arbitrary"),
            vmem_limit_bytes=_vmem_limit(
                _nbytes((tq, groups * dk), BF16), _nbytes((seq, dk), BF16),
                _nbytes((seq, dv), BF16), _nbytes((tq, groups * dv), BF16),
                scratch=_nbytes((dv, seq), BF16),
                temps=2 * _nbytes((seq, dv), F32) + 6 * _nbytes((kc, tq), F32)),
        ),
        name="attention",
    )(q_arr, k_arr, v_arr)


def _merge_kernel(u_ref, oa_ref, ob_ref, wa_ref, wb_ref, wga_ref, wgb_ref, o_ref):
    u = u_ref[...]
    ya = jnp.dot(oa_ref[...], wa_ref[...].astype(BF16), preferred_element_type=F32)
    yb = jnp.dot(ob_ref[...], wb_ref[...].astype(BF16), preferred_element_type=F32)
    ga = jnp.dot(u, wga_ref[...].astype(BF16), preferred_element_type=F32)
    gb = jnp.dot(u, wgb_ref[...].astype(BF16), preferred_element_type=F32)
    o_ref[...] = (jax.nn.sigmoid(ga) * ya + jax.nn.sigmoid(gb) * yb).astype(BF16)


def _merge(u, oa, ob, w_a_out, w_b_out, w_gates, *, tm, tn):
    t, d = u.shape
    ka = oa.shape[1]
    kb = ob.shape[1]
    nj = d // tn
    return pl.pallas_call(
        _merge_kernel,
        grid=(t // tm, nj),
        in_specs=[
            pl.BlockSpec((tm, d), lambda i, j: (i, 0)),
            pl.BlockSpec((tm, ka), lambda i, j: (i, 0)),
            pl.BlockSpec((tm, kb), lambda i, j: (i, 0)),
            pl.BlockSpec((ka, tn), lambda i, j: (0, j)),
            pl.BlockSpec((kb, tn), lambda i, j: (0, j)),
            pl.BlockSpec((d, tn), lambda i, j: (0, j)),
            pl.BlockSpec((d, tn), lambda i, j: (0, nj + j)),
        ],
        out_specs=pl.BlockSpec((tm, tn), lambda i, j: (i, j)),
        out_shape=jax.ShapeDtypeStruct((t, d), BF16),
        compiler_params=pltpu.CompilerParams(
            dimension_semantics=("parallel", "arbitrary"),
            vmem_limit_bytes=_vmem_limit(
                _nbytes((tm, d), BF16), _nbytes((tm, ka), BF16), _nbytes((tm, kb), BF16),
                _nbytes((ka, tn), F32), _nbytes((kb, tn), F32), 2 * _nbytes((d, tn), F32),
                _nbytes((tm, tn), BF16),
                temps=_nbytes((ka + kb + 2 * d, tn), BF16) + 6 * _nbytes((tm, tn), F32)),
        ),
        name="merge",
    )(u, oa, ob, w_a_out, w_b_out, w_gates, w_gates)


def _resid_mm_kernel(lhs_ref, w_ref, resid_ref, g_ref, h_ref, un_ref, *, tn, nj):
    j = pl.program_id(1)
    z = jnp.dot(lhs_ref[...], w_ref[...].astype(BF16), preferred_element_type=F32)
    val = resid_ref[...] + z
    for jj in range(nj):
        @pl.when(j == jj)
        def _(jj=jj):
            h_ref[:, jj * tn:(jj + 1) * tn] = val

    @pl.when(j == nj - 1)
    def _():
        un_ref[...] = _rms(h_ref[...], g_ref[...]).astype(BF16)


def _resid_mm(lhs, w, resid, gain, *, tm, tn):
    t, k = lhs.shape
    d = w.shape[1]
    nj = d // tn
    return pl.pallas_call(
        functools.partial(_resid_mm_kernel, tn=tn, nj=nj),
        grid=(t // tm, nj),
        in_specs=[
            pl.BlockSpec((tm, k), lambda i, j: (i, 0)),
            pl.BlockSpec((k, tn), lambda i, j: (0, j)),
            pl.BlockSpec((tm, tn), lambda i, j: (i, j)),
            pl.BlockSpec((1, d), lambda i, j: (0, 0)),
        ],
        out_specs=[
            pl.BlockSpec((tm, d), lambda i, j: (i, 0)),
            pl.BlockSpec((tm, d), lambda i, j: (i, 0)),
        ],
        out_shape=[
            jax.ShapeDtypeStruct((t, d), F32),
            jax.ShapeDtypeStruct((t, d), BF16),
        ],
        compiler_params=pltpu.CompilerParams(
            dimension_semantics=("parallel", "arbitrary"),
            vmem_limit_bytes=_vmem_limit(
                _nbytes((tm, k), BF16), _nbytes((k, tn), F32), _nbytes((tm, tn), F32),
                _nbytes((tm, d), F32), _nbytes((tm, d), BF16),
                temps=_nbytes((k, tn), BF16) + 2 * _nbytes((tm, tn), F32)
                + 2 * _nbytes((tm, d), F32)),
        ),
        name="resid_mm",
    )(lhs, w, resid, gain)


def _ffn_up_kernel(un_ref, wg_ref, wu_ref, o_ref):
    un = un_ref[...]
    g = jnp.dot(un, wg_ref[...].astype(BF16), preferred_element_type=F32)
    u = jnp.dot(un, wu_ref[...].astype(BF16), preferred_element_type=F32)
    o_ref[...] = (jax.nn.silu(g) * u).astype(BF16)


def _ffn_up(un, w_gate_up, *, tm, tn):
    t, d = un.shape
    d_ff = w_gate_up.shape[1] // 2
    nj = d_ff // tn
    return pl.pallas_call(
        _ffn_up_kernel,
        grid=(t // tm, nj),
        in_specs=[
            pl.BlockSpec((tm, d), lambda i, j: (i, 0)),
            pl.BlockSpec((d, tn), lambda i, j: (0, j)),
            pl.BlockSpec((d, tn), lambda i, j: (0, nj + j)),
        ],
        out_specs=pl.BlockSpec((tm, tn), lambda i, j: (i, j)),
        out_shape=jax.ShapeDtypeStruct((t, d_ff), BF16),
        compiler_params=pltpu.CompilerParams(
            dimension_semantics=("parallel", "arbitrary"),
            vmem_limit_bytes=_vmem_limit(
                _nbytes((tm, d), BF16), 2 * _nbytes((d, tn), F32), _nbytes((tm, tn), BF16),
                temps=2 * _nbytes((d, tn), BF16) + 4 * _nbytes((tm, tn), F32)),
        ),
        name="ffn_up",
    )(un, w_gate_up, w_gate_up)


def _ple_kernel(un_ref, w_ref, h_ref, p_ref, wp_ref, gpost_ref, gfin_ref,
                o_ref, e_ref, *, tn, nj, final_norm):
    j = pl.program_id(1)

    @pl.when(j == 0)
    def _():
        e = jnp.dot(p_ref[...].astype(BF16), wp_ref[...].astype(BF16),
                    preferred_element_type=F32)
        e_ref[...] = _rms(e, gpost_ref[...])

    gate = jax.nn.sigmoid(
        jnp.dot(un_ref[...], w_ref[...].astype(BF16), preferred_element_type=F32))
    for jj in range(nj):
        @pl.when(j == jj)
        def _(jj=jj):
            o_ref[:, jj * tn:(jj + 1) * tn] = h_ref[...] + gate * e_ref[:, jj * tn:(jj + 1) * tn]

    if final_norm:
        @pl.when(j == nj - 1)
        def _():
            o_ref[...] = _rms(o_ref[...], gfin_ref[...])


def _ple(un, w_gate, h, p2, w_proj, g_post, g_final, *, tm, tn, final_norm):
    t, d = un.shape
    pd = p2.shape[1]
    nj = d // tn
    return pl.pallas_call(
        functools.partial(_ple_kernel, tn=tn, nj=nj, final_norm=final_norm),
        grid=(t // tm, nj),
        in_specs=[
            pl.BlockSpec((tm, d), lambda i, j: (i, 0)),
            pl.BlockSpec((d, tn), lambda i, j: (0, j)),
            pl.BlockSpec((tm, tn), lambda i, j: (i, j)),
            pl.BlockSpec((tm, pd), lambda i, j: (i, 0)),
            pl.BlockSpec((pd, d), lambda i, j: (0, 0)),
            pl.BlockSpec((1, d), lambda i, j: (0, 0)),
            pl.BlockSpec((1, d), lambda i, j: (0, 0)),
        ],
        out_specs=pl.BlockSpec((tm, d), lambda i, j: (i, 0)),
        out_shape=jax.ShapeDtypeStruct((t, d), F32),
        scratch_shapes=[pltpu.VMEM((tm, d), F32)],
        compiler_params=pltpu.CompilerParams(
            dimension_semantics=("parallel", "arbitrary"),
            vmem_limit_bytes=_vmem_limit(
                _nbytes((tm, d), BF16), _nbytes((d, tn), F32), _nbytes((tm, tn), F32),
                _nbytes((tm, pd), F32), _nbytes((pd, d), F32), _nbytes((tm, d), F32),
                scratch=_nbytes((tm, d), F32),
                temps=_nbytes((d, tn), BF16) + _nbytes((pd, d), BF16)
                + 2 * _nbytes((tm, tn), F32) + 2 * _nbytes((tm, d), F32)),
        ),
        name="ple_final",
    )(un, w_gate, h, p2, w_proj, g_post, g_final)


def kernel(x, p, g_mix, w_in, q_norm_a, k_norm_a, q_a_norm, w_q_b, kv_a_norm, w_kv_b,
           w_a_out, w_b_out, w_o, g_ffn, w_gate_up, w_down, g_ple, w_ple_gate,
           w_ple_proj, g_ple_post, g_final):
    batch, seq, d = x.shape
    depth = w_in.shape[0]
    t = batch * seq
    q_lora = q_a_norm.shape[-1]
    kv_lora = kv_a_norm.shape[-1]
    assert q_lora == kv_lora
    n_qa = N_HEADS_A * HEAD_DIM_A
    n_kva = N_KV_A * HEAD_DIM_A
    lat0 = n_qa + 2 * n_kva
    kr0 = lat0 + q_lora + kv_lora
    gate0 = kr0 + ROPE_DIM_B

    cos_a, sin_a = _rope_tables(seq, HEAD_DIM_A)
    cos_b, sin_b = _rope_tables(seq, ROPE_DIM_B)

    h = x.reshape(t, d)
    for i in range(depth):
        wi = w_in[i]
        w_kr_pad = jnp.pad(wi[:, kr0:gate0], ((0, 0), (0, LANES - ROPE_DIM_B)))
        w_gates = wi[:, gate0:]
        wq_pad = jnp.pad(
            w_q_b[i].reshape(q_lora, N_HEADS_B, QK_DIM_B),
            ((0, 0), (0, 0), (0, QK_PAD_B - QK_DIM_B))).reshape(q_lora, N_HEADS_B * QK_PAD_B)
        lat_gains = jnp.stack([q_a_norm[i], kv_a_norm[i]])[:, None, :]

        qkv_a, u = _qkv_a(h, g_mix[i][None], wi, q_norm_a[i][None], k_norm_a[i][None],
                          cos_a, sin_a, seq=seq, tm=512, tn=2 * n_kva)
        lat, kpe = _lat_b(u, wi, w_kr_pad, lat_gains, cos_b, sin_b, seq=seq, tm=512, col0=lat0)
        q_b, k_b, v_b = _qkv_b(lat, wq_pad, w_kv_b[i], kpe, cos_b, sin_b, seq=seq, tm=512)

        oa = _attention(qkv_a, qkv_a, qkv_a, batch=batch, seq=seq, n_kv=N_KV_A,
                        groups=N_HEADS_A // N_KV_A, dk=HEAD_DIM_A, dv=HEAD_DIM_A,
                        q_col0=0, k_col0=n_qa, v_col0=n_qa + n_kva, tq=256, kc=512)
        ob = _attention(q_b, k_b, v_b, batch=batch, seq=seq, n_kv=N_HEADS_B,
                        groups=1, dk=QK_PAD_B, dv=V_DIM_B,
                        q_col0=0, k_col0=0, v_col0=0, tq=256, kc=512)

        merged = _merge(u, oa, ob, w_a_out[i], w_b_out[i], w_gates, tm=512, tn=512)
        h1, un1 = _resid_mm(merged, w_o[i], h, g_ffn[i][None], tm=512, tn=512)
        act = _ffn_up(un1, w_gate_up[i], tm=1024, tn=512)
        h2, un2 = _resid_mm(act, w_down[i], h1, g_ple[i][None], tm=512, tn=256)
        h = _ple(un2, w_ple_gate[i], h2, p[i].reshape(t, -1), w_ple_proj[i],
                 g_ple_post[i][None], g_final[None], tm=512, tn=512,
                 final_norm=(i == depth - 1))
    return h.reshape(batch, seq, d)
```

```python
import functools
import math

import jax
import jax.numpy as jnp
from jax import lax
from jax.experimental import pallas as pl
from jax.experimental.pallas import tpu as pltpu

F32 = jnp.float32
BF16 = jnp.bfloat16

GRID_W = 64
ROPE_THETA = 10000.0
RMS_EPS = 1e-6

N_HEADS_A = 8
N_KV_A = 2
HEAD_DIM_A = 128
N_HEADS_B = 8
NOPE_DIM_B = 128
ROPE_DIM_B = 64
V_DIM_B = 128
QK_DIM_B = NOPE_DIM_B + ROPE_DIM_B
QK_PAD_B = 256

LANES = 128
V7X_VMEM_BYTES = 64 * 1024 * 1024
LOG2E = math.log2(math.e)


def _vmem_limit(*block_bytes, scratch=0, temps=0):
    need = 2 * sum(block_bytes) + scratch + temps + (4 << 20)
    return int(min(need, V7X_VMEM_BYTES - (6 << 20)))


def _nbytes(shape, dtype):
    return math.prod(shape) * jnp.dtype(dtype).itemsize


def _rms(x32, gain_row):
    ms = jnp.mean(x32 * x32, axis=-1, keepdims=True)
    return x32 * lax.rsqrt(ms + RMS_EPS) * gain_row


def _swap_chunks(y, chunk):
    lane = lax.broadcasted_iota(jnp.int32, y.shape, 1)
    first = (lane % (2 * chunk)) < chunk
    return jnp.where(first, pltpu.roll(y, LANES - chunk, 1), pltpu.roll(y, chunk, 1))


def _rope(y, cos, sin_signed, chunk):
    return y * cos + _swap_chunks(y, chunk) * sin_signed


def _rope_tables(seq, rot_dim):
    rows = seq // GRID_W
    row = jnp.repeat(jnp.arange(rows, dtype=F32), GRID_W)
    col = jnp.tile(jnp.arange(GRID_W, dtype=F32), rows)
    half = rot_dim // 2
    inv = 1.0 / (ROPE_THETA ** (jnp.arange(0, half, 2, dtype=F32) / half))
    ar = row[:, None] * inv[None, :]
    ac = col[:, None] * inv[None, :]
    cos = jnp.concatenate([jnp.cos(ar), jnp.cos(ar), jnp.cos(ac), jnp.cos(ac)], axis=-1)
    sin = jnp.concatenate([-jnp.sin(ar), jnp.sin(ar), -jnp.sin(ac), jnp.sin(ac)], axis=-1)
    pad = LANES - rot_dim
    if pad:
        cos = jnp.concatenate([cos, jnp.ones((seq, pad), F32)], axis=-1)
        sin = jnp.concatenate([sin, jnp.zeros((seq, pad), F32)], axis=-1)
    return cos, sin


def _qkv_a_kernel(x_ref, g_ref, w_ref, qn_ref, kn_ref, cos_ref, sin_ref,
                  o_ref, u_ref, *, q_scale):
    j = pl.program_id(1)

    @pl.when(j == 0)
    def _():
        u_ref[...] = _rms(x_ref[...], g_ref[...]).astype(BF16)

    z = jnp.dot(u_ref[...], w_ref[...].astype(BF16), preferred_element_type=F32)
    cos = cos_ref[...]
    sin = sin_ref[...]
    hd = HEAD_DIM_A
    n_heads_tile = z.shape[1] // hd
    q_tiles = (N_HEADS_A * hd) // z.shape[1]

    @pl.when(j < q_tiles)
    def _():
        for h in range(n_heads_tile):
            y = _rms(z[:, h * hd:(h + 1) * hd], qn_ref[...])
            o_ref[:, h * hd:(h + 1) * hd] = (_rope(y, cos, sin, hd // 4) * q_scale).astype(BF16)

    @pl.when(j == q_tiles)
    def _():
        for h in range(N_KV_A):
            y = _rms(z[:, h * hd:(h + 1) * hd], kn_ref[...])
            o_ref[:, h * hd:(h + 1) * hd] = _rope(y, cos, sin, hd // 4).astype(BF16)
        o_ref[:, N_KV_A * hd:] = z[:, N_KV_A * hd:].astype(BF16)


def _qkv_a(x2, g_mix, w_in, q_norm, k_norm, cos, sin, *, seq, tm, tn):
    t, d = x2.shape
    n_out = (N_HEADS_A + 2 * N_KV_A) * HEAD_DIM_A
    assert tn == 2 * N_KV_A * HEAD_DIM_A and t % tm == 0 and seq % tm == 0
    sblocks = seq // tm
    q_scale = (HEAD_DIM_A ** -0.5) * LOG2E
    return pl.pallas_call(
        functools.partial(_qkv_a_kernel, q_scale=q_scale),
        grid=(t // tm, n_out // tn),
        in_specs=[
            pl.BlockSpec((tm, d), lambda i, j: (i, 0)),
            pl.BlockSpec((1, d), lambda i, j: (0, 0)),
            pl.BlockSpec((d, tn), lambda i, j: (0, j)),
            pl.BlockSpec((1, HEAD_DIM_A), lambda i, j: (0, 0)),
            pl.BlockSpec((1, HEAD_DIM_A), lambda i, j: (0, 0)),
            pl.BlockSpec((tm, LANES), lambda i, j: (i % sblocks, 0)),
            pl.BlockSpec((tm, LANES), lambda i, j: (i % sblocks, 0)),
        ],
        out_specs=[
            pl.BlockSpec((tm, tn), lambda i, j: (i, j)),
            pl.BlockSpec((tm, d), lambda i, j: (i, 0)),
        ],
        out_shape=[
            jax.ShapeDtypeStruct((t, n_out), BF16),
            jax.ShapeDtypeStruct((t, d), BF16),
        ],
        compiler_params=pltpu.CompilerParams(
            dimension_semantics=("parallel", "arbitrary"),
            vmem_limit_bytes=_vmem_limit(
                _nbytes((tm, d), F32), _nbytes((d, tn), F32), _nbytes((tm, tn), BF16),
                _nbytes((tm, d), BF16), 2 * _nbytes((tm, LANES), F32),
                temps=_nbytes((d, tn), BF16) + 3 * _nbytes((tm, tn), F32)),
        ),
        name="qkv_a",
    )(x2, g_mix, w_in, q_norm, k_norm, cos, sin)


def _lat_b_kernel(u_ref, w_ref, wkr_ref, gains_ref, cos_ref, sin_ref, lat_ref, kpe_ref):
    j = pl.program_id(1)
    u = u_ref[...]
    z = jnp.dot(u, w_ref[...].astype(BF16), preferred_element_type=F32)
    lat_ref[...] = _rms(z, gains_ref[...]).astype(BF16)

    @pl.when(j == 0)
    def _():
        kr = jnp.dot(u, wkr_ref[...].astype(BF16), preferred_element_type=F32)
        kpe_ref[...] = _rope(kr, cos_ref[...], sin_ref[...], ROPE_DIM_B // 4).astype(BF16)


def _lat_b(u, w_in, w_kr_pad, gains, cos, sin, *, seq, tm, col0):
    t, d = u.shape
    lora = gains.shape[-1]
    sblocks = seq // tm
    first = col0 // lora
    return pl.pallas_call(
        _lat_b_kernel,
        grid=(t // tm, 2),
        in_specs=[
            pl.BlockSpec((tm, d), lambda i, j: (i, 0)),
            pl.BlockSpec((d, lora), lambda i, j: (0, first + j)),
            pl.BlockSpec((d, LANES), lambda i, j: (0, 0)),
            pl.BlockSpec((None, 1, lora), lambda i, j: (j, 0, 0)),
            pl.BlockSpec((tm, LANES), lambda i, j: (i % sblocks, 0)),
            pl.BlockSpec((tm, LANES), lambda i, j: (i % sblocks, 0)),
        ],
        out_specs=[
            pl.BlockSpec((tm, lora), lambda i, j: (i, j)),
            pl.BlockSpec((tm, LANES), lambda i, j: (i, 0)),
        ],
        out_shape=[
            jax.ShapeDtypeStruct((t, 2 * lora), BF16),
            jax.ShapeDtypeStruct((t, LANES), BF16),
        ],
        compiler_params=pltpu.CompilerParams(
            dimension_semantics=("parallel", "arbitrary"),
            vmem_limit_bytes=_vmem_limit(
                _nbytes((tm, d), BF16), _nbytes((d, lora), F32), _nbytes((d, LANES), F32),
                _nbytes((tm, lora), BF16), 3 * _nbytes((tm, LANES), F32),
                temps=_nbytes((d, lora), BF16) + 3 * _nbytes((tm, lora), F32)),
        ),
        name="lat_b",
    )(u, w_in, w_kr_pad, gains, cos, sin)


def _qkv_b_kernel(lat_ref, wq_ref, wkv_ref, kpe_ref, cos_ref, sin_ref,
                  q_ref, k_ref, v_ref, *, q_scale):
    lora = wq_ref.shape[0]
    zq = jnp.dot(lat_ref[:, :lora], wq_ref[...].astype(BF16), preferred_element_type=F32)
    zkv = jnp.dot(lat_ref[:, lora:], wkv_ref[...].astype(BF16), preferred_element_type=F32)
    cos = cos_ref[...]
    sin = sin_ref[...]
    kpe = kpe_ref[...]
    for h in range(N_HEADS_B):
        c0 = h * QK_PAD_B
        q_ref[:, c0:c0 + LANES] = (zq[:, c0:c0 + LANES] * q_scale).astype(BF16)
        pe = _rope(zq[:, c0 + LANES:c0 + 2 * LANES], cos, sin, ROPE_DIM_B // 4)
        q_ref[:, c0 + LANES:c0 + 2 * LANES] = (pe * q_scale).astype(BF16)
        k_ref[:, c0:c0 + LANES] = zkv[:, c0:c0 + LANES].astype(BF16)
        k_ref[:, c0 + LANES:c0 + 2 * LANES] = kpe
        v_ref[:, h * V_DIM_B:(h + 1) * V_DIM_B] = zkv[:, c0 + LANES:c0 + 2 * LANES].astype(BF16)


def _qkv_b(lat, wq_pad, w_kv_b, kpe, cos, sin, *, seq, tm):
    t = lat.shape[0]
    lora = wq_pad.shape[0]
    nq = wq_pad.shape[1]
    nkv = w_kv_b.shape[1]
    nv = N_HEADS_B * V_DIM_B
    sblocks = seq // tm
    q_scale = (QK_DIM_B ** -0.5) * LOG2E
    return pl.pallas_call(
        functools.partial(_qkv_b_kernel, q_scale=q_scale),
        grid=(t // tm,),
        in_specs=[
            pl.BlockSpec((tm, 2 * lora), lambda i: (i, 0)),
            pl.BlockSpec((lora, nq), lambda i: (0, 0)),
            pl.BlockSpec((lora, nkv), lambda i: (0, 0)),
            pl.BlockSpec((tm, LANES), lambda i: (i, 0)),
            pl.BlockSpec((tm, LANES), lambda i: (i % sblocks, 0)),
            pl.BlockSpec((tm, LANES), lambda i: (i % sblocks, 0)),
        ],
        out_specs=[
            pl.BlockSpec((tm, nq), lambda i: (i, 0)),
            pl.BlockSpec((tm, nq), lambda i: (i, 0)),
            pl.BlockSpec((tm, nv), lambda i: (i, 0)),
        ],
        out_shape=[
            jax.ShapeDtypeStruct((t, nq), BF16),
            jax.ShapeDtypeStruct((t, nq), BF16),
            jax.ShapeDtypeStruct((t, nv), BF16),
        ],
        compiler_params=pltpu.CompilerParams(
            dimension_semantics=("parallel",),
            vmem_limit_bytes=_vmem_limit(
                _nbytes((tm, 2 * lora), BF16), _nbytes((lora, nq), F32), _nbytes((lora, nkv), F32),
                2 * _nbytes((tm, nq), BF16), _nbytes((tm, nv), BF16), 3 * _nbytes((tm, LANES), F32),
                temps=2 * _nbytes((lora, nq), BF16) + 3 * _nbytes((tm, nq), F32)),
        ),
        name="qkv_b",
    )(lat, wq_pad, w_kv_b, kpe, cos, sin)


_L_SAFE = 2.0 ** -64


def _tree_rows8(x, op):
    while x.shape[0] > 8:
        half = x.shape[0] // 2
        x = op(x[:half], x[half:])
    return x


def _attn_kernel(q_ref, k_ref, v_ref, o_ref, kaug_ref, vt_ref, kmax_ref, *,
                 groups, dk, dv, kc, sq, aug_lane, unroll):
    seq = k_ref.shape[0]
    tq = q_ref.shape[0]
    dka = kaug_ref.shape[1]
    append_tile = aug_lane == dk

    @pl.when(pl.program_id(2) == 0)
    def _():
        kf = k_ref[...].astype(F32)
        kmax2 = jnp.max(jnp.sum(kf * kf, axis=1, keepdims=True), axis=0, keepdims=True)
        kmax_ref[...] = jnp.broadcast_to(kmax2, kmax_ref.shape)
        if append_tile:
            kaug_ref[:, :dk] = k_ref[...]
            lane = lax.broadcasted_iota(jnp.int32, (seq, dka - dk), 1)
            kaug_ref[:, dk:] = jnp.where(lane == 0, 1.0, 0.0).astype(BF16)
        else:
            lane = lax.broadcasted_iota(jnp.int32, (seq, dk), 1)
            kaug_ref[...] = jnp.where(lane == aug_lane, 1.0, kf).astype(BF16)
        vt_ref[...] = v_ref[...].astype(F32).T.astype(BF16)

    streams = [(r, g) for r in range(tq // sq) for g in range(groups)]

    def q_aug(r, g, shift_col):
        q = q_ref[r * sq:(r + 1) * sq, g * dk:(g + 1) * dk]
        neg = -shift_col
        if append_tile:
            lane = lax.broadcasted_iota(jnp.int32, (sq, dka - dk), 1)
            tile = jnp.where(lane == 0, neg, 0.0).astype(BF16)
            return jnp.concatenate([q, tile], axis=1)
        lane = lax.broadcasted_iota(jnp.int32, (sq, dk), 1)
        return jnp.where(lane == aug_lane, neg, q.astype(F32)).astype(BF16)

    def run(qas):
        def body(c, carry):
            start = pl.multiple_of(c * kc, kc)
            ks = kaug_ref[pl.ds(start, kc), :]
            vt = vt_ref[:, pl.ds(start, kc)]
            out = []
            for qa, (acc, l8) in zip(qas, carry):
                s = lax.dot_general(ks, qa, (((1,), (1,)), ((), ())),
                                    preferred_element_type=F32)
                p = jnp.exp2(s)
                out.append((acc + jnp.dot(vt, p.astype(BF16), preferred_element_type=F32),
                            l8 + _tree_rows8(p, jnp.add)))
            return tuple(out)

        init = tuple((jnp.zeros((dv, sq), F32), jnp.zeros((8, sq), F32)) for _ in qas)
        return lax.fori_loop(0, seq // kc, body, init, unroll=unroll)

    def write_out(results):
        for (r, g), (acc, l8) in zip(streams, results):
            l = jnp.sum(l8, axis=0, keepdims=True)
            o_ref[r * sq:(r + 1) * sq, g * dv:(g + 1) * dv] = (acc * (1.0 / l)).T.astype(BF16)

    kmax2 = kmax_ref[0:1, 0:1]
    bound_qas = []
    for r, g in streams:
        qf = q_ref[r * sq:(r + 1) * sq, g * dk:(g + 1) * dk].astype(F32)
        shift = jnp.sqrt(jnp.sum(qf * qf, axis=1, keepdims=True) * kmax2)
        bound_qas.append(q_aug(r, g, shift))
    fast = run(bound_qas)
    write_out(fast)

    l_min = None
    for _, l8 in fast:
        cur = jnp.min(jnp.sum(l8, axis=0, keepdims=True))
        l_min = cur if l_min is None else jnp.minimum(l_min, cur)

    @pl.when(jnp.logical_not(l_min >= _L_SAFE))
    def _():
        zero_qas = [q_aug(r, g, jnp.zeros((sq, 1), F32)) for r, g in streams]

        def max_body(c, ms):
            start = pl.multiple_of(c * kc, kc)
            ks = kaug_ref[pl.ds(start, kc), :]
            return tuple(
                jnp.maximum(m, jnp.max(
                    lax.dot_general(qa, ks, (((1,), (1,)), ((), ())),
                                    preferred_element_type=F32), axis=1, keepdims=True))
                for qa, m in zip(zero_qas, ms))

        ms = lax.fori_loop(0, seq // kc, max_body,
                           tuple(jnp.full((sq, 1), -jnp.inf, F32) for _ in streams))
        write_out(run([q_aug(r, g, m) for (r, g), m in zip(streams, ms)]))


def _attention(q_arr, k_arr, v_arr, *, batch, seq, n_kv, groups, dk, dv,
               q_col0, k_col0, v_col0, tq, kc, sq, aug_lane, unroll):
    t = batch * seq
    nq = seq // tq
    qb0 = q_col0 // (groups * dk)
    kb0 = k_col0 // dk
    vb0 = v_col0 // dv
    dka = dk + LANES if aug_lane == dk else dk
    assert aug_lane <= dk and tq % sq == 0 and seq % kc == 0
    return pl.pallas_call(
        functools.partial(_attn_kernel, groups=groups, dk=dk, dv=dv, kc=kc, sq=sq,
                          aug_lane=aug_lane, unroll=unroll),
        grid=(batch, n_kv, nq),
        in_specs=[
            pl.BlockSpec((tq, groups * dk), lambda b, h, i: (b * nq + i, qb0 + h)),
            pl.BlockSpec((seq, dk), lambda b, h, i: (b, kb0 + h)),
            pl.BlockSpec((seq, dv), lambda b, h, i: (b, vb0 + h)),
        ],
        out_specs=pl.BlockSpec((tq, groups * dv), lambda b, h, i: (b * nq + i, h)),
        out_shape=jax.ShapeDtypeStruct((t, n_kv * groups * dv), BF16),
        scratch_shapes=[pltpu.VMEM((seq, dka), BF16), pltpu.VMEM((dv, seq), BF16),
                        pltpu.VMEM((8, LANES), F32)],
        compiler_params=pltpu.CompilerParams(
            dimension_semantics=("parallel", "parallel", "arbitrary"),
            vmem_limit_bytes=_vmem_limit(
                _nbytes((tq, groups * dk), BF16), _nbytes((seq, dk), BF16),
                _nbytes((seq, dv), BF16), _nbytes((tq, groups * dv), BF16),
                scratch=_nbytes((dv, seq), BF16) + _nbytes((seq, dka), BF16),
                temps=2 * _nbytes((seq, dk), F32) + 2 * _nbytes((seq, dv), F32)
                + 6 * unroll * groups * _nbytes((kc, tq), F32)),
        ),
        name="attention",
    )(q_arr, k_arr, v_arr)


def _merge_kernel(u_ref, oa_ref, ob_ref, wa_ref, wb_ref, wga_ref, wgb_ref, o_ref):
    u = u_ref[...]
    ya = jnp.dot(oa_ref[...], wa_ref[...].astype(BF16), preferred_element_type=F32)
    yb = jnp.dot(ob_ref[...], wb_ref[...].astype(BF16), preferred_element_type=F32)
    ga = jnp.dot(u, wga_ref[...].astype(BF16), preferred_element_type=F32)
    gb = jnp.dot(u, wgb_ref[...].astype(BF16), preferred_element_type=F32)
    o_ref[...] = (jax.nn.sigmoid(ga) * ya + jax.nn.sigmoid(gb) * yb).astype(BF16)


def _merge(u, oa, ob, w_a_out, w_b_out, w_gates, *, tm, tn):
    t, d = u.shape
    ka = oa.shape[1]
    kb = ob.shape[1]
    nj = d // tn
    return pl.pallas_call(
        _merge_kernel,
        grid=(t // tm, nj),
        in_specs=[
            pl.BlockSpec((tm, d), lambda i, j: (i, 0)),
            pl.BlockSpec((tm, ka), lambda i, j: (i, 0)),
            pl.BlockSpec((tm, kb), lambda i, j: (i, 0)),
            pl.BlockSpec((ka, tn), lambda i, j: (0, j)),
            pl.BlockSpec((kb, tn), lambda i, j: (0, j)),
            pl.BlockSpec((d, tn), lambda i, j: (0, j)),
            pl.BlockSpec((d, tn), lambda i, j: (0, nj + j)),
        ],
        out_specs=pl.BlockSpec((tm, tn), lambda i, j: (i, j)),
        out_shape=jax.ShapeDtypeStruct((t, d), BF16),
        compiler_params=pltpu.CompilerParams(
            dimension_semantics=("parallel", "arbitrary"),
            vmem_limit_bytes=_vmem_limit(
                _nbytes((tm, d), BF16), _nbytes((tm, ka), BF16), _nbytes((tm, kb), BF16),
                _nbytes((ka, tn), F32), _nbytes((kb, tn), F32), 2 * _nbytes((d, tn), F32),
                _nbytes((tm, tn), BF16),
                temps=_nbytes((ka + kb + 2 * d, tn), BF16) + 6 * _nbytes((tm, tn), F32)),
        ),
        name="merge",
    )(u, oa, ob, w_a_out, w_b_out, w_gates, w_gates)


def _resid_mm_kernel(lhs_ref, w_ref, resid_ref, g_ref, h_ref, un_ref, *, tn, nj):
    j = pl.program_id(1)
    z = jnp.dot(lhs_ref[...], w_ref[...].astype(BF16), preferred_element_type=F32)
    val = resid_ref[...] + z
    for jj in range(nj):
        @pl.when(j == jj)
        def _(jj=jj):
            h_ref[:, jj * tn:(jj + 1) * tn] = val

    @pl.when(j == nj - 1)
    def _():
        un_ref[...] = _rms(h_ref[...], g_ref[...]).astype(BF16)


def _resid_mm(lhs, w, resid, gain, *, tm, tn):
    t, k = lhs.shape
    d = w.shape[1]
    nj = d // tn
    return pl.pallas_call(
        functools.partial(_resid_mm_kernel, tn=tn, nj=nj),
        grid=(t // tm, nj),
        in_specs=[
            pl.BlockSpec((tm, k), lambda i, j: (i, 0)),
            pl.BlockSpec((k, tn), lambda i, j: (0, j)),
            pl.BlockSpec((tm, tn), lambda i, j: (i, j)),
            pl.BlockSpec((1, d), lambda i, j: (0, 0)),
        ],
        out_specs=[
            pl.BlockSpec((tm, d), lambda i, j: (i, 0)),
            pl.BlockSpec((tm, d), lambda i, j: (i, 0)),
        ],
        out_shape=[
            jax.ShapeDtypeStruct((t, d), F32),
            jax.ShapeDtypeStruct((t, d), BF16),
        ],
        compiler_params=pltpu.CompilerParams(
            dimension_semantics=("parallel", "arbitrary"),
            vmem_limit_bytes=_vmem_limit(
                _nbytes((tm, k), BF16), _nbytes((k, tn), F32), _nbytes((tm, tn), F32),
                _nbytes((tm, d), F32), _nbytes((tm, d), BF16),
                temps=_nbytes((k, tn), BF16) + 2 * _nbytes((tm, tn), F32)
                + 2 * _nbytes((tm, d), F32)),
        ),
        name="resid_mm",
    )(lhs, w, resid, gain)


def _ffn_up_kernel(un_ref, wg_ref, wu_ref, o_ref):
    un = un_ref[...]
    g = jnp.dot(un, wg_ref[...].astype(BF16), preferred_element_type=F32)
    u = jnp.dot(un, wu_ref[...].astype(BF16), preferred_element_type=F32)
    o_ref[...] = (jax.nn.silu(g) * u).astype(BF16)


def _ffn_up(un, w_gate_up, *, tm, tn):
    t, d = un.shape
    d_ff = w_gate_up.shape[1] // 2
    nj = d_ff // tn
    return pl.pallas_call(
        _ffn_up_kernel,
        grid=(t // tm, nj),
        in_specs=[
            pl.BlockSpec((tm, d), lambda i, j: (i, 0)),
            pl.BlockSpec((d, tn), lambda i, j: (0, j)),
            pl.BlockSpec((d, tn), lambda i, j: (0, nj + j)),
        ],
        out_specs=pl.BlockSpec((tm, tn), lambda i, j: (i, j)),
        out_shape=jax.ShapeDtypeStruct((t, d_ff), BF16),
        compiler_params=pltpu.CompilerParams(
            dimension_semantics=("parallel", "arbitrary"),
            vmem_limit_bytes=_vmem_limit(
                _nbytes((tm, d), BF16), 2 * _nbytes((d, tn), F32), _nbytes((tm, tn), BF16),
                temps=2 * _nbytes((d, tn), BF16) + 4 * _nbytes((tm, tn), F32)),
        ),
        name="ffn_up",
    )(un, w_gate_up, w_gate_up)


def _ple_kernel(un_ref, w_ref, h_ref, p_ref, wp_ref, gpost_ref, gfin_ref,
                o_ref, e_ref, *, tn, nj, final_norm):
    j = pl.program_id(1)

    @pl.when(j == 0)
    def _():
        e = jnp.dot(p_ref[...].astype(BF16), wp_ref[...].astype(BF16),
                    preferred_element_type=F32)
        e_ref[...] = _rms(e, gpost_ref[...])

    gate = jax.nn.sigmoid(
        jnp.dot(un_ref[...], w_ref[...].astype(BF16), preferred_element_type=F32))
    for jj in range(nj):
        @pl.when(j == jj)
        def _(jj=jj):
            o_ref[:, jj * tn:(jj + 1) * tn] = h_ref[...] + gate * e_ref[:, jj * tn:(jj + 1) * tn]

    if final_norm:
        @pl.when(j == nj - 1)
        def _():
            o_ref[...] = _rms(o_ref[...], gfin_ref[...])


def _ple(un, w_gate, h, p2, w_proj, g_post, g_final, *, tm, tn, final_norm):
    t, d = un.shape
    pd = p2.shape[1]
    nj = d // tn
    return pl.pallas_call(
        functools.partial(_ple_kernel, tn=tn, nj=nj, final_norm=final_norm),
        grid=(t // tm, nj),
        in_specs=[
            pl.BlockSpec((tm, d), lambda i, j: (i, 0)),
            pl.BlockSpec((d, tn), lambda i, j: (0, j)),
            pl.BlockSpec((tm, tn), lambda i, j: (i, j)),
            pl.BlockSpec((tm, pd), lambda i, j: (i, 0)),
            pl.BlockSpec((pd, d), lambda i, j: (0, 0)),
            pl.BlockSpec((1, d), lambda i, j: (0, 0)),
            pl.BlockSpec((1, d), lambda i, j: (0, 0)),
        ],
        out_specs=pl.BlockSpec((tm, d), lambda i, j: (i, 0)),
        out_shape=jax.ShapeDtypeStruct((t, d), F32),
        scratch_shapes=[pltpu.VMEM((tm, d), F32)],
        compiler_params=pltpu.CompilerParams(
            dimension_semantics=("parallel", "arbitrary"),
            vmem_limit_bytes=_vmem_limit(
                _nbytes((tm, d), BF16), _nbytes((d, tn), F32), _nbytes((tm, tn), F32),
                _nbytes((tm, pd), F32), _nbytes((pd, d), F32), _nbytes((tm, d), F32),
                scratch=_nbytes((tm, d), F32),
                temps=_nbytes((d, tn), BF16) + _nbytes((pd, d), BF16)
                + 2 * _nbytes((tm, tn), F32) + 2 * _nbytes((tm, d), F32)),
        ),
        name="ple_final",
    )(un, w_gate, h, p2, w_proj, g_post, g_final)


def kernel(x, p, g_mix, w_in, q_norm_a, k_norm_a, q_a_norm, w_q_b, kv_a_norm, w_kv_b,
           w_a_out, w_b_out, w_o, g_ffn, w_gate_up, w_down, g_ple, w_ple_gate,
           w_ple_proj, g_ple_post, g_final):
    batch, seq, d = x.shape
    depth = w_in.shape[0]
    t = batch * seq
    q_lora = q_a_norm.shape[-1]
    kv_lora = kv_a_norm.shape[-1]
    assert q_lora == kv_lora
    n_qa = N_HEADS_A * HEAD_DIM_A
    n_kva = N_KV_A * HEAD_DIM_A
    lat0 = n_qa + 2 * n_kva
    kr0 = lat0 + q_lora + kv_lora
    gate0 = kr0 + ROPE_DIM_B

    cos_a, sin_a = _rope_tables(seq, HEAD_DIM_A)
    cos_b, sin_b = _rope_tables(seq, ROPE_DIM_B)

    h = x.reshape(t, d)
    for i in range(depth):
        wi = w_in[i]
        w_kr_pad = jnp.pad(wi[:, kr0:gate0], ((0, 0), (0, LANES - ROPE_DIM_B)))
        w_gates = wi[:, gate0:]
        wq_pad = jnp.pad(
            w_q_b[i].reshape(q_lora, N_HEADS_B, QK_DIM_B),
            ((0, 0), (0, 0), (0, QK_PAD_B - QK_DIM_B))).reshape(q_lora, N_HEADS_B * QK_PAD_B)
        lat_gains = jnp.stack([q_a_norm[i], kv_a_norm[i]])[:, None, :]

        qkv_a, u = _qkv_a(h, g_mix[i][None], wi, q_norm_a[i][None], k_norm_a[i][None],
                          cos_a, sin_a, seq=seq, tm=512, tn=2 * n_kva)
        lat, kpe = _lat_b(u, wi, w_kr_pad, lat_gains, cos_b, sin_b, seq=seq, tm=512, col0=lat0)
        q_b, k_b, v_b = _qkv_b(lat, wq_pad, w_kv_b[i], kpe, cos_b, sin_b, seq=seq, tm=512)

        oa = _attention(qkv_a, qkv_a, qkv_a, batch=batch, seq=seq, n_kv=N_KV_A,
                        groups=N_HEADS_A // N_KV_A, dk=HEAD_DIM_A, dv=HEAD_DIM_A,
                        q_col0=0, k_col0=n_qa, v_col0=n_qa + n_kva, tq=512, kc=512, sq=512,
                        aug_lane=HEAD_DIM_A, unroll=1)
        ob = _attention(q_b, k_b, v_b, batch=batch, seq=seq, n_kv=N_HEADS_B,
                        groups=1, dk=QK_PAD_B, dv=V_DIM_B,
                        q_col0=0, k_col0=0, v_col0=0, tq=1024, kc=512, sq=512,
                        aug_lane=QK_DIM_B, unroll=1)

        merged = _merge(u, oa, ob, w_a_out[i], w_b_out[i], w_gates, tm=512, tn=512)
        h1, un1 = _resid_mm(merged, w_o[i], h, g_ffn[i][None], tm=512, tn=512)
        act = _ffn_up(un1, w_gate_up[i], tm=1024, tn=512)
        h2, un2 = _resid_mm(act, w_down[i], h1, g_ple[i][None], tm=512, tn=256)
        h = _ple(un2, w_ple_gate[i], h2, p[i].reshape(t, -1), w_ple_proj[i],
                 g_ple_post[i][None], g_final[None], tm=512, tn=512,
                 final_norm=(i == depth - 1))
    return h.reshape(batch, seq, d)
```

```python
import functools
import math

import jax
import jax.numpy as jnp
from jax import lax
from jax.experimental import pallas as pl
from jax.experimental.pallas import tpu as pltpu

F32 = jnp.float32
BF16 = jnp.bfloat16

GRID_W = 64
ROPE_THETA = 10000.0
RMS_EPS = 1e-6

N_HEADS_A = 8
N_KV_A = 2
HEAD_DIM_A = 128
N_HEADS_B = 8
NOPE_DIM_B = 128
ROPE_DIM_B = 64
V_DIM_B = 128
QK_DIM_B = NOPE_DIM_B + ROPE_DIM_B
QK_PAD_B = 256

LANES = 128
V7X_VMEM_BYTES = 64 * 1024 * 1024
LOG2E = math.log2(math.e)


def _vmem_limit(*block_bytes, scratch=0, temps=0):
    need = 2 * sum(block_bytes) + scratch + temps + (4 << 20)
    return int(min(need, V7X_VMEM_BYTES - (6 << 20)))


def _nbytes(shape, dtype):
    return math.prod(shape) * jnp.dtype(dtype).itemsize


def _rms(x32, gain_row):
    ms = jnp.mean(x32 * x32, axis=-1, keepdims=True)
    return x32 * lax.rsqrt(ms + RMS_EPS) * gain_row


def _swap_chunks(y, chunk):
    lane = lax.broadcasted_iota(jnp.int32, y.shape, 1)
    first = (lane % (2 * chunk)) < chunk
    return jnp.where(first, pltpu.roll(y, LANES - chunk, 1), pltpu.roll(y, chunk, 1))


def _rope(y, cos, sin_signed, chunk):
    return y * cos + _swap_chunks(y, chunk) * sin_signed


def _rope_tables(seq, rot_dim):
    rows = seq // GRID_W
    row = jnp.repeat(jnp.arange(rows, dtype=F32), GRID_W)
    col = jnp.tile(jnp.arange(GRID_W, dtype=F32), rows)
    half = rot_dim // 2
    inv = 1.0 / (ROPE_THETA ** (jnp.arange(0, half, 2, dtype=F32) / half))
    ar = row[:, None] * inv[None, :]
    ac = col[:, None] * inv[None, :]
    cos = jnp.concatenate([jnp.cos(ar), jnp.cos(ar), jnp.cos(ac), jnp.cos(ac)], axis=-1)
    sin = jnp.concatenate([-jnp.sin(ar), jnp.sin(ar), -jnp.sin(ac), jnp.sin(ac)], axis=-1)
    pad = LANES - rot_dim
    if pad:
        cos = jnp.concatenate([cos, jnp.ones((seq, pad), F32)], axis=-1)
        sin = jnp.concatenate([sin, jnp.zeros((seq, pad), F32)], axis=-1)
    return cos, sin


def _once(i, j, nj):
    return jnp.where(i == 0, j, nj - 1)


def _cast_kernel(w_ref, o_ref):
    o_ref[...] = w_ref[...].astype(BF16)


def _cast_bf16(w, *, tr):
    r, c = w.shape
    return pl.pallas_call(
        _cast_kernel,
        grid=(r // tr,),
        in_specs=[pl.BlockSpec((tr, c), lambda i: (i, 0))],
        out_specs=pl.BlockSpec((tr, c), lambda i: (i, 0)),
        out_shape=jax.ShapeDtypeStruct((r, c), BF16),
        compiler_params=pltpu.CompilerParams(
            dimension_semantics=("parallel",),
            vmem_limit_bytes=_vmem_limit(_nbytes((tr, c), F32), _nbytes((tr, c), BF16))),
        name="cast_bf16",
    )(w)


def _gate_weights_kernel(a_ref, b_ref, o_ref, *, shift):
    tn = a_ref.shape[1]
    lane = lax.broadcasted_iota(jnp.int32, a_ref.shape, 1)
    ra = pltpu.roll(a_ref[...], tn - shift, 1)
    rb = pltpu.roll(b_ref[...], tn - shift, 1)
    o_ref[...] = jnp.where(lane < tn - shift, ra, rb).astype(BF16)


def _gate_weights(w_in, *, col0, n_cols, tn):
    d = w_in.shape[0]
    first, shift = divmod(col0, tn)
    return pl.pallas_call(
        functools.partial(_gate_weights_kernel, shift=shift),
        grid=(n_cols // tn,),
        in_specs=[
            pl.BlockSpec((d, tn), lambda j: (0, first + j)),
            pl.BlockSpec((d, tn), lambda j: (0, first + j + 1)),
        ],
        out_specs=pl.BlockSpec((d, tn), lambda j: (0, j)),
        out_shape=jax.ShapeDtypeStruct((d, n_cols), BF16),
        compiler_params=pltpu.CompilerParams(
            dimension_semantics=("parallel",),
            vmem_limit_bytes=_vmem_limit(
                2 * _nbytes((d, tn), F32), _nbytes((d, tn), BF16),
                temps=3 * _nbytes((d, tn), F32))),
        name="gate_weights",
    )(w_in, w_in)


def _inproj_kernel(x_ref, g_ref, w_ref, wkr_ref, qn_ref, kn_ref, lg_ref,
                   cosa_ref, sina_ref, cosb_ref, sinb_ref,
                   qkv_ref, lat_ref, kpe_ref, u_ref, wbf_ref, wkrbf_ref,
                   *, q_scale, q_tiles, nj):
    i = pl.program_id(0)
    j = pl.program_id(1)
    hd = HEAD_DIM_A

    @pl.when(i == 0)
    def _():
        wbf_ref[j] = w_ref[...].astype(BF16)

    @pl.when(jnp.logical_and(i == 0, j == 0))
    def _():
        lane = lax.broadcasted_iota(jnp.int32, wkr_ref.shape, 1)
        wkrbf_ref[...] = jnp.where(lane < ROPE_DIM_B, wkr_ref[...], 0.0).astype(BF16)

    @pl.when(j == 0)
    def _():
        u_ref[...] = _rms(x_ref[...], g_ref[...]).astype(BF16)

    z = jnp.dot(u_ref[...], wbf_ref[j], preferred_element_type=F32)

    @pl.when(j < q_tiles)
    def _():
        cos = cosa_ref[...]
        sin = sina_ref[...]
        for h in range(z.shape[1] // hd):
            y = _rms(z[:, h * hd:(h + 1) * hd], qn_ref[...])
            qkv_ref[:, h * hd:(h + 1) * hd] = (_rope(y, cos, sin, hd // 4) * q_scale).astype(BF16)

    @pl.when(j == q_tiles)
    def _():
        cos = cosa_ref[...]
        sin = sina_ref[...]
        for h in range(N_KV_A):
            y = _rms(z[:, h * hd:(h + 1) * hd], kn_ref[...])
            qkv_ref[:, h * hd:(h + 1) * hd] = _rope(y, cos, sin, hd // 4).astype(BF16)
        qkv_ref[:, N_KV_A * hd:] = z[:, N_KV_A * hd:].astype(BF16)

    @pl.when(j > q_tiles)
    def _():
        lat_ref[...] = _rms(z, lg_ref[...]).astype(BF16)

    @pl.when(j == nj - 1)
    def _():
        kr = jnp.dot(u_ref[...], wkrbf_ref[...], preferred_element_type=F32)
        kpe_ref[...] = _rope(kr, cosb_ref[...], sinb_ref[...], ROPE_DIM_B // 4).astype(BF16)


def _inproj(x2, g_mix, w_in, q_norm, k_norm, lat_gains, cos_a, sin_a, cos_b, sin_b,
            *, seq, tm, tn, kr_col0):
    t, d = x2.shape
    lora = lat_gains.shape[-1]
    n_qkv = (N_HEADS_A + 2 * N_KV_A) * HEAD_DIM_A
    assert tn == 2 * N_KV_A * HEAD_DIM_A and tn == lora and kr_col0 % LANES == 0
    assert t % tm == 0 and seq % tm == 0
    q_tiles = (N_HEADS_A * HEAD_DIM_A) // tn
    nj = q_tiles + 3
    sblocks = seq // tm
    q_scale = (HEAD_DIM_A ** -0.5) * LOG2E
    return pl.pallas_call(
        functools.partial(_inproj_kernel, q_scale=q_scale, q_tiles=q_tiles, nj=nj),
        grid=(t // tm, nj),
        in_specs=[
            pl.BlockSpec((tm, d), lambda i, j: (i, 0)),
            pl.BlockSpec((1, d), lambda i, j: (0, 0)),
            pl.BlockSpec((d, tn), lambda i, j: (0, _once(i, j, nj))),
            pl.BlockSpec((d, LANES), lambda i, j: (0, kr_col0 // LANES)),
            pl.BlockSpec((1, HEAD_DIM_A), lambda i, j: (0, 0)),
            pl.BlockSpec((1, HEAD_DIM_A), lambda i, j: (0, 0)),
            pl.BlockSpec((None, 1, lora), lambda i, j: (jnp.maximum(j - q_tiles - 1, 0), 0, 0)),
            pl.BlockSpec((tm, LANES), lambda i, j: (i % sblocks, 0)),
            pl.BlockSpec((tm, LANES), lambda i, j: (i % sblocks, 0)),
            pl.BlockSpec((tm, LANES), lambda i, j: (i % sblocks, 0)),
            pl.BlockSpec((tm, LANES), lambda i, j: (i % sblocks, 0)),
        ],
        out_specs=[
            pl.BlockSpec((tm, tn), lambda i, j: (i, jnp.minimum(j, q_tiles))),
            pl.BlockSpec((tm, lora), lambda i, j: (i, jnp.maximum(j - q_tiles - 1, 0))),
            pl.BlockSpec((tm, LANES), lambda i, j: (i, 0)),
            pl.BlockSpec((tm, d), lambda i, j: (i, 0)),
        ],
        out_shape=[
            jax.ShapeDtypeStruct((t, n_qkv), BF16),
            jax.ShapeDtypeStruct((t, 2 * lora), BF16),
            jax.ShapeDtypeStruct((t, LANES), BF16),
            jax.ShapeDtypeStruct((t, d), BF16),
        ],
        scratch_shapes=[pltpu.VMEM((nj, d, tn), BF16), pltpu.VMEM((d, LANES), BF16)],
        compiler_params=pltpu.CompilerParams(
            dimension_semantics=("arbitrary", "arbitrary"),
            vmem_limit_bytes=_vmem_limit(
                _nbytes((tm, d), F32), _nbytes((d, tn), F32), _nbytes((d, LANES), F32),
                2 * _nbytes((tm, tn), BF16), _nbytes((tm, d), BF16), 5 * _nbytes((tm, LANES), F32),
                scratch=_nbytes((nj, d, tn), BF16) + _nbytes((d, LANES), BF16),
                temps=_nbytes((d, tn), BF16) + 4 * _nbytes((tm, tn), F32)),
        ),
        name="inproj",
    )(x2, g_mix, w_in, w_in, q_norm, k_norm, lat_gains, cos_a, sin_a, cos_b, sin_b)


def _qkv_b_kernel(lat_ref, wq_ref, wkv_ref, kpe_ref, cos_ref, sin_ref,
                  q_ref, k_ref, v_ref, wqbf_ref, wkvbf_ref, *, q_scale):
    @pl.when(pl.program_id(0) == 0)
    def _():
        wqbf_ref[...] = wq_ref[...].astype(BF16)
        wkvbf_ref[...] = wkv_ref[...].astype(BF16)

    lora = wq_ref.shape[0]
    zq = jnp.dot(lat_ref[:, :lora], wqbf_ref[...], preferred_element_type=F32)
    zkv = jnp.dot(lat_ref[:, lora:], wkvbf_ref[...], preferred_element_type=F32)
    cos = cos_ref[...]
    sin = sin_ref[...]
    kpe = kpe_ref[...]
    for h in range(N_HEADS_B):
        c0 = h * QK_PAD_B
        q_ref[:, c0:c0 + LANES] = (zq[:, c0:c0 + LANES] * q_scale).astype(BF16)
        pe = _rope(zq[:, c0 + LANES:c0 + 2 * LANES], cos, sin, ROPE_DIM_B // 4)
        q_ref[:, c0 + LANES:c0 + 2 * LANES] = (pe * q_scale).astype(BF16)
        k_ref[:, c0:c0 + LANES] = zkv[:, c0:c0 + LANES].astype(BF16)
        k_ref[:, c0 + LANES:c0 + 2 * LANES] = kpe
        v_ref[:, h * V_DIM_B:(h + 1) * V_DIM_B] = zkv[:, c0 + LANES:c0 + 2 * LANES].astype(BF16)


def _qkv_b(lat, wq_pad, w_kv_b, kpe, cos, sin, *, seq, tm):
    t = lat.shape[0]
    lora = wq_pad.shape[0]
    nq = wq_pad.shape[1]
    nkv = w_kv_b.shape[1]
    nv = N_HEADS_B * V_DIM_B
    sblocks = seq // tm
    q_scale = (QK_DIM_B ** -0.5) * LOG2E
    return pl.pallas_call(
        functools.partial(_qkv_b_kernel, q_scale=q_scale),
        grid=(t // tm,),
        in_specs=[
            pl.BlockSpec((tm, 2 * lora), lambda i: (i, 0)),
            pl.BlockSpec((lora, nq), lambda i: (0, 0)),
            pl.BlockSpec((lora, nkv), lambda i: (0, 0)),
            pl.BlockSpec((tm, LANES), lambda i: (i, 0)),
            pl.BlockSpec((tm, LANES), lambda i: (i % sblocks, 0)),
            pl.BlockSpec((tm, LANES), lambda i: (i % sblocks, 0)),
        ],
        out_specs=[
            pl.BlockSpec((tm, nq), lambda i: (i, 0)),
            pl.BlockSpec((tm, nq), lambda i: (i, 0)),
            pl.BlockSpec((tm, nv), lambda i: (i, 0)),
        ],
        out_shape=[
            jax.ShapeDtypeStruct((t, nq), BF16),
            jax.ShapeDtypeStruct((t, nq), BF16),
            jax.ShapeDtypeStruct((t, nv), BF16),
        ],
        scratch_shapes=[pltpu.VMEM((lora, nq), BF16), pltpu.VMEM((lora, nkv), BF16)],
        compiler_params=pltpu.CompilerParams(
            dimension_semantics=("arbitrary",),
            vmem_limit_bytes=_vmem_limit(
                _nbytes((tm, 2 * lora), BF16), _nbytes((lora, nq), F32), _nbytes((lora, nkv), F32),
                2 * _nbytes((tm, nq), BF16), _nbytes((tm, nv), BF16), 3 * _nbytes((tm, LANES), F32),
                scratch=_nbytes((lora, nq), BF16) + _nbytes((lora, nkv), BF16),
                temps=3 * _nbytes((tm, nq), F32)),
        ),
        name="qkv_b",
    )(lat, wq_pad, w_kv_b, kpe, cos, sin)


_L_SAFE = 2.0 ** -64


def _tree_rows8(x, op):
    while x.shape[0] > 8:
        half = x.shape[0] // 2
        x = op(x[:half], x[half:])
    return x


def _attn_kernel(q_ref, k_ref, v_ref, o_ref, kaug_ref, vt_ref, kmax_ref, *,
                 groups, dk, dv, kc, sq, aug_lane, unroll):
    seq = k_ref.shape[0]
    tq = q_ref.shape[0]
    dka = kaug_ref.shape[1]
    append_tile = aug_lane == dk

    @pl.when(pl.program_id(2) == 0)
    def _():
        kf = k_ref[...].astype(F32)
        kmax2 = jnp.max(jnp.sum(kf * kf, axis=1, keepdims=True), axis=0, keepdims=True)
        kmax_ref[...] = jnp.broadcast_to(kmax2, kmax_ref.shape)
        if append_tile:
            kaug_ref[:, :dk] = k_ref[...]
            lane = lax.broadcasted_iota(jnp.int32, (seq, dka - dk), 1)
            kaug_ref[:, dk:] = jnp.where(lane == 0, 1.0, 0.0).astype(BF16)
        else:
            lane = lax.broadcasted_iota(jnp.int32, (seq, dk), 1)
            kaug_ref[...] = jnp.where(lane == aug_lane, 1.0, kf).astype(BF16)
        vt_ref[...] = v_ref[...].astype(F32).T.astype(BF16)

    streams = [(r, g) for r in range(tq // sq) for g in range(groups)]

    def q_aug(r, g, shift_col):
        q = q_ref[r * sq:(r + 1) * sq, g * dk:(g + 1) * dk]
        neg = -shift_col
        if append_tile:
            lane = lax.broadcasted_iota(jnp.int32, (sq, dka - dk), 1)
            tile = jnp.where(lane == 0, neg, 0.0).astype(BF16)
            return jnp.concatenate([q, tile], axis=1)
        lane = lax.broadcasted_iota(jnp.int32, (sq, dk), 1)
        return jnp.where(lane == aug_lane, neg, q.astype(F32)).astype(BF16)

    def run(qas):
        def body(c, carry):
            start = pl.multiple_of(c * kc, kc)
            ks = kaug_ref[pl.ds(start, kc), :]
            vt = vt_ref[:, pl.ds(start, kc)]
            out = []
            for qa, (acc, l8) in zip(qas, carry):
                s = lax.dot_general(ks, qa, (((1,), (1,)), ((), ())),
                                    preferred_element_type=F32)
                p = jnp.exp2(s)
                out.append((acc + jnp.dot(vt, p.astype(BF16), preferred_element_type=F32),
                            l8 + _tree_rows8(p, jnp.add)))
            return tuple(out)

        init = tuple((jnp.zeros((dv, sq), F32), jnp.zeros((8, sq), F32)) for _ in qas)
        return lax.fori_loop(0, seq // kc, body, init, unroll=unroll)

    def write_out(results):
        for (r, g), (acc, l8) in zip(streams, results):
            l = jnp.sum(l8, axis=0, keepdims=True)
            o_ref[r * sq:(r + 1) * sq, g * dv:(g + 1) * dv] = (acc * (1.0 / l)).T.astype(BF16)

    kmax2 = kmax_ref[0:1, 0:1]
    bound_qas = []
    for r, g in streams:
        qf = q_ref[r * sq:(r + 1) * sq, g * dk:(g + 1) * dk].astype(F32)
        shift = jnp.sqrt(jnp.sum(qf * qf, axis=1, keepdims=True) * kmax2)
        bound_qas.append(q_aug(r, g, shift))
    fast = run(bound_qas)
    write_out(fast)

    l_min = None
    for _, l8 in fast:
        cur = jnp.min(jnp.sum(l8, axis=0, keepdims=True))
        l_min = cur if l_min is None else jnp.minimum(l_min, cur)

    @pl.when(jnp.logical_not(l_min >= _L_SAFE))
    def _():
        zero_qas = [q_aug(r, g, jnp.zeros((sq, 1), F32)) for r, g in streams]

        def max_body(c, ms):
            start = pl.multiple_of(c * kc, kc)
            ks = kaug_ref[pl.ds(start, kc), :]
            return tuple(
                jnp.maximum(m, jnp.max(
                    lax.dot_general(qa, ks, (((1,), (1,)), ((), ())),
                                    preferred_element_type=F32), axis=1, keepdims=True))
                for qa, m in zip(zero_qas, ms))

        ms = lax.fori_loop(0, seq // kc, max_body,
                           tuple(jnp.full((sq, 1), -jnp.inf, F32) for _ in streams))
        write_out(run([q_aug(r, g, m) for (r, g), m in zip(streams, ms)]))


def _attention(q_arr, k_arr, v_arr, *, batch, seq, n_kv, groups, dk, dv,
               q_col0, k_col0, v_col0, tq, kc, sq, aug_lane, unroll):
    t = batch * seq
    nq = seq // tq
    qb0 = q_col0 // (groups * dk)
    kb0 = k_col0 // dk
    vb0 = v_col0 // dv
    dka = dk + LANES if aug_lane == dk else dk
    assert aug_lane <= dk and tq % sq == 0 and seq % kc == 0
    return pl.pallas_call(
        functools.partial(_attn_kernel, groups=groups, dk=dk, dv=dv, kc=kc, sq=sq,
                          aug_lane=aug_lane, unroll=unroll),
        grid=(batch, n_kv, nq),
        in_specs=[
            pl.BlockSpec((tq, groups * dk), lambda b, h, i: (b * nq + i, qb0 + h)),
            pl.BlockSpec((seq, dk), lambda b, h, i: (b, kb0 + h)),
            pl.BlockSpec((seq, dv), lambda b, h, i: (b, vb0 + h)),
        ],
        out_specs=pl.BlockSpec((tq, groups * dv), lambda b, h, i: (b * nq + i, h)),
        out_shape=jax.ShapeDtypeStruct((t, n_kv * groups * dv), BF16),
        scratch_shapes=[pltpu.VMEM((seq, dka), BF16), pltpu.VMEM((dv, seq), BF16),
                        pltpu.VMEM((8, LANES), F32)],
        compiler_params=pltpu.CompilerParams(
            dimension_semantics=("parallel", "parallel", "arbitrary"),
            vmem_limit_bytes=_vmem_limit(
                _nbytes((tq, groups * dk), BF16), _nbytes((seq, dk), BF16),
                _nbytes((seq, dv), BF16), _nbytes((tq, groups * dv), BF16),
                scratch=_nbytes((dv, seq), BF16) + _nbytes((seq, dka), BF16),
                temps=2 * _nbytes((seq, dk), F32) + 2 * _nbytes((seq, dv), F32)
                + 6 * unroll * groups * _nbytes((kc, tq), F32)),
        ),
        name="attention",
    )(q_arr, k_arr, v_arr)


def _merge_kernel(u_ref, oa_ref, ob_ref, wa_ref, wb_ref, wga_ref, wgb_ref, o_ref,
                  wabf_ref, wbbf_ref):
    i = pl.program_id(0)
    j = pl.program_id(1)

    @pl.when(i == 0)
    def _():
        wabf_ref[j] = wa_ref[...].astype(BF16)
        wbbf_ref[j] = wb_ref[...].astype(BF16)

    u = u_ref[...]
    ya = jnp.dot(oa_ref[...], wabf_ref[j], preferred_element_type=F32)
    yb = jnp.dot(ob_ref[...], wbbf_ref[j], preferred_element_type=F32)
    ga = jnp.dot(u, wga_ref[...], preferred_element_type=F32)
    gb = jnp.dot(u, wgb_ref[...], preferred_element_type=F32)
    o_ref[...] = (jax.nn.sigmoid(ga) * ya + jax.nn.sigmoid(gb) * yb).astype(BF16)


def _merge(u, oa, ob, w_a_out, w_b_out, w_gates_bf, *, tm, tn):
    t, d = u.shape
    ka = oa.shape[1]
    kb = ob.shape[1]
    nj = d // tn
    return pl.pallas_call(
        _merge_kernel,
        grid=(t // tm, nj),
        in_specs=[
            pl.BlockSpec((tm, d), lambda i, j: (i, 0)),
            pl.BlockSpec((tm, ka), lambda i, j: (i, 0)),
            pl.BlockSpec((tm, kb), lambda i, j: (i, 0)),
            pl.BlockSpec((ka, tn), lambda i, j: (0, _once(i, j, nj))),
            pl.BlockSpec((kb, tn), lambda i, j: (0, _once(i, j, nj))),
            pl.BlockSpec((d, tn), lambda i, j: (0, j)),
            pl.BlockSpec((d, tn), lambda i, j: (0, nj + j)),
        ],
        out_specs=pl.BlockSpec((tm, tn), lambda i, j: (i, j)),
        out_shape=jax.ShapeDtypeStruct((t, d), BF16),
        scratch_shapes=[pltpu.VMEM((nj, ka, tn), BF16), pltpu.VMEM((nj, kb, tn), BF16)],
        compiler_params=pltpu.CompilerParams(
            dimension_semantics=("arbitrary", "arbitrary"),
            vmem_limit_bytes=_vmem_limit(
                _nbytes((tm, d), BF16), _nbytes((tm, ka), BF16), _nbytes((tm, kb), BF16),
                _nbytes((ka, tn), F32), _nbytes((kb, tn), F32), 2 * _nbytes((d, tn), BF16),
                _nbytes((tm, tn), BF16),
                scratch=_nbytes((nj, ka, tn), BF16) + _nbytes((nj, kb, tn), BF16),
                temps=_nbytes((ka + kb, tn), BF16) + 6 * _nbytes((tm, tn), F32)),
        ),
        name="merge",
    )(u, oa, ob, w_a_out, w_b_out, w_gates_bf, w_gates_bf)


def _resid_mm_kernel(lhs_ref, w_ref, resid_ref, g_ref, h_ref, un_ref, *scratch, tn, nj):
    i = pl.program_id(0)
    j = pl.program_id(1)
    if scratch:
        (wbf_ref,) = scratch

        @pl.when(i == 0)
        def _():
            wbf_ref[j] = w_ref[...].astype(BF16)

        w = wbf_ref[j]
    else:
        w = w_ref[...]
    val = resid_ref[...] + jnp.dot(lhs_ref[...], w, preferred_element_type=F32)
    for jj in range(nj):
        @pl.when(j == jj)
        def _(jj=jj):
            h_ref[:, jj * tn:(jj + 1) * tn] = val

    @pl.when(j == nj - 1)
    def _():
        un_ref[...] = _rms(h_ref[...], g_ref[...]).astype(BF16)


def _resid_mm(lhs, w, resid, gain, *, tm, tn):
    t, k = lhs.shape
    d = w.shape[1]
    nj = d // tn
    resident = w.dtype == F32
    if resident:
        w_spec = pl.BlockSpec((k, tn), lambda i, j: (0, _once(i, j, nj)))
        scratch_shapes = [pltpu.VMEM((nj, k, tn), BF16)]
        scratch_bytes = _nbytes((nj, k, tn), BF16)
    else:
        w_spec = pl.BlockSpec((k, tn), lambda i, j: (0, j))
        scratch_shapes = []
        scratch_bytes = 0
    return pl.pallas_call(
        functools.partial(_resid_mm_kernel, tn=tn, nj=nj),
        grid=(t // tm, nj),
        in_specs=[
            pl.BlockSpec((tm, k), lambda i, j: (i, 0)),
            w_spec,
            pl.BlockSpec((tm, tn), lambda i, j: (i, j)),
            pl.BlockSpec((1, d), lambda i, j: (0, 0)),
        ],
        out_specs=[
            pl.BlockSpec((tm, d), lambda i, j: (i, 0)),
            pl.BlockSpec((tm, d), lambda i, j: (i, 0)),
        ],
        out_shape=[
            jax.ShapeDtypeStruct((t, d), F32),
            jax.ShapeDtypeStruct((t, d), BF16),
        ],
        scratch_shapes=scratch_shapes,
        compiler_params=pltpu.CompilerParams(
            dimension_semantics=("arbitrary", "arbitrary"),
            vmem_limit_bytes=_vmem_limit(
                _nbytes((tm, k), BF16), _nbytes((k, tn), w.dtype), _nbytes((tm, tn), F32),
                _nbytes((tm, d), F32), _nbytes((tm, d), BF16),
                scratch=scratch_bytes,
                temps=_nbytes((k, tn), BF16) + 2 * _nbytes((tm, tn), F32)
                + 2 * _nbytes((tm, d), F32)),
        ),
        name="resid_mm",
    )(lhs, w, resid, gain)


def _ffn_up_kernel(un_ref, wg_ref, wu_ref, o_ref):
    un = un_ref[...]
    g = jnp.dot(un, wg_ref[...].astype(BF16), preferred_element_type=F32)
    u = jnp.dot(un, wu_ref[...].astype(BF16), preferred_element_type=F32)
    o_ref[...] = (jax.nn.silu(g) * u).astype(BF16)


def _ffn_up(un, w_gate_up, *, tm, tn):
    t, d = un.shape
    d_ff = w_gate_up.shape[1] // 2
    nj = d_ff // tn
    return pl.pallas_call(
        _ffn_up_kernel,
        grid=(t // tm, nj),
        in_specs=[
            pl.BlockSpec((tm, d), lambda i, j: (i, 0)),
            pl.BlockSpec((d, tn), lambda i, j: (0, j)),
            pl.BlockSpec((d, tn), lambda i, j: (0, nj + j)),
        ],
        out_specs=pl.BlockSpec((tm, tn), lambda i, j: (i, j)),
        out_shape=jax.ShapeDtypeStruct((t, d_ff), BF16),
        compiler_params=pltpu.CompilerParams(
            dimension_semantics=("parallel", "arbitrary"),
            vmem_limit_bytes=_vmem_limit(
                _nbytes((tm, d), BF16), 2 * _nbytes((d, tn), F32), _nbytes((tm, tn), BF16),
                temps=2 * _nbytes((d, tn), BF16) + 4 * _nbytes((tm, tn), F32)),
        ),
        name="ffn_up",
    )(un, w_gate_up, w_gate_up)


def _ple_kernel(un_ref, w_ref, h_ref, p_ref, wp_ref, gpost_ref, gfin_ref,
                o_ref, e_ref, wbf_ref, *, tn, nj, final_norm):
    i = pl.program_id(0)
    j = pl.program_id(1)

    @pl.when(i == 0)
    def _():
        wbf_ref[j] = w_ref[...].astype(BF16)

    @pl.when(j == 0)
    def _():
        e = jnp.dot(p_ref[...].astype(BF16), wp_ref[...].astype(BF16),
                    preferred_element_type=F32)
        e_ref[...] = _rms(e, gpost_ref[...])

    gate = jax.nn.sigmoid(jnp.dot(un_ref[...], wbf_ref[j], preferred_element_type=F32))
    for jj in range(nj):
        @pl.when(j == jj)
        def _(jj=jj):
            o_ref[:, jj * tn:(jj + 1) * tn] = h_ref[...] + gate * e_ref[:, jj * tn:(jj + 1) * tn]

    if final_norm:
        @pl.when(j == nj - 1)
        def _():
            o_ref[...] = _rms(o_ref[...], gfin_ref[...])


def _ple(un, w_gate, h, p2, w_proj, g_post, g_final, *, tm, tn, final_norm):
    t, d = un.shape
    pd = p2.shape[1]
    nj = d // tn
    return pl.pallas_call(
        functools.partial(_ple_kernel, tn=tn, nj=nj, final_norm=final_norm),
        grid=(t // tm, nj),
        in_specs=[
            pl.BlockSpec((tm, d), lambda i, j: (i, 0)),
            pl.BlockSpec((d, tn), lambda i, j: (0, _once(i, j, nj))),
            pl.BlockSpec((tm, tn), lambda i, j: (i, j)),
            pl.BlockSpec((tm, pd), lambda i, j: (i, 0)),
            pl.BlockSpec((pd, d), lambda i, j: (0, 0)),
            pl.BlockSpec((1, d), lambda i, j: (0, 0)),
            pl.BlockSpec((1, d), lambda i, j: (0, 0)),
        ],
        out_specs=pl.BlockSpec((tm, d), lambda i, j: (i, 0)),
        out_shape=jax.ShapeDtypeStruct((t, d), F32),
        scratch_shapes=[pltpu.VMEM((tm, d), F32), pltpu.VMEM((nj, d, tn), BF16)],
        compiler_params=pltpu.CompilerParams(
            dimension_semantics=("arbitrary", "arbitrary"),
            vmem_limit_bytes=_vmem_limit(
                _nbytes((tm, d), BF16), _nbytes((d, tn), F32), _nbytes((tm, tn), F32),
                _nbytes((tm, pd), F32), _nbytes((pd, d), F32), _nbytes((tm, d), F32),
                scratch=_nbytes((tm, d), F32) + _nbytes((nj, d, tn), BF16),
                temps=_nbytes((d, tn), BF16) + _nbytes((pd, d), BF16)
                + 2 * _nbytes((tm, tn), F32) + 2 * _nbytes((tm, d), F32)),
        ),
        name="ple_final",
    )(un, w_gate, h, p2, w_proj, g_post, g_final)


def kernel(x, p, g_mix, w_in, q_norm_a, k_norm_a, q_a_norm, w_q_b, kv_a_norm, w_kv_b,
           w_a_out, w_b_out, w_o, g_ffn, w_gate_up, w_down, g_ple, w_ple_gate,
           w_ple_proj, g_ple_post, g_final):
    batch, seq, d = x.shape
    depth = w_in.shape[0]
    t = batch * seq
    q_lora = q_a_norm.shape[-1]
    kv_lora = kv_a_norm.shape[-1]
    assert q_lora == kv_lora
    n_qa = N_HEADS_A * HEAD_DIM_A
    n_kva = N_KV_A * HEAD_DIM_A
    lat0 = n_qa + 2 * n_kva
    kr0 = lat0 + q_lora + kv_lora
    gate0 = kr0 + ROPE_DIM_B

    cos_a, sin_a = _rope_tables(seq, HEAD_DIM_A)
    cos_b, sin_b = _rope_tables(seq, ROPE_DIM_B)

    h = x.reshape(t, d)
    for i in range(depth):
        wi = w_in[i]
        wq_pad = jnp.pad(
            w_q_b[i].reshape(q_lora, N_HEADS_B, QK_DIM_B),
            ((0, 0), (0, 0), (0, QK_PAD_B - QK_DIM_B))).reshape(q_lora, N_HEADS_B * QK_PAD_B)
        lat_gains = jnp.stack([q_a_norm[i], kv_a_norm[i]])[:, None, :]
        w_gates_bf = _gate_weights(wi, col0=gate0, n_cols=2 * d, tn=512)
        w_down_bf = _cast_bf16(w_down[i], tr=512)

        qkv_a, lat, kpe, u = _inproj(
            h, g_mix[i][None], wi, q_norm_a[i][None], k_norm_a[i][None], lat_gains,
            cos_a, sin_a, cos_b, sin_b, seq=seq, tm=512, tn=2 * n_kva, kr_col0=kr0)
        q_b, k_b, v_b = _qkv_b(lat, wq_pad, w_kv_b[i], kpe, cos_b, sin_b, seq=seq, tm=512)

        oa = _attention(qkv_a, qkv_a, qkv_a, batch=batch, seq=seq, n_kv=N_KV_A,
                        groups=N_HEADS_A // N_KV_A, dk=HEAD_DIM_A, dv=HEAD_DIM_A,
                        q_col0=0, k_col0=n_qa, v_col0=n_qa + n_kva, tq=512, kc=512, sq=512,
                        aug_lane=HEAD_DIM_A, unroll=1)
        ob = _attention(q_b, k_b, v_b, batch=batch, seq=seq, n_kv=N_HEADS_B,
                        groups=1, dk=QK_PAD_B, dv=V_DIM_B,
                        q_col0=0, k_col0=0, v_col0=0, tq=1024, kc=512, sq=512,
                        aug_lane=QK_DIM_B, unroll=1)

        merged = _merge(u, oa, ob, w_a_out[i], w_b_out[i], w_gates_bf, tm=512, tn=512)
        h1, un1 = _resid_mm(merged, w_o[i], h, g_ffn[i][None], tm=512, tn=512)
        act = _ffn_up(un1, w_gate_up[i], tm=1024, tn=512)
        h2, un2 = _resid_mm(act, w_down_bf, h1, g_ple[i][None], tm=512, tn=512)
        h = _ple(un2, w_ple_gate[i], h2, p[i].reshape(t, -1), w_ple_proj[i],
                 g_ple_post[i][None], g_final[None], tm=512, tn=512,
                 final_norm=(i == depth - 1))
    return h.reshape(batch, seq, d)
```

```python
import functools
import math

import numpy as np
import jax
import jax.numpy as jnp
from jax import lax
from jax.experimental import pallas as pl
from jax.experimental.pallas import tpu as pltpu

F32 = jnp.float32
BF16 = jnp.bfloat16

GRID_W = 64
ROPE_THETA = 10000.0
RMS_EPS = 1e-6

N_HEADS_A = 8
N_KV_A = 2
HEAD_DIM_A = 128
N_HEADS_B = 8
NOPE_DIM_B = 128
ROPE_DIM_B = 64
V_DIM_B = 128
QK_DIM_B = NOPE_DIM_B + ROPE_DIM_B
QK_PAD_B = 256

LANES = 128
V7X_VMEM_BYTES = 64 * 1024 * 1024
LOG2E = math.log2(math.e)


def _vmem_limit(*block_bytes, scratch=0, temps=0):
    need = 2 * sum(block_bytes) + scratch + temps + (4 << 20)
    return int(min(need, V7X_VMEM_BYTES - (6 << 20)))


def _nbytes(shape, dtype):
    return math.prod(shape) * jnp.dtype(dtype).itemsize


def _rms(x32, gain_row):
    ms = jnp.mean(x32 * x32, axis=-1, keepdims=True)
    return x32 * lax.rsqrt(ms + RMS_EPS) * gain_row


def _swap_chunks(y, chunk):
    lane = lax.broadcasted_iota(jnp.int32, y.shape, 1)
    first = (lane % (2 * chunk)) < chunk
    return jnp.where(first, pltpu.roll(y, LANES - chunk, 1), pltpu.roll(y, chunk, 1))


def _rope(y, cos, sin_signed, chunk):
    return y * cos + _swap_chunks(y, chunk) * sin_signed


def _rope_tables(seq, rot_dim):
    rows = seq // GRID_W
    row = np.repeat(np.arange(rows, dtype=np.float64), GRID_W)
    col = np.tile(np.arange(GRID_W, dtype=np.float64), rows)
    half = rot_dim // 2
    inv = 1.0 / np.power(ROPE_THETA, np.arange(0, half, 2, dtype=np.float64) / half)
    ar = row[:, None] * inv[None, :]
    ac = col[:, None] * inv[None, :]
    cos = np.concatenate([np.cos(ar), np.cos(ar), np.cos(ac), np.cos(ac)], axis=-1)
    sin = np.concatenate([-np.sin(ar), np.sin(ar), -np.sin(ac), np.sin(ac)], axis=-1)
    pad = LANES - rot_dim
    if pad:
        cos = np.concatenate([cos, np.ones((seq, pad))], axis=-1)
        sin = np.concatenate([sin, np.zeros((seq, pad))], axis=-1)
    return jnp.asarray(cos, F32), jnp.asarray(sin, F32)


def _once(i, j, nj):
    return jnp.where(i == 0, j, nj - 1)


def _cast_kernel(w_ref, o_ref):
    o_ref[...] = w_ref[...].astype(BF16)


def _cast_bf16(w, *, tr):
    r, c = w.shape
    return pl.pallas_call(
        _cast_kernel,
        grid=(r // tr,),
        in_specs=[pl.BlockSpec((tr, c), lambda i: (i, 0))],
        out_specs=pl.BlockSpec((tr, c), lambda i: (i, 0)),
        out_shape=jax.ShapeDtypeStruct((r, c), BF16),
        compiler_params=pltpu.CompilerParams(
            dimension_semantics=("parallel",),
            vmem_limit_bytes=_vmem_limit(_nbytes((tr, c), F32), _nbytes((tr, c), BF16))),
        name="cast_bf16",
    )(w)


def _gate_weights_kernel(wt_ref, o_ref):
    o_ref[...] = wt_ref[...].T.astype(BF16)


def _gate_weights(w_in_t, *, row0, n_rows, tn):
    d = w_in_t.shape[1]
    assert row0 % 8 == 0 and n_rows % tn == 0
    return pl.pallas_call(
        _gate_weights_kernel,
        grid=(n_rows // tn,),
        in_specs=[pl.BlockSpec((pl.Element(tn), pl.Element(d)), lambda j: (pl.multiple_of(row0 + j * tn, 8), 0))],
        out_specs=pl.BlockSpec((d, tn), lambda j: (0, j)),
        out_shape=jax.ShapeDtypeStruct((d, n_rows), BF16),
        compiler_params=pltpu.CompilerParams(
            dimension_semantics=("parallel",),
            vmem_limit_bytes=_vmem_limit(
                _nbytes((tn, d), F32), _nbytes((d, tn), BF16),
                temps=2 * _nbytes((d, tn), F32))),
        name="gate_weights",
    )(w_in_t)


def _inproj_kernel(x_ref, g_ref, w_ref, wkr_ref, qn_ref, kn_ref, lg_ref,
                   cosa_ref, sina_ref, cosb_ref, sinb_ref,
                   qkv_ref, lat_ref, kpe_ref, u_ref, wbf_ref, wkrbf_ref,
                   *, q_scale, q_tiles, nj):
    i = pl.program_id(0)
    j = pl.program_id(1)
    hd = HEAD_DIM_A

    @pl.when(i == 0)
    def _():
        wbf_ref[j] = w_ref[...].T.astype(BF16)

    @pl.when(jnp.logical_and(i == 0, j == 0))
    def _():
        row = lax.broadcasted_iota(jnp.int32, wkr_ref.shape, 0)
        wkrbf_ref[...] = jnp.where(row < ROPE_DIM_B, wkr_ref[...], 0.0).T.astype(BF16)

    @pl.when(j == 0)
    def _():
        u_ref[...] = _rms(x_ref[...], g_ref[...]).astype(BF16)

    z = jnp.dot(u_ref[...], wbf_ref[j], preferred_element_type=F32)

    @pl.when(j < q_tiles)
    def _():
        cos = cosa_ref[...]
        sin = sina_ref[...]
        for h in range(z.shape[1] // hd):
            y = _rms(z[:, h * hd:(h + 1) * hd], qn_ref[...])
            qkv_ref[:, h * hd:(h + 1) * hd] = (_rope(y, cos, sin, hd // 4) * q_scale).astype(BF16)

    @pl.when(j == q_tiles)
    def _():
        cos = cosa_ref[...]
        sin = sina_ref[...]
        for h in range(N_KV_A):
            y = _rms(z[:, h * hd:(h + 1) * hd], kn_ref[...])
            qkv_ref[:, h * hd:(h + 1) * hd] = _rope(y, cos, sin, hd // 4).astype(BF16)
        qkv_ref[:, N_KV_A * hd:] = z[:, N_KV_A * hd:].astype(BF16)

    @pl.when(j > q_tiles)
    def _():
        lat_ref[...] = _rms(z, lg_ref[...]).astype(BF16)

    @pl.when(j == nj - 1)
    def _():
        kr = jnp.dot(u_ref[...], wkrbf_ref[...], preferred_element_type=F32)
        kpe_ref[...] = _rope(kr, cosb_ref[...], sinb_ref[...], ROPE_DIM_B // 4).astype(BF16)


def _inproj(x2, g_mix, w_in_t, q_norm, k_norm, lat_gains, cos_a, sin_a, cos_b, sin_b,
            *, seq, tm, tn, kr_col0):
    t, d = x2.shape
    lora = lat_gains.shape[-1]
    n_qkv = (N_HEADS_A + 2 * N_KV_A) * HEAD_DIM_A
    assert tn == 2 * N_KV_A * HEAD_DIM_A and tn == lora and kr_col0 % LANES == 0
    assert t % tm == 0 and seq % tm == 0
    q_tiles = (N_HEADS_A * HEAD_DIM_A) // tn
    nj = q_tiles + 3
    sblocks = seq // tm
    q_scale = (HEAD_DIM_A ** -0.5) * LOG2E
    return pl.pallas_call(
        functools.partial(_inproj_kernel, q_scale=q_scale, q_tiles=q_tiles, nj=nj),
        grid=(t // tm, nj),
        in_specs=[
            pl.BlockSpec((tm, d), lambda i, j: (i, 0)),
            pl.BlockSpec((1, d), lambda i, j: (0, 0)),
            pl.BlockSpec((tn, d), lambda i, j: (_once(i, j, nj), 0)),
            pl.BlockSpec((LANES, d), lambda i, j: (kr_col0 // LANES, 0)),
            pl.BlockSpec((1, HEAD_DIM_A), lambda i, j: (0, 0)),
            pl.BlockSpec((1, HEAD_DIM_A), lambda i, j: (0, 0)),
            pl.BlockSpec((None, 1, lora), lambda i, j: (jnp.maximum(j - q_tiles - 1, 0), 0, 0)),
            pl.BlockSpec((tm, LANES), lambda i, j: (i % sblocks, 0)),
            pl.BlockSpec((tm, LANES), lambda i, j: (i % sblocks, 0)),
            pl.BlockSpec((tm, LANES), lambda i, j: (i % sblocks, 0)),
            pl.BlockSpec((tm, LANES), lambda i, j: (i % sblocks, 0)),
        ],
        out_specs=[
            pl.BlockSpec((tm, tn), lambda i, j: (i, jnp.minimum(j, q_tiles))),
            pl.BlockSpec((tm, lora), lambda i, j: (i, jnp.maximum(j - q_tiles - 1, 0))),
            pl.BlockSpec((tm, LANES), lambda i, j: (i, 0)),
            pl.BlockSpec((tm, d), lambda i, j: (i, 0)),
        ],
        out_shape=[
            jax.ShapeDtypeStruct((t, n_qkv), BF16),
            jax.ShapeDtypeStruct((t, 2 * lora), BF16),
            jax.ShapeDtypeStruct((t, LANES), BF16),
            jax.ShapeDtypeStruct((t, d), BF16),
        ],
        scratch_shapes=[pltpu.VMEM((nj, d, tn), BF16), pltpu.VMEM((d, LANES), BF16)],
        compiler_params=pltpu.CompilerParams(
            dimension_semantics=("arbitrary", "arbitrary"),
            vmem_limit_bytes=_vmem_limit(
                _nbytes((tm, d), F32), _nbytes((d, tn), F32), _nbytes((d, LANES), F32),
                2 * _nbytes((tm, tn), BF16), _nbytes((tm, d), BF16), 5 * _nbytes((tm, LANES), F32),
                scratch=_nbytes((nj, d, tn), BF16) + _nbytes((d, LANES), BF16),
                temps=_nbytes((d, tn), BF16) + 4 * _nbytes((tm, tn), F32)),
        ),
        name="inproj",
    )(x2, g_mix, w_in_t, w_in_t, q_norm, k_norm, lat_gains, cos_a, sin_a, cos_b, sin_b)


def _qkv_b_kernel(lat_ref, wq_ref, wkv_ref, kpe_ref, cos_ref, sin_ref,
                  q_ref, k_ref, v_ref, wqbf_ref, wkvbf_ref, *, q_scale):
    @pl.when(pl.program_id(0) == 0)
    def _():
        wqbf_ref[...] = wq_ref[...].astype(BF16)
        wkvbf_ref[...] = wkv_ref[...].astype(BF16)

    lora = wq_ref.shape[0]
    zq = jnp.dot(lat_ref[:, :lora], wqbf_ref[...], preferred_element_type=F32)
    zkv = jnp.dot(lat_ref[:, lora:], wkvbf_ref[...], preferred_element_type=F32)
    cos = cos_ref[...]
    sin = sin_ref[...]
    kpe = kpe_ref[...]
    for h in range(N_HEADS_B):
        c0 = h * QK_PAD_B
        q_ref[:, c0:c0 + LANES] = (zq[:, c0:c0 + LANES] * q_scale).astype(BF16)
        pe = _rope(zq[:, c0 + LANES:c0 + 2 * LANES], cos, sin, ROPE_DIM_B // 4)
        q_ref[:, c0 + LANES:c0 + 2 * LANES] = (pe * q_scale).astype(BF16)
        k_ref[:, c0:c0 + LANES] = zkv[:, c0:c0 + LANES].astype(BF16)
        k_ref[:, c0 + LANES:c0 + 2 * LANES] = kpe
        v_ref[:, h * V_DIM_B:(h + 1) * V_DIM_B] = zkv[:, c0 + LANES:c0 + 2 * LANES].astype(BF16)


def _qkv_b(lat, wq_pad, w_kv_b, kpe, cos, sin, *, seq, tm):
    t = lat.shape[0]
    lora = wq_pad.shape[0]
    nq = wq_pad.shape[1]
    nkv = w_kv_b.shape[1]
    nv = N_HEADS_B * V_DIM_B
    sblocks = seq // tm
    q_scale = (QK_DIM_B ** -0.5) * LOG2E
    return pl.pallas_call(
        functools.partial(_qkv_b_kernel, q_scale=q_scale),
        grid=(t // tm,),
        in_specs=[
            pl.BlockSpec((tm, 2 * lora), lambda i: (i, 0)),
            pl.BlockSpec((lora, nq), lambda i: (0, 0)),
            pl.BlockSpec((lora, nkv), lambda i: (0, 0)),
            pl.BlockSpec((tm, LANES), lambda i: (i, 0)),
            pl.BlockSpec((tm, LANES), lambda i: (i % sblocks, 0)),
            pl.BlockSpec((tm, LANES), lambda i: (i % sblocks, 0)),
        ],
        out_specs=[
            pl.BlockSpec((tm, nq), lambda i: (i, 0)),
            pl.BlockSpec((tm, nq), lambda i: (i, 0)),
            pl.BlockSpec((tm, nv), lambda i: (i, 0)),
        ],
        out_shape=[
            jax.ShapeDtypeStruct((t, nq), BF16),
            jax.ShapeDtypeStruct((t, nq), BF16),
            jax.ShapeDtypeStruct((t, nv), BF16),
        ],
        scratch_shapes=[pltpu.VMEM((lora, nq), BF16), pltpu.VMEM((lora, nkv), BF16)],
        compiler_params=pltpu.CompilerParams(
            dimension_semantics=("arbitrary",),
            vmem_limit_bytes=_vmem_limit(
                _nbytes((tm, 2 * lora), BF16), _nbytes((lora, nq), F32), _nbytes((lora, nkv), F32),
                2 * _nbytes((tm, nq), BF16), _nbytes((tm, nv), BF16), 3 * _nbytes((tm, LANES), F32),
                scratch=_nbytes((lora, nq), BF16) + _nbytes((lora, nkv), BF16),
                temps=3 * _nbytes((tm, nq), F32)),
        ),
        name="qkv_b",
    )(lat, wq_pad, w_kv_b, kpe, cos, sin)


_L_SAFE = 2.0 ** -64


def _tree_rows8(x, op):
    while x.shape[0] > 8:
        half = x.shape[0] // 2
        x = op(x[:half], x[half:])
    return x


def _attn_kernel(q_ref, k_ref, v_ref, o_ref, kaug_ref, vt_ref, kmax_ref, *,
                 groups, dk, dv, kc, sq, aug_lane, unroll):
    seq = k_ref.shape[0]
    tq = q_ref.shape[0]
    dka = kaug_ref.shape[1]
    append_tile = aug_lane == dk

    @pl.when(pl.program_id(2) == 0)
    def _():
        kf = k_ref[...].astype(F32)
        kmax2 = jnp.max(jnp.sum(kf * kf, axis=1, keepdims=True), axis=0, keepdims=True)
        kmax_ref[...] = jnp.broadcast_to(kmax2, kmax_ref.shape)
        if append_tile:
            kaug_ref[:, :dk] = k_ref[...]
            lane = lax.broadcasted_iota(jnp.int32, (seq, dka - dk), 1)
            kaug_ref[:, dk:] = jnp.where(lane == 0, 1.0, 0.0).astype(BF16)
        else:
            lane = lax.broadcasted_iota(jnp.int32, (seq, dk), 1)
            kaug_ref[...] = jnp.where(lane == aug_lane, 1.0, kf).astype(BF16)
        vt_ref[...] = v_ref[...].astype(F32).T.astype(BF16)

    streams = [(r, g) for r in range(tq // sq) for g in range(groups)]

    def q_aug(r, g, shift_col):
        q = q_ref[r * sq:(r + 1) * sq, g * dk:(g + 1) * dk]
        neg = -shift_col
        if append_tile:
            lane = lax.broadcasted_iota(jnp.int32, (sq, dka - dk), 1)
            tile = jnp.where(lane == 0, neg, 0.0).astype(BF16)
            return jnp.concatenate([q, tile], axis=1)
        lane = lax.broadcasted_iota(jnp.int32, (sq, dk), 1)
        return jnp.where(lane == aug_lane, neg, q.astype(F32)).astype(BF16)

    def run(qas):
        def body(c, carry):
            start = pl.multiple_of(c * kc, kc)
            ks = kaug_ref[pl.ds(start, kc), :]
            vt = vt_ref[:, pl.ds(start, kc)]
            out = []
            for qa, (acc, l8) in zip(qas, carry):
                s = lax.dot_general(ks, qa, (((1,), (1,)), ((), ())),
                                    preferred_element_type=F32)
                p = jnp.exp2(s)
                out.append((acc + jnp.dot(vt, p.astype(BF16), preferred_element_type=F32),
                            l8 + _tree_rows8(p, jnp.add)))
            return tuple(out)

        init = tuple((jnp.zeros((dv, sq), F32), jnp.zeros((8, sq), F32)) for _ in qas)
        return lax.fori_loop(0, seq // kc, body, init, unroll=unroll)

    def write_out(results):
        for (r, g), (acc, l8) in zip(streams, results):
            l = jnp.sum(l8, axis=0, keepdims=True)
            o_ref[r * sq:(r + 1) * sq, g * dv:(g + 1) * dv] = (acc * (1.0 / l)).T.astype(BF16)

    kmax2 = kmax_ref[0:1, 0:1]
    bound_qas = []
    for r, g in streams:
        qf = q_ref[r * sq:(r + 1) * sq, g * dk:(g + 1) * dk].astype(F32)
        shift = jnp.sqrt(jnp.sum(qf * qf, axis=1, keepdims=True) * kmax2)
        bound_qas.append(q_aug(r, g, shift))
    fast = run(bound_qas)
    write_out(fast)

    l_min = None
    for _, l8 in fast:
        cur = jnp.min(jnp.sum(l8, axis=0, keepdims=True))
        l_min = cur if l_min is None else jnp.minimum(l_min, cur)

    @pl.when(jnp.logical_not(l_min >= _L_SAFE))
    def _():
        zero_qas = [q_aug(r, g, jnp.zeros((sq, 1), F32)) for r, g in streams]

        def max_body(c, ms):
            start = pl.multiple_of(c * kc, kc)
            ks = kaug_ref[pl.ds(start, kc), :]
            return tuple(
                jnp.maximum(m, jnp.max(
                    lax.dot_general(qa, ks, (((1,), (1,)), ((), ())),
                                    preferred_element_type=F32), axis=1, keepdims=True))
                for qa, m in zip(zero_qas, ms))

        ms = lax.fori_loop(0, seq // kc, max_body,
                           tuple(jnp.full((sq, 1), -jnp.inf, F32) for _ in streams))
        write_out(run([q_aug(r, g, m) for (r, g), m in zip(streams, ms)]))


def _attention(q_arr, k_arr, v_arr, *, batch, seq, n_kv, groups, dk, dv,
               q_col0, k_col0, v_col0, tq, kc, sq, aug_lane, unroll):
    t = batch * seq
    nq = seq // tq
    qb0 = q_col0 // (groups * dk)
    kb0 = k_col0 // dk
    vb0 = v_col0 // dv
    dka = dk + LANES if aug_lane == dk else dk
    assert aug_lane <= dk and tq % sq == 0 and seq % kc == 0
    return pl.pallas_call(
        functools.partial(_attn_kernel, groups=groups, dk=dk, dv=dv, kc=kc, sq=sq,
                          aug_lane=aug_lane, unroll=unroll),
        grid=(batch, n_kv, nq),
        in_specs=[
            pl.BlockSpec((tq, groups * dk), lambda b, h, i: (b * nq + i, qb0 + h)),
            pl.BlockSpec((seq, dk), lambda b, h, i: (b, kb0 + h)),
            pl.BlockSpec((seq, dv), lambda b, h, i: (b, vb0 + h)),
        ],
        out_specs=pl.BlockSpec((tq, groups * dv), lambda b, h, i: (b * nq + i, h)),
        out_shape=jax.ShapeDtypeStruct((t, n_kv * groups * dv), BF16),
        scratch_shapes=[pltpu.VMEM((seq, dka), BF16), pltpu.VMEM((dv, seq), BF16),
                        pltpu.VMEM((8, LANES), F32)],
        compiler_params=pltpu.CompilerParams(
            dimension_semantics=("parallel", "parallel", "arbitrary"),
            vmem_limit_bytes=_vmem_limit(
                _nbytes((tq, groups * dk), BF16), _nbytes((seq, dk), BF16),
                _nbytes((seq, dv), BF16), _nbytes((tq, groups * dv), BF16),
                scratch=_nbytes((dv, seq), BF16) + _nbytes((seq, dka), BF16),
                temps=2 * _nbytes((seq, dk), F32) + 2 * _nbytes((seq, dv), F32)
                + 6 * unroll * groups * _nbytes((kc, tq), F32)),
        ),
        name="attention",
    )(q_arr, k_arr, v_arr)


def _merge_kernel(u_ref, oa_ref, ob_ref, wa_ref, wb_ref, wga_ref, wgb_ref, o_ref,
                  wabf_ref, wbbf_ref):
    i = pl.program_id(0)
    j = pl.program_id(1)

    @pl.when(i == 0)
    def _():
        wabf_ref[j] = wa_ref[...].astype(BF16)
        wbbf_ref[j] = wb_ref[...].astype(BF16)

    u = u_ref[...]
    ya = jnp.dot(oa_ref[...], wabf_ref[j], preferred_element_type=F32)
    yb = jnp.dot(ob_ref[...], wbbf_ref[j], preferred_element_type=F32)
    ga = jnp.dot(u, wga_ref[...], preferred_element_type=F32)
    gb = jnp.dot(u, wgb_ref[...], preferred_element_type=F32)
    o_ref[...] = (jax.nn.sigmoid(ga) * ya + jax.nn.sigmoid(gb) * yb).astype(BF16)


def _merge(u, oa, ob, w_a_out, w_b_out, w_gates_bf, *, tm, tn):
    t, d = u.shape
    ka = oa.shape[1]
    kb = ob.shape[1]
    nj = d // tn
    return pl.pallas_call(
        _merge_kernel,
        grid=(t // tm, nj),
        in_specs=[
            pl.BlockSpec((tm, d), lambda i, j: (i, 0)),
            pl.BlockSpec((tm, ka), lambda i, j: (i, 0)),
            pl.BlockSpec((tm, kb), lambda i, j: (i, 0)),
            pl.BlockSpec((ka, tn), lambda i, j: (0, _once(i, j, nj))),
            pl.BlockSpec((kb, tn), lambda i, j: (0, _once(i, j, nj))),
            pl.BlockSpec((d, tn), lambda i, j: (0, j)),
            pl.BlockSpec((d, tn), lambda i, j: (0, nj + j)),
        ],
        out_specs=pl.BlockSpec((tm, tn), lambda i, j: (i, j)),
        out_shape=jax.ShapeDtypeStruct((t, d), BF16),
        scratch_shapes=[pltpu.VMEM((nj, ka, tn), BF16), pltpu.VMEM((nj, kb, tn), BF16)],
        compiler_params=pltpu.CompilerParams(
            dimension_semantics=("arbitrary", "arbitrary"),
            vmem_limit_bytes=_vmem_limit(
                _nbytes((tm, d), BF16), _nbytes((tm, ka), BF16), _nbytes((tm, kb), BF16),
                _nbytes((ka, tn), F32), _nbytes((kb, tn), F32), 2 * _nbytes((d, tn), BF16),
                _nbytes((tm, tn), BF16),
                scratch=_nbytes((nj, ka, tn), BF16) + _nbytes((nj, kb, tn), BF16),
                temps=_nbytes((ka + kb, tn), BF16) + 6 * _nbytes((tm, tn), F32)),
        ),
        name="merge",
    )(u, oa, ob, w_a_out, w_b_out, w_gates_bf, w_gates_bf)


def _resid_mm_kernel(lhs_ref, w_ref, resid_ref, g_ref, h_ref, un_ref, *scratch, tn, nj):
    i = pl.program_id(0)
    j = pl.program_id(1)
    if scratch:
        (wbf_ref,) = scratch

        @pl.when(i == 0)
        def _():
            wbf_ref[j] = w_ref[...].astype(BF16)

        w = wbf_ref[j]
    else:
        w = w_ref[...]
    val = resid_ref[...] + jnp.dot(lhs_ref[...], w, preferred_element_type=F32)
    for jj in range(nj):
        @pl.when(j == jj)
        def _(jj=jj):
            h_ref[:, jj * tn:(jj + 1) * tn] = val

    @pl.when(j == nj - 1)
    def _():
        un_ref[...] = _rms(h_ref[...], g_ref[...]).astype(BF16)


def _resid_mm(lhs, w, resid, gain, *, tm, tn):
    t, k = lhs.shape
    d = w.shape[1]
    nj = d // tn
    resident = w.dtype == F32
    if resident:
        w_spec = pl.BlockSpec((k, tn), lambda i, j: (0, _once(i, j, nj)))
        scratch_shapes = [pltpu.VMEM((nj, k, tn), BF16)]
        scratch_bytes = _nbytes((nj, k, tn), BF16)
    else:
        w_spec = pl.BlockSpec((k, tn), lambda i, j: (0, j))
        scratch_shapes = []
        scratch_bytes = 0
    return pl.pallas_call(
        functools.partial(_resid_mm_kernel, tn=tn, nj=nj),
        grid=(t // tm, nj),
        in_specs=[
            pl.BlockSpec((tm, k), lambda i, j: (i, 0)),
            w_spec,
            pl.BlockSpec((tm, tn), lambda i, j: (i, j)),
            pl.BlockSpec((1, d), lambda i, j: (0, 0)),
        ],
        out_specs=[
            pl.BlockSpec((tm, d), lambda i, j: (i, 0)),
            pl.BlockSpec((tm, d), lambda i, j: (i, 0)),
        ],
        out_shape=[
            jax.ShapeDtypeStruct((t, d), F32),
            jax.ShapeDtypeStruct((t, d), BF16),
        ],
        scratch_shapes=scratch_shapes,
        compiler_params=pltpu.CompilerParams(
            dimension_semantics=("arbitrary", "arbitrary"),
            vmem_limit_bytes=_vmem_limit(
                _nbytes((tm, k), BF16), _nbytes((k, tn), w.dtype), _nbytes((tm, tn), F32),
                _nbytes((tm, d), F32), _nbytes((tm, d), BF16),
                scratch=scratch_bytes,
                temps=_nbytes((k, tn), BF16) + 2 * _nbytes((tm, tn), F32)
                + 2 * _nbytes((tm, d), F32)),
        ),
        name="resid_mm",
    )(lhs, w, resid, gain)


def _ffn_up_kernel(un_ref, wg_ref, wu_ref, o_ref):
    un = un_ref[...]
    g = jnp.dot(un, wg_ref[...].astype(BF16), preferred_element_type=F32)
    u = jnp.dot(un, wu_ref[...].astype(BF16), preferred_element_type=F32)
    o_ref[...] = (jax.nn.silu(g) * u).astype(BF16)


def _ffn_up(un, w_gate_up, *, tm, tn):
    t, d = un.shape
    d_ff = w_gate_up.shape[1] // 2
    nj = d_ff // tn
    return pl.pallas_call(
        _ffn_up_kernel,
        grid=(t // tm, nj),
        in_specs=[
            pl.BlockSpec((tm, d), lambda i, j: (i, 0)),
            pl.BlockSpec((d, tn), lambda i, j: (0, j)),
            pl.BlockSpec((d, tn), lambda i, j: (0, nj + j)),
        ],
        out_specs=pl.BlockSpec((tm, tn), lambda i, j: (i, j)),
        out_shape=jax.ShapeDtypeStruct((t, d_ff), BF16),
        compiler_params=pltpu.CompilerParams(
            dimension_semantics=("parallel", "arbitrary"),
            vmem_limit_bytes=_vmem_limit(
                _nbytes((tm, d), BF16), 2 * _nbytes((d, tn), F32), _nbytes((tm, tn), BF16),
                temps=2 * _nbytes((d, tn), BF16) + 4 * _nbytes((tm, tn), F32)),
        ),
        name="ffn_up",
    )(un, w_gate_up, w_gate_up)


def _ple_kernel(un_ref, w_ref, h_ref, p_ref, wp_ref, gpost_ref, gfin_ref,
                o_ref, e_ref, wbf_ref, *, tn, nj, final_norm):
    i = pl.program_id(0)
    j = pl.program_id(1)

    @pl.when(i == 0)
    def _():
        wbf_ref[j] = w_ref[...].astype(BF16)

    @pl.when(j == 0)
    def _():
        e = jnp.dot(p_ref[...].astype(BF16), wp_ref[...].astype(BF16),
                    preferred_element_type=F32)
        e_ref[...] = _rms(e, gpost_ref[...])

    gate = jax.nn.sigmoid(jnp.dot(un_ref[...], wbf_ref[j], preferred_element_type=F32))
    for jj in range(nj):
        @pl.when(j == jj)
        def _(jj=jj):
            o_ref[:, jj * tn:(jj + 1) * tn] = h_ref[...] + gate * e_ref[:, jj * tn:(jj + 1) * tn]

    if final_norm:
        @pl.when(j == nj - 1)
        def _():
            o_ref[...] = _rms(o_ref[...], gfin_ref[...])


def _ple(un, w_gate, h, p2, w_proj, g_post, g_final, *, tm, tn, final_norm):
    t, d = un.shape
    pd = p2.shape[1]
    nj = d // tn
    return pl.pallas_call(
        functools.partial(_ple_kernel, tn=tn, nj=nj, final_norm=final_norm),
        grid=(t // tm, nj),
        in_specs=[
            pl.BlockSpec((tm, d), lambda i, j: (i, 0)),
            pl.BlockSpec((d, tn), lambda i, j: (0, _once(i, j, nj))),
            pl.BlockSpec((tm, tn), lambda i, j: (i, j)),
            pl.BlockSpec((tm, pd), lambda i, j: (i, 0)),
            pl.BlockSpec((pd, d), lambda i, j: (0, 0)),
            pl.BlockSpec((1, d), lambda i, j: (0, 0)),
            pl.BlockSpec((1, d), lambda i, j: (0, 0)),
        ],
        out_specs=pl.BlockSpec((tm, d), lambda i, j: (i, 0)),
        out_shape=jax.ShapeDtypeStruct((t, d), F32),
        scratch_shapes=[pltpu.VMEM((tm, d), F32), pltpu.VMEM((nj, d, tn), BF16)],
        compiler_params=pltpu.CompilerParams(
            dimension_semantics=("arbitrary", "arbitrary"),
            vmem_limit_bytes=_vmem_limit(
                _nbytes((tm, d), BF16), _nbytes((d, tn), F32), _nbytes((tm, tn), F32),
                _nbytes((tm, pd), F32), _nbytes((pd, d), F32), _nbytes((tm, d), F32),
                scratch=_nbytes((tm, d), F32) + _nbytes((nj, d, tn), BF16),
                temps=_nbytes((d, tn), BF16) + _nbytes((pd, d), BF16)
                + 2 * _nbytes((tm, tn), F32) + 2 * _nbytes((tm, d), F32)),
        ),
        name="ple_final",
    )(un, w_gate, h, p2, w_proj, g_post, g_final)


def kernel(x, p, g_mix, w_in, q_norm_a, k_norm_a, q_a_norm, w_q_b, kv_a_norm, w_kv_b,
           w_a_out, w_b_out, w_o, g_ffn, w_gate_up, w_down, g_ple, w_ple_gate,
           w_ple_proj, g_ple_post, g_final):
    batch, seq, d = x.shape
    depth = w_in.shape[0]
    t = batch * seq
    q_lora = q_a_norm.shape[-1]
    kv_lora = kv_a_norm.shape[-1]
    assert q_lora == kv_lora
    n_qa = N_HEADS_A * HEAD_DIM_A
    n_kva = N_KV_A * HEAD_DIM_A
    lat0 = n_qa + 2 * n_kva
    kr0 = lat0 + q_lora + kv_lora
    gate0 = kr0 + ROPE_DIM_B

    cos_a, sin_a = _rope_tables(seq, HEAD_DIM_A)
    cos_b, sin_b = _rope_tables(seq, ROPE_DIM_B)

    h = x.reshape(t, d)
    for i in range(depth):
        wi_t = jnp.swapaxes(w_in[i], 0, 1)
        wq_pad = jnp.pad(
            w_q_b[i].reshape(q_lora, N_HEADS_B, QK_DIM_B),
            ((0, 0), (0, 0), (0, QK_PAD_B - QK_DIM_B))).reshape(q_lora, N_HEADS_B * QK_PAD_B)
        lat_gains = jnp.stack([q_a_norm[i], kv_a_norm[i]])[:, None, :]
        w_gates_bf = _gate_weights(wi_t, row0=gate0, n_rows=2 * d, tn=512)
        w_down_bf = _cast_bf16(w_down[i], tr=512)

        qkv_a, lat, kpe, u = _inproj(
            h, g_mix[i][None], wi_t, q_norm_a[i][None], k_norm_a[i][None], lat_gains,
            cos_a, sin_a, cos_b, sin_b, seq=seq, tm=512, tn=2 * n_kva, kr_col0=kr0)
        q_b, k_b, v_b = _qkv_b(lat, wq_pad, w_kv_b[i], kpe, cos_b, sin_b, seq=seq, tm=512)

        oa = _attention(qkv_a, qkv_a, qkv_a, batch=batch, seq=seq, n_kv=N_KV_A,
                        groups=N_HEADS_A // N_KV_A, dk=HEAD_DIM_A, dv=HEAD_DIM_A,
                        q_col0=0, k_col0=n_qa, v_col0=n_qa + n_kva, tq=512, kc=512, sq=512,
                        aug_lane=HEAD_DIM_A, unroll=1)
        ob = _attention(q_b, k_b, v_b, batch=batch, seq=seq, n_kv=N_HEADS_B,
                        groups=1, dk=QK_PAD_B, dv=V_DIM_B,
                        q_col0=0, k_col0=0, v_col0=0, tq=1024, kc=512, sq=512,
                        aug_lane=QK_DIM_B, unroll=1)

        merged = _merge(u, oa, ob, w_a_out[i], w_b_out[i], w_gates_bf, tm=512, tn=512)
        h1, un1 = _resid_mm(merged, w_o[i], h, g_ffn[i][None], tm=512, tn=512)
        act = _ffn_up(un1, w_gate_up[i], tm=1024, tn=512)
        h2, un2 = _resid_mm(act, w_down_bf, h1, g_ple[i][None], tm=512, tn=512)
        h = _ple(un2, w_ple_gate[i], h2, p[i].reshape(t, -1), w_ple_proj[i],
                 g_ple_post[i][None], g_final[None], tm=512, tn=512,
                 final_norm=(i == depth - 1))
    return h.reshape(batch, seq, d)
```

```python
import functools
import math

import numpy as np
import jax
import jax.numpy as jnp
from jax import lax
from jax.experimental import pallas as pl
from jax.experimental.pallas import tpu as pltpu

F32 = jnp.float32
BF16 = jnp.bfloat16

GRID_W = 64
ROPE_THETA = 10000.0
RMS_EPS = 1e-6

N_HEADS_A = 8
N_KV_A = 2
HEAD_DIM_A = 128
N_HEADS_B = 8
NOPE_DIM_B = 128
ROPE_DIM_B = 64
V_DIM_B = 128
QK_DIM_B = NOPE_DIM_B + ROPE_DIM_B
QK_PAD_B = 256

LANES = 128
V7X_VMEM_BYTES = 64 * 1024 * 1024
LOG2E = math.log2(math.e)


def _vmem_limit(*block_bytes, scratch=0, temps=0):
    need = 2 * sum(block_bytes) + scratch + temps + (4 << 20)
    return int(min(need, V7X_VMEM_BYTES - (6 << 20)))


def _nbytes(shape, dtype):
    return math.prod(shape) * jnp.dtype(dtype).itemsize


def _rms(x32, gain_row):
    ms = jnp.mean(x32 * x32, axis=-1, keepdims=True)
    return x32 * lax.rsqrt(ms + RMS_EPS) * gain_row


def _swap_chunks(y, chunk):
    lane = lax.broadcasted_iota(jnp.int32, y.shape, 1)
    first = (lane % (2 * chunk)) < chunk
    return jnp.where(first, pltpu.roll(y, LANES - chunk, 1), pltpu.roll(y, chunk, 1))


def _rope(y, cos, sin_signed, chunk):
    return y * cos + _swap_chunks(y, chunk) * sin_signed


def _rope_tables(seq, rot_dim):
    rows = seq // GRID_W
    row = np.repeat(np.arange(rows, dtype=np.float64), GRID_W)
    col = np.tile(np.arange(GRID_W, dtype=np.float64), rows)
    half = rot_dim // 2
    inv = 1.0 / np.power(ROPE_THETA, np.arange(0, half, 2, dtype=np.float64) / half)
    ar = row[:, None] * inv[None, :]
    ac = col[:, None] * inv[None, :]
    cos = np.concatenate([np.cos(ar), np.cos(ar), np.cos(ac), np.cos(ac)], axis=-1)
    sin = np.concatenate([-np.sin(ar), np.sin(ar), -np.sin(ac), np.sin(ac)], axis=-1)
    pad = LANES - rot_dim
    if pad:
        cos = np.concatenate([cos, np.ones((seq, pad))], axis=-1)
        sin = np.concatenate([sin, np.zeros((seq, pad))], axis=-1)
    return jnp.asarray(cos, F32), jnp.asarray(sin, F32)


def _once(i, j, nj):
    return jnp.where(i == 0, j, nj - 1)


def _cast_kernel(w_ref, o_ref):
    o_ref[...] = w_ref[...].astype(BF16)


def _cast_bf16(w, *, tr):
    r, c = w.shape
    return pl.pallas_call(
        _cast_kernel,
        grid=(r // tr,),
        in_specs=[pl.BlockSpec((tr, c), lambda i: (i, 0))],
        out_specs=pl.BlockSpec((tr, c), lambda i: (i, 0)),
        out_shape=jax.ShapeDtypeStruct((r, c), BF16),
        compiler_params=pltpu.CompilerParams(
            dimension_semantics=("parallel",),
            vmem_limit_bytes=_vmem_limit(_nbytes((tr, c), F32), _nbytes((tr, c), BF16))),
        name="cast_bf16",
    )(w)


def _gate_weights_kernel(wt_ref, o_ref):
    o_ref[...] = wt_ref[...].T.astype(BF16)


def _gate_weights(w_in_t, *, row0, n_rows, tn):
    d = w_in_t.shape[1]
    assert row0 % 8 == 0 and n_rows % tn == 0
    return pl.pallas_call(
        _gate_weights_kernel,
        grid=(n_rows // tn,),
        in_specs=[pl.BlockSpec((pl.Element(tn), pl.Element(d)), lambda j: (pl.multiple_of(row0 + j * tn, 8), 0))],
        out_specs=pl.BlockSpec((d, tn), lambda j: (0, j)),
        out_shape=jax.ShapeDtypeStruct((d, n_rows), BF16),
        compiler_params=pltpu.CompilerParams(
            dimension_semantics=("parallel",),
            vmem_limit_bytes=_vmem_limit(
                _nbytes((tn, d), F32), _nbytes((d, tn), BF16),
                temps=2 * _nbytes((d, tn), F32))),
        name="gate_weights",
    )(w_in_t)


def _inproj_kernel(x_ref, g_ref, w_ref, wkr_ref, qn_ref, kn_ref, lg_ref,
                   cosa_ref, sina_ref, cosb_ref, sinb_ref,
                   qkv_ref, lat_ref, kpe_ref, u_ref, wbf_ref, wkrbf_ref,
                   *, q_scale, q_tiles, nj):
    i = pl.program_id(0)
    j = pl.program_id(1)
    hd = HEAD_DIM_A

    @pl.when(i == 0)
    def _():
        wbf_ref[j] = w_ref[...].T.astype(BF16)

    @pl.when(jnp.logical_and(i == 0, j == 0))
    def _():
        row = lax.broadcasted_iota(jnp.int32, wkr_ref.shape, 0)
        wkrbf_ref[...] = jnp.where(row < ROPE_DIM_B, wkr_ref[...], 0.0).T.astype(BF16)

    @pl.when(j == 0)
    def _():
        u_ref[...] = _rms(x_ref[...], g_ref[...]).astype(BF16)

    z = jnp.dot(u_ref[...], wbf_ref[j], preferred_element_type=F32)

    @pl.when(j < q_tiles)
    def _():
        cos = cosa_ref[...]
        sin = sina_ref[...]
        for h in range(z.shape[1] // hd):
            y = _rms(z[:, h * hd:(h + 1) * hd], qn_ref[...])
            qkv_ref[:, h * hd:(h + 1) * hd] = (_rope(y, cos, sin, hd // 4) * q_scale).astype(BF16)

    @pl.when(j == q_tiles)
    def _():
        cos = cosa_ref[...]
        sin = sina_ref[...]
        for h in range(N_KV_A):
            y = _rms(z[:, h * hd:(h + 1) * hd], kn_ref[...])
            qkv_ref[:, h * hd:(h + 1) * hd] = _rope(y, cos, sin, hd // 4).astype(BF16)
        qkv_ref[:, N_KV_A * hd:] = z[:, N_KV_A * hd:].astype(BF16)

    @pl.when(j > q_tiles)
    def _():
        lat_ref[...] = _rms(z, lg_ref[...]).astype(BF16)

    @pl.when(j == nj - 1)
    def _():
        kr = jnp.dot(u_ref[...], wkrbf_ref[...], preferred_element_type=F32)
        kpe_ref[...] = _rope(kr, cosb_ref[...], sinb_ref[...], ROPE_DIM_B // 4).astype(BF16)


def _inproj(x2, g_mix, w_in_t, q_norm, k_norm, lat_gains, cos_a, sin_a, cos_b, sin_b,
            *, seq, tm, tn, kr_col0):
    t, d = x2.shape
    lora = lat_gains.shape[-1]
    n_qkv = (N_HEADS_A + 2 * N_KV_A) * HEAD_DIM_A
    assert tn == 2 * N_KV_A * HEAD_DIM_A and tn == lora and kr_col0 % LANES == 0
    assert t % tm == 0 and seq % tm == 0
    q_tiles = (N_HEADS_A * HEAD_DIM_A) // tn
    nj = q_tiles + 3
    sblocks = seq // tm
    q_scale = (HEAD_DIM_A ** -0.5) * LOG2E
    return pl.pallas_call(
        functools.partial(_inproj_kernel, q_scale=q_scale, q_tiles=q_tiles, nj=nj),
        grid=(t // tm, nj),
        in_specs=[
            pl.BlockSpec((tm, d), lambda i, j: (i, 0)),
            pl.BlockSpec((1, d), lambda i, j: (0, 0)),
            pl.BlockSpec((tn, d), lambda i, j: (_once(i, j, nj), 0)),
            pl.BlockSpec((LANES, d), lambda i, j: (kr_col0 // LANES, 0)),
            pl.BlockSpec((1, HEAD_DIM_A), lambda i, j: (0, 0)),
            pl.BlockSpec((1, HEAD_DIM_A), lambda i, j: (0, 0)),
            pl.BlockSpec((None, 1, lora), lambda i, j: (jnp.maximum(j - q_tiles - 1, 0), 0, 0)),
            pl.BlockSpec((tm, LANES), lambda i, j: (i % sblocks, 0)),
            pl.BlockSpec((tm, LANES), lambda i, j: (i % sblocks, 0)),
            pl.BlockSpec((tm, LANES), lambda i, j: (i % sblocks, 0)),
            pl.BlockSpec((tm, LANES), lambda i, j: (i % sblocks, 0)),
        ],
        out_specs=[
            pl.BlockSpec((tm, tn), lambda i, j: (i, jnp.minimum(j, q_tiles))),
            pl.BlockSpec((tm, lora), lambda i, j: (i, jnp.maximum(j - q_tiles - 1, 0))),
            pl.BlockSpec((tm, LANES), lambda i, j: (i, 0)),
            pl.BlockSpec((tm, d), lambda i, j: (i, 0)),
        ],
        out_shape=[
            jax.ShapeDtypeStruct((t, n_qkv), BF16),
            jax.ShapeDtypeStruct((t, 2 * lora), BF16),
            jax.ShapeDtypeStruct((t, LANES), BF16),
            jax.ShapeDtypeStruct((t, d), BF16),
        ],
        scratch_shapes=[pltpu.VMEM((nj, d, tn), BF16), pltpu.VMEM((d, LANES), BF16)],
        compiler_params=pltpu.CompilerParams(
            dimension_semantics=("arbitrary", "arbitrary"),
            vmem_limit_bytes=_vmem_limit(
                _nbytes((tm, d), F32), _nbytes((d, tn), F32), _nbytes((d, LANES), F32),
                2 * _nbytes((tm, tn), BF16), _nbytes((tm, d), BF16), 5 * _nbytes((tm, LANES), F32),
                scratch=_nbytes((nj, d, tn), BF16) + _nbytes((d, LANES), BF16),
                temps=_nbytes((d, tn), BF16) + 4 * _nbytes((tm, tn), F32)),
        ),
        name="inproj",
    )(x2, g_mix, w_in_t, w_in_t, q_norm, k_norm, lat_gains, cos_a, sin_a, cos_b, sin_b)


def _qkv_b_kernel(lat_ref, wq_ref, wkv_ref, kpe_ref, cos_ref, sin_ref,
                  q_ref, k_ref, v_ref, wqbf_ref, wkvbf_ref, *, q_scale):
    @pl.when(pl.program_id(0) == 0)
    def _():
        wqbf_ref[...] = wq_ref[...].astype(BF16)
        wkvbf_ref[...] = wkv_ref[...].astype(BF16)

    lora = wq_ref.shape[0]
    zq = jnp.dot(lat_ref[:, :lora], wqbf_ref[...], preferred_element_type=F32)
    zkv = jnp.dot(lat_ref[:, lora:], wkvbf_ref[...], preferred_element_type=F32)
    cos = cos_ref[...]
    sin = sin_ref[...]
    kpe = kpe_ref[...]
    for h in range(N_HEADS_B):
        c0 = h * QK_PAD_B
        q_ref[:, c0:c0 + LANES] = (zq[:, c0:c0 + LANES] * q_scale).astype(BF16)
        pe = _rope(zq[:, c0 + LANES:c0 + 2 * LANES], cos, sin, ROPE_DIM_B // 4)
        q_ref[:, c0 + LANES:c0 + 2 * LANES] = (pe * q_scale).astype(BF16)
        k_ref[:, c0:c0 + LANES] = zkv[:, c0:c0 + LANES].astype(BF16)
        k_ref[:, c0 + LANES:c0 + 2 * LANES] = kpe
        v_ref[:, h * V_DIM_B:(h + 1) * V_DIM_B] = zkv[:, c0 + LANES:c0 + 2 * LANES].astype(BF16)


def _qkv_b(lat, wq_pad, w_kv_b, kpe, cos, sin, *, seq, tm):
    t = lat.shape[0]
    lora = wq_pad.shape[0]
    nq = wq_pad.shape[1]
    nkv = w_kv_b.shape[1]
    nv = N_HEADS_B * V_DIM_B
    sblocks = seq // tm
    q_scale = (QK_DIM_B ** -0.5) * LOG2E
    return pl.pallas_call(
        functools.partial(_qkv_b_kernel, q_scale=q_scale),
        grid=(t // tm,),
        in_specs=[
            pl.BlockSpec((tm, 2 * lora), lambda i: (i, 0)),
            pl.BlockSpec((lora, nq), lambda i: (0, 0)),
            pl.BlockSpec((lora, nkv), lambda i: (0, 0)),
            pl.BlockSpec((tm, LANES), lambda i: (i, 0)),
            pl.BlockSpec((tm, LANES), lambda i: (i % sblocks, 0)),
            pl.BlockSpec((tm, LANES), lambda i: (i % sblocks, 0)),
        ],
        out_specs=[
            pl.BlockSpec((tm, nq), lambda i: (i, 0)),
            pl.BlockSpec((tm, nq), lambda i: (i, 0)),
            pl.BlockSpec((tm, nv), lambda i: (i, 0)),
        ],
        out_shape=[
            jax.ShapeDtypeStruct((t, nq), BF16),
            jax.ShapeDtypeStruct((t, nq), BF16),
            jax.ShapeDtypeStruct((t, nv), BF16),
        ],
        scratch_shapes=[pltpu.VMEM((lora, nq), BF16), pltpu.VMEM((lora, nkv), BF16)],
        compiler_params=pltpu.CompilerParams(
            dimension_semantics=("arbitrary",),
            vmem_limit_bytes=_vmem_limit(
                _nbytes((tm, 2 * lora), BF16), _nbytes((lora, nq), F32), _nbytes((lora, nkv), F32),
                2 * _nbytes((tm, nq), BF16), _nbytes((tm, nv), BF16), 3 * _nbytes((tm, LANES), F32),
                scratch=_nbytes((lora, nq), BF16) + _nbytes((lora, nkv), BF16),
                temps=3 * _nbytes((tm, nq), F32)),
        ),
        name="qkv_b",
    )(lat, wq_pad, w_kv_b, kpe, cos, sin)


_L_SAFE = 2.0 ** -64


def _tree_rows8(x, op):
    while x.shape[0] > 8:
        half = x.shape[0] // 2
        x = op(x[:half], x[half:])
    return x


def _attn_kernel(q_ref, k_ref, v_ref, o_ref, kaug_ref, vt_ref, kmax_ref,
                 qa_ref, s_ref, acc_ref, l_ref, *, groups, dk, dv, kc, sq, aug_lane):
    seq = k_ref.shape[0]
    tq = q_ref.shape[0]
    dka = kaug_ref.shape[1]
    append_tile = aug_lane == dk

    @pl.when(pl.program_id(2) == 0)
    def _():
        kf = k_ref[...].astype(F32)
        kmax2 = jnp.max(jnp.sum(kf * kf, axis=1, keepdims=True), axis=0, keepdims=True)
        kmax_ref[...] = jnp.broadcast_to(kmax2, kmax_ref.shape)
        if append_tile:
            kaug_ref[:, :dk] = k_ref[...]
            lane = lax.broadcasted_iota(jnp.int32, (seq, dka - dk), 1)
            kaug_ref[:, dk:] = jnp.where(lane == 0, 1.0, 0.0).astype(BF16)
        else:
            lane = lax.broadcasted_iota(jnp.int32, (seq, dk), 1)
            kaug_ref[...] = jnp.where(lane == aug_lane, 1.0, kf).astype(BF16)
        vt_ref[...] = v_ref[...].astype(F32).T.astype(BF16)

    streams = [(r, g) for r in range(tq // sq) for g in range(groups)]

    def q_aug(r, g, shift_col):
        q = q_ref[r * sq:(r + 1) * sq, g * dk:(g + 1) * dk]
        neg = -shift_col
        if append_tile:
            lane = lax.broadcasted_iota(jnp.int32, (sq, dka - dk), 1)
            tile = jnp.where(lane == 0, neg, 0.0).astype(BF16)
            return jnp.concatenate([q, tile], axis=1)
        lane = lax.broadcasted_iota(jnp.int32, (sq, dk), 1)
        return jnp.where(lane == aug_lane, neg, q.astype(F32)).astype(BF16)

    def run(qas):
        n_chunks = seq // kc
        n_str = len(qas)
        for i, qa in enumerate(qas):
            qa_ref[i] = qa
        acc_ref[...] = jnp.zeros(acc_ref.shape, F32)
        l_ref[...] = jnp.zeros(l_ref.shape, F32)

        def scores(c, slot):
            start = c * kc if isinstance(c, int) else pl.multiple_of(c * kc, kc)
            ks = kaug_ref[pl.ds(start, kc), :]
            for i in range(n_str):
                s_ref[slot, i] = lax.dot_general(ks, qa_ref[i], (((1,), (1,)), ((), ())),
                                                 preferred_element_type=F32)

        def accumulate(c, slot):
            start = c * kc if isinstance(c, int) else pl.multiple_of(c * kc, kc)
            vt = vt_ref[:, pl.ds(start, kc)]
            for i in range(n_str):
                p = jnp.exp2(s_ref[slot, i])
                acc_ref[i] += jnp.dot(vt, p.astype(BF16), preferred_element_type=F32)
                l_ref[i] += _tree_rows8(p, jnp.add)

        def body(c2, carry):
            c = 2 * c2
            scores(c + 1, 1)
            accumulate(c, 0)
            scores(c + 2, 0)
            accumulate(c + 1, 1)
            return carry

        scores(0, 0)
        lax.fori_loop(0, n_chunks // 2 - 1, body, 0)
        scores(n_chunks - 1, 1)
        accumulate(n_chunks - 2, 0)
        accumulate(n_chunks - 1, 1)
        return tuple((acc_ref[i], l_ref[i]) for i in range(n_str))

    def write_out(results):
        for (r, g), (acc, l8) in zip(streams, results):
            l = jnp.sum(l8, axis=0, keepdims=True)
            o_ref[r * sq:(r + 1) * sq, g * dv:(g + 1) * dv] = (acc * (1.0 / l)).T.astype(BF16)

    kmax2 = kmax_ref[0:1, 0:1]
    bound_qas = []
    for r, g in streams:
        qf = q_ref[r * sq:(r + 1) * sq, g * dk:(g + 1) * dk].astype(F32)
        shift = jnp.sqrt(jnp.sum(qf * qf, axis=1, keepdims=True) * kmax2)
        bound_qas.append(q_aug(r, g, shift))
    fast = run(bound_qas)
    write_out(fast)

    l_min = None
    for _, l8 in fast:
        cur = jnp.min(jnp.sum(l8, axis=0, keepdims=True))
        l_min = cur if l_min is None else jnp.minimum(l_min, cur)

    @pl.when(jnp.logical_not(l_min >= _L_SAFE))
    def _():
        zero_qas = [q_aug(r, g, jnp.zeros((sq, 1), F32)) for r, g in streams]

        def max_body(c, ms):
            start = c * kc if isinstance(c, int) else pl.multiple_of(c * kc, kc)
            ks = kaug_ref[pl.ds(start, kc), :]
            return tuple(
                jnp.maximum(m, jnp.max(
                    lax.dot_general(qa, ks, (((1,), (1,)), ((), ())),
                                    preferred_element_type=F32), axis=1, keepdims=True))
                for qa, m in zip(zero_qas, ms))

        ms = lax.fori_loop(0, seq // kc, max_body,
                           tuple(jnp.full((sq, 1), -jnp.inf, F32) for _ in streams))
        write_out(run([q_aug(r, g, m) for (r, g), m in zip(streams, ms)]))


def _attention(q_arr, k_arr, v_arr, *, batch, seq, n_kv, groups, dk, dv,
               q_col0, k_col0, v_col0, tq, kc, sq, aug_lane):
    t = batch * seq
    nq = seq // tq
    qb0 = q_col0 // (groups * dk)
    kb0 = k_col0 // dk
    vb0 = v_col0 // dv
    dka = dk + LANES if aug_lane == dk else dk
    n_str = (tq // sq) * groups
    assert aug_lane <= dk and tq % sq == 0 and seq % (2 * kc) == 0 and seq // kc >= 4
    scratch_bytes = (_nbytes((seq, dka), BF16) + _nbytes((dv, seq), BF16)
                     + _nbytes((n_str, sq, dka), BF16) + _nbytes((2, n_str, kc, sq), F32)
                     + _nbytes((n_str, dv + 8, sq), F32))
    return pl.pallas_call(
        functools.partial(_attn_kernel, groups=groups, dk=dk, dv=dv, kc=kc, sq=sq,
                          aug_lane=aug_lane),
        grid=(batch, n_kv, nq),
        in_specs=[
            pl.BlockSpec((tq, groups * dk), lambda b, h, i: (b * nq + i, qb0 + h)),
            pl.BlockSpec((seq, dk), lambda b, h, i: (b, kb0 + h)),
            pl.BlockSpec((seq, dv), lambda b, h, i: (b, vb0 + h)),
        ],
        out_specs=pl.BlockSpec((tq, groups * dv), lambda b, h, i: (b * nq + i, h)),
        out_shape=jax.ShapeDtypeStruct((t, n_kv * groups * dv), BF16),
        scratch_shapes=[pltpu.VMEM((seq, dka), BF16), pltpu.VMEM((dv, seq), BF16),
                        pltpu.VMEM((8, LANES), F32), pltpu.VMEM((n_str, sq, dka), BF16),
                        pltpu.VMEM((2, n_str, kc, sq), F32), pltpu.VMEM((n_str, dv, sq), F32),
                        pltpu.VMEM((n_str, 8, sq), F32)],
        compiler_params=pltpu.CompilerParams(
            dimension_semantics=("parallel", "parallel", "arbitrary"),
            vmem_limit_bytes=_vmem_limit(
                _nbytes((tq, groups * dk), BF16), _nbytes((seq, dk), BF16),
                _nbytes((seq, dv), BF16), _nbytes((tq, groups * dv), BF16),
                scratch=scratch_bytes,
                temps=2 * _nbytes((seq, dk), F32) + 2 * _nbytes((seq, dv), F32)
                + 4 * _nbytes((kc, sq), F32)),
        ),
        name="attention",
    )(q_arr, k_arr, v_arr)


def _merge_kernel(u_ref, oa_ref, ob_ref, wa_ref, wb_ref, wga_ref, wgb_ref, o_ref,
                  wabf_ref, wbbf_ref):
    i = pl.program_id(0)
    j = pl.program_id(1)

    @pl.when(i == 0)
    def _():
        wabf_ref[j] = wa_ref[...].astype(BF16)
        wbbf_ref[j] = wb_ref[...].astype(BF16)

    u = u_ref[...]
    ya = jnp.dot(oa_ref[...], wabf_ref[j], preferred_element_type=F32)
    yb = jnp.dot(ob_ref[...], wbbf_ref[j], preferred_element_type=F32)
    ga = jnp.dot(u, wga_ref[...], preferred_element_type=F32)
    gb = jnp.dot(u, wgb_ref[...], preferred_element_type=F32)
    o_ref[...] = (jax.nn.sigmoid(ga) * ya + jax.nn.sigmoid(gb) * yb).astype(BF16)


def _merge(u, oa, ob, w_a_out, w_b_out, w_gates_bf, *, tm, tn):
    t, d = u.shape
    ka = oa.shape[1]
    kb = ob.shape[1]
    nj = d // tn
    return pl.pallas_call(
        _merge_kernel,
        grid=(t // tm, nj),
        in_specs=[
            pl.BlockSpec((tm, d), lambda i, j: (i, 0)),
            pl.BlockSpec((tm, ka), lambda i, j: (i, 0)),
            pl.BlockSpec((tm, kb), lambda i, j: (i, 0)),
            pl.BlockSpec((ka, tn), lambda i, j: (0, _once(i, j, nj))),
            pl.BlockSpec((kb, tn), lambda i, j: (0, _once(i, j, nj))),
            pl.BlockSpec((d, tn), lambda i, j: (0, j)),
            pl.BlockSpec((d, tn), lambda i, j: (0, nj + j)),
        ],
        out_specs=pl.BlockSpec((tm, tn), lambda i, j: (i, j)),
        out_shape=jax.ShapeDtypeStruct((t, d), BF16),
        scratch_shapes=[pltpu.VMEM((nj, ka, tn), BF16), pltpu.VMEM((nj, kb, tn), BF16)],
        compiler_params=pltpu.CompilerParams(
            dimension_semantics=("arbitrary", "arbitrary"),
            vmem_limit_bytes=_vmem_limit(
                _nbytes((tm, d), BF16), _nbytes((tm, ka), BF16), _nbytes((tm, kb), BF16),
                _nbytes((ka, tn), F32), _nbytes((kb, tn), F32), 2 * _nbytes((d, tn), BF16),
                _nbytes((tm, tn), BF16),
                scratch=_nbytes((nj, ka, tn), BF16) + _nbytes((nj, kb, tn), BF16),
                temps=_nbytes((ka + kb, tn), BF16) + 6 * _nbytes((tm, tn), F32)),
        ),
        name="merge",
    )(u, oa, ob, w_a_out, w_b_out, w_gates_bf, w_gates_bf)


def _resid_mm_kernel(lhs_ref, w_ref, resid_ref, g_ref, h_ref, un_ref, *scratch, tn, nj):
    i = pl.program_id(0)
    j = pl.program_id(1)
    if scratch:
        (wbf_ref,) = scratch

        @pl.when(i == 0)
        def _():
            wbf_ref[j] = w_ref[...].astype(BF16)

        w = wbf_ref[j]
    else:
        w = w_ref[...]
    val = resid_ref[...] + jnp.dot(lhs_ref[...], w, preferred_element_type=F32)
    for jj in range(nj):
        @pl.when(j == jj)
        def _(jj=jj):
            h_ref[:, jj * tn:(jj + 1) * tn] = val

    @pl.when(j == nj - 1)
    def _():
        un_ref[...] = _rms(h_ref[...], g_ref[...]).astype(BF16)


def _resid_mm(lhs, w, resid, gain, *, tm, tn):
    t, k = lhs.shape
    d = w.shape[1]
    nj = d // tn
    resident = w.dtype == F32
    if resident:
        w_spec = pl.BlockSpec((k, tn), lambda i, j: (0, _once(i, j, nj)))
        scratch_shapes = [pltpu.VMEM((nj, k, tn), BF16)]
        scratch_bytes = _nbytes((nj, k, tn), BF16)
    else:
        w_spec = pl.BlockSpec((k, tn), lambda i, j: (0, j))
        scratch_shapes = []
        scratch_bytes = 0
    return pl.pallas_call(
        functools.partial(_resid_mm_kernel, tn=tn, nj=nj),
        grid=(t // tm, nj),
        in_specs=[
            pl.BlockSpec((tm, k), lambda i, j: (i, 0)),
            w_spec,
            pl.BlockSpec((tm, tn), lambda i, j: (i, j)),
            pl.BlockSpec((1, d), lambda i, j: (0, 0)),
        ],
        out_specs=[
            pl.BlockSpec((tm, d), lambda i, j: (i, 0)),
            pl.BlockSpec((tm, d), lambda i, j: (i, 0)),
        ],
        out_shape=[
            jax.ShapeDtypeStruct((t, d), F32),
            jax.ShapeDtypeStruct((t, d), BF16),
        ],
        scratch_shapes=scratch_shapes,
        compiler_params=pltpu.CompilerParams(
            dimension_semantics=("arbitrary", "arbitrary"),
            vmem_limit_bytes=_vmem_limit(
                _nbytes((tm, k), BF16), _nbytes((k, tn), w.dtype), _nbytes((tm, tn), F32),
                _nbytes((tm, d), F32), _nbytes((tm, d), BF16),
                scratch=scratch_bytes,
                temps=_nbytes((k, tn), BF16) + 2 * _nbytes((tm, tn), F32)
                + 2 * _nbytes((tm, d), F32)),
        ),
        name="resid_mm",
    )(lhs, w, resid, gain)


def _ffn_up_kernel(un_ref, wg_ref, wu_ref, o_ref):
    un = un_ref[...]
    g = jnp.dot(un, wg_ref[...].astype(BF16), preferred_element_type=F32)
    u = jnp.dot(un, wu_ref[...].astype(BF16), preferred_element_type=F32)
    o_ref[...] = (jax.nn.silu(g) * u).astype(BF16)


def _ffn_up(un, w_gate_up, *, tm, tn):
    t, d = un.shape
    d_ff = w_gate_up.shape[1] // 2
    nj = d_ff // tn
    return pl.pallas_call(
        _ffn_up_kernel,
        grid=(t // tm, nj),
        in_specs=[
            pl.BlockSpec((tm, d), lambda i, j: (i, 0)),
            pl.BlockSpec((d, tn), lambda i, j: (0, j)),
            pl.BlockSpec((d, tn), lambda i, j: (0, nj + j)),
        ],
        out_specs=pl.BlockSpec((tm, tn), lambda i, j: (i, j)),
        out_shape=jax.ShapeDtypeStruct((t, d_ff), BF16),
        compiler_params=pltpu.CompilerParams(
            dimension_semantics=("parallel", "arbitrary"),
            vmem_limit_bytes=_vmem_limit(
                _nbytes((tm, d), BF16), 2 * _nbytes((d, tn), F32), _nbytes((tm, tn), BF16),
                temps=2 * _nbytes((d, tn), BF16) + 4 * _nbytes((tm, tn), F32)),
        ),
        name="ffn_up",
    )(un, w_gate_up, w_gate_up)


def _ple_kernel(un_ref, w_ref, h_ref, p_ref, wp_ref, gpost_ref, gfin_ref,
                o_ref, e_ref, wbf_ref, *, tn, nj, final_norm):
    i = pl.program_id(0)
    j = pl.program_id(1)

    @pl.when(i == 0)
    def _():
        wbf_ref[j] = w_ref[...].astype(BF16)

    @pl.when(j == 0)
    def _():
        e = jnp.dot(p_ref[...].astype(BF16), wp_ref[...].astype(BF16),
                    preferred_element_type=F32)
        e_ref[...] = _rms(e, gpost_ref[...])

    gate = jax.nn.sigmoid(jnp.dot(un_ref[...], wbf_ref[j], preferred_element_type=F32))
    for jj in range(nj):
        @pl.when(j == jj)
        def _(jj=jj):
            o_ref[:, jj * tn:(jj + 1) * tn] = h_ref[...] + gate * e_ref[:, jj * tn:(jj + 1) * tn]

    if final_norm:
        @pl.when(j == nj - 1)
        def _():
            o_ref[...] = _rms(o_ref[...], gfin_ref[...])


def _ple(un, w_gate, h, p2, w_proj, g_post, g_final, *, tm, tn, final_norm):
    t, d = un.shape
    pd = p2.shape[1]
    nj = d // tn
    return pl.pallas_call(
        functools.partial(_ple_kernel, tn=tn, nj=nj, final_norm=final_norm),
        grid=(t // tm, nj),
        in_specs=[
            pl.BlockSpec((tm, d), lambda i, j: (i, 0)),
            pl.BlockSpec((d, tn), lambda i, j: (0, _once(i, j, nj))),
            pl.BlockSpec((tm, tn), lambda i, j: (i, j)),
            pl.BlockSpec((tm, pd), lambda i, j: (i, 0)),
            pl.BlockSpec((pd, d), lambda i, j: (0, 0)),
            pl.BlockSpec((1, d), lambda i, j: (0, 0)),
            pl.BlockSpec((1, d), lambda i, j: (0, 0)),
        ],
        out_specs=pl.BlockSpec((tm, d), lambda i, j: (i, 0)),
        out_shape=jax.ShapeDtypeStruct((t, d), F32),
        scratch_shapes=[pltpu.VMEM((tm, d), F32), pltpu.VMEM((nj, d, tn), BF16)],
        compiler_params=pltpu.CompilerParams(
            dimension_semantics=("arbitrary", "arbitrary"),
            vmem_limit_bytes=_vmem_limit(
                _nbytes((tm, d), BF16), _nbytes((d, tn), F32), _nbytes((tm, tn), F32),
                _nbytes((tm, pd), F32), _nbytes((pd, d), F32), _nbytes((tm, d), F32),
                scratch=_nbytes((tm, d), F32) + _nbytes((nj, d, tn), BF16),
                temps=_nbytes((d, tn), BF16) + _nbytes((pd, d), BF16)
                + 2 * _nbytes((tm, tn), F32) + 2 * _nbytes((tm, d), F32)),
        ),
        name="ple_final",
    )(un, w_gate, h, p2, w_proj, g_post, g_final)


def kernel(x, p, g_mix, w_in, q_norm_a, k_norm_a, q_a_norm, w_q_b, kv_a_norm, w_kv_b,
           w_a_out, w_b_out, w_o, g_ffn, w_gate_up, w_down, g_ple, w_ple_gate,
           w_ple_proj, g_ple_post, g_final):
    batch, seq, d = x.shape
    depth = w_in.shape[0]
    t = batch * seq
    q_lora = q_a_norm.shape[-1]
    kv_lora = kv_a_norm.shape[-1]
    assert q_lora == kv_lora
    n_qa = N_HEADS_A * HEAD_DIM_A
    n_kva = N_KV_A * HEAD_DIM_A
    lat0 = n_qa + 2 * n_kva
    kr0 = lat0 + q_lora + kv_lora
    gate0 = kr0 + ROPE_DIM_B

    cos_a, sin_a = _rope_tables(seq, HEAD_DIM_A)
    cos_b, sin_b = _rope_tables(seq, ROPE_DIM_B)

    h = x.reshape(t, d)
    for i in range(depth):
        wi_t = jnp.swapaxes(w_in[i], 0, 1)
        wq_pad = jnp.pad(
            w_q_b[i].reshape(q_lora, N_HEADS_B, QK_DIM_B),
            ((0, 0), (0, 0), (0, QK_PAD_B - QK_DIM_B))).reshape(q_lora, N_HEADS_B * QK_PAD_B)
        lat_gains = jnp.stack([q_a_norm[i], kv_a_norm[i]])[:, None, :]
        w_gates_bf = _gate_weights(wi_t, row0=gate0, n_rows=2 * d, tn=512)
        w_down_bf = _cast_bf16(w_down[i], tr=512)

        qkv_a, lat, kpe, u = _inproj(
            h, g_mix[i][None], wi_t, q_norm_a[i][None], k_norm_a[i][None], lat_gains,
            cos_a, sin_a, cos_b, sin_b, seq=seq, tm=512, tn=2 * n_kva, kr_col0=kr0)
        q_b, k_b, v_b = _qkv_b(lat, wq_pad, w_kv_b[i], kpe, cos_b, sin_b, seq=seq, tm=512)

        oa = _attention(qkv_a, qkv_a, qkv_a, batch=batch, seq=seq, n_kv=N_KV_A,
                        groups=N_HEADS_A // N_KV_A, dk=HEAD_DIM_A, dv=HEAD_DIM_A,
                        q_col0=0, k_col0=n_qa, v_col0=n_qa + n_kva, tq=512, kc=512, sq=512,
                        aug_lane=HEAD_DIM_A)
        ob = _attention(q_b, k_b, v_b, batch=batch, seq=seq, n_kv=N_HEADS_B,
                        groups=1, dk=QK_PAD_B, dv=V_DIM_B,
                        q_col0=0, k_col0=0, v_col0=0, tq=1024, kc=512, sq=512,
                        aug_lane=QK_DIM_B)

        merged = _merge(u, oa, ob, w_a_out[i], w_b_out[i], w_gates_bf, tm=512, tn=512)
        h1, un1 = _resid_mm(merged, w_o[i], h, g_ffn[i][None], tm=512, tn=512)
        act = _ffn_up(un1, w_gate_up[i], tm=1024, tn=512)
        h2, un2 = _resid_mm(act, w_down_bf, h1, g_ple[i][None], tm=512, tn=512)
        h = _ple(un2, w_ple_gate[i], h2, p[i].reshape(t, -1), w_ple_proj[i],
                 g_ple_post[i][None], g_final[None], tm=512, tn=512,
                 final_norm=(i == depth - 1))
    return h.reshape(batch, seq, d)
```

```python
import functools
import math

import numpy as np
import jax
import jax.numpy as jnp
from jax import lax
from jax.experimental import pallas as pl
from jax.experimental.pallas import tpu as pltpu

F32 = jnp.float32
BF16 = jnp.bfloat16

GRID_W = 64
ROPE_THETA = 10000.0
RMS_EPS = 1e-6

N_HEADS_A = 8
N_KV_A = 2
HEAD_DIM_A = 128
N_HEADS_B = 8
NOPE_DIM_B = 128
ROPE_DIM_B = 64
V_DIM_B = 128
QK_DIM_B = NOPE_DIM_B + ROPE_DIM_B
QK_PAD_B = 256

LANES = 128
BF16_ROWS = 16
V7X_VMEM_BYTES = 64 * 1024 * 1024
LOG2E = math.log2(math.e)


def _vmem_limit(*block_bytes, scratch=0, temps=0):
    need = 2 * sum(block_bytes) + scratch + temps + (4 << 20)
    return int(min(need, V7X_VMEM_BYTES - (6 << 20)))


def _nbytes(shape, dtype):
    return math.prod(shape) * jnp.dtype(dtype).itemsize


def _rms(x32, gain_row):
    ms = jnp.mean(x32 * x32, axis=-1, keepdims=True)
    return x32 * lax.rsqrt(ms + RMS_EPS) * gain_row


def _swap_chunks(y, chunk):
    lane = lax.broadcasted_iota(jnp.int32, y.shape, 1)
    first = (lane % (2 * chunk)) < chunk
    return jnp.where(first, pltpu.roll(y, LANES - chunk, 1), pltpu.roll(y, chunk, 1))


def _rope(y, cos, sin_signed, chunk):
    return y * cos + _swap_chunks(y, chunk) * sin_signed


def _rope_tables(seq, rot_dim):
    rows = seq // GRID_W
    row = np.repeat(np.arange(rows, dtype=np.float64), GRID_W)
    col = np.tile(np.arange(GRID_W, dtype=np.float64), rows)
    half = rot_dim // 2
    inv = 1.0 / np.power(ROPE_THETA, np.arange(0, half, 2, dtype=np.float64) / half)
    ar = row[:, None] * inv[None, :]
    ac = col[:, None] * inv[None, :]
    cos = np.concatenate([np.cos(ar), np.cos(ar), np.cos(ac), np.cos(ac)], axis=-1)
    sin = np.concatenate([-np.sin(ar), np.sin(ar), -np.sin(ac), np.sin(ac)], axis=-1)
    pad = LANES - rot_dim
    if pad:
        cos = np.concatenate([cos, np.ones((seq, pad))], axis=-1)
        sin = np.concatenate([sin, np.zeros((seq, pad))], axis=-1)
    return jnp.asarray(cos, F32), jnp.asarray(sin, F32)


def _once(i, j, nj):
    return jnp.where(i == 0, j, nj - 1)


def _cast_kernel(w_ref, o_ref):
    o_ref[...] = w_ref[...].astype(BF16)


def _cast_bf16(w, *, tr):
    r, c = w.shape
    return pl.pallas_call(
        _cast_kernel,
        grid=(r // tr,),
        in_specs=[pl.BlockSpec((tr, c), lambda i: (i, 0))],
        out_specs=pl.BlockSpec((tr, c), lambda i: (i, 0)),
        out_shape=jax.ShapeDtypeStruct((r, c), BF16),
        compiler_params=pltpu.CompilerParams(
            dimension_semantics=("parallel",),
            vmem_limit_bytes=_vmem_limit(_nbytes((tr, c), F32), _nbytes((tr, c), BF16))),
        name="cast_bf16",
    )(w)


def _gate_weights_kernel(wt_ref, o_ref):
    o_ref[...] = wt_ref[...].T.astype(BF16)


def _gate_weights(w_in_t, *, row0, n_rows, tn):
    d = w_in_t.shape[1]
    assert row0 % 8 == 0 and n_rows % tn == 0
    return pl.pallas_call(
        _gate_weights_kernel,
        grid=(n_rows // tn,),
        in_specs=[pl.BlockSpec((pl.Element(tn), pl.Element(d)), lambda j: (pl.multiple_of(row0 + j * tn, 8), 0))],
        out_specs=pl.BlockSpec((d, tn), lambda j: (0, j)),
        out_shape=jax.ShapeDtypeStruct((d, n_rows), BF16),
        compiler_params=pltpu.CompilerParams(
            dimension_semantics=("parallel",),
            vmem_limit_bytes=_vmem_limit(
                _nbytes((tn, d), F32), _nbytes((d, tn), BF16),
                temps=2 * _nbytes((d, tn), F32))),
        name="gate_weights",
    )(w_in_t)


def _inproj_kernel(x_ref, g_ref, w_ref, wkr_ref, qn_ref, kn_ref, lg_ref,
                   cosa_ref, sina_ref, cosb_ref, sinb_ref,
                   qkv_ref, lat_ref, kpe_ref, u_ref, wbf_ref, wkrbf_ref,
                   *, q_scale, q_tiles, nj):
    i = pl.program_id(0)
    j = pl.program_id(1)
    hd = HEAD_DIM_A

    @pl.when(i == 0)
    def _():
        wbf_ref[j] = w_ref[...].T.astype(BF16)

    @pl.when(jnp.logical_and(i == 0, j == 0))
    def _():
        row = lax.broadcasted_iota(jnp.int32, wkr_ref.shape, 0)
        wkrbf_ref[...] = jnp.where(row < ROPE_DIM_B, wkr_ref[...], 0.0).T.astype(BF16)

    @pl.when(j == 0)
    def _():
        u_ref[...] = _rms(x_ref[...], g_ref[...]).astype(BF16)

    z = jnp.dot(u_ref[...], wbf_ref[j], preferred_element_type=F32)

    @pl.when(j < q_tiles)
    def _():
        cos = cosa_ref[...]
        sin = sina_ref[...]
        for h in range(z.shape[1] // hd):
            y = _rms(z[:, h * hd:(h + 1) * hd], qn_ref[...])
            qkv_ref[:, h * hd:(h + 1) * hd] = (_rope(y, cos, sin, hd // 4) * q_scale).astype(BF16)

    @pl.when(j == q_tiles)
    def _():
        cos = cosa_ref[...]
        sin = sina_ref[...]
        for h in range(N_KV_A):
            y = _rms(z[:, h * hd:(h + 1) * hd], kn_ref[...])
            qkv_ref[:, h * hd:(h + 1) * hd] = _rope(y, cos, sin, hd // 4).astype(BF16)
        qkv_ref[:, N_KV_A * hd:] = z[:, N_KV_A * hd:].astype(BF16)

    @pl.when(j > q_tiles)
    def _():
        lat_ref[...] = _rms(z, lg_ref[...]).astype(BF16)

    @pl.when(j == nj - 1)
    def _():
        kr = jnp.dot(u_ref[...], wkrbf_ref[...], preferred_element_type=F32)
        kpe_ref[...] = _rope(kr, cosb_ref[...], sinb_ref[...], ROPE_DIM_B // 4).astype(BF16)


def _inproj(x2, g_mix, w_in_t, q_norm, k_norm, lat_gains, cos_a, sin_a, cos_b, sin_b,
            *, seq, tm, tn, kr_col0):
    t, d = x2.shape
    lora = lat_gains.shape[-1]
    n_qkv = (N_HEADS_A + 2 * N_KV_A) * HEAD_DIM_A
    assert tn == 2 * N_KV_A * HEAD_DIM_A and tn == lora and kr_col0 % LANES == 0
    assert t % tm == 0 and seq % tm == 0
    q_tiles = (N_HEADS_A * HEAD_DIM_A) // tn
    nj = q_tiles + 3
    sblocks = seq // tm
    q_scale = (HEAD_DIM_A ** -0.5) * LOG2E
    return pl.pallas_call(
        functools.partial(_inproj_kernel, q_scale=q_scale, q_tiles=q_tiles, nj=nj),
        grid=(t // tm, nj),
        in_specs=[
            pl.BlockSpec((tm, d), lambda i, j: (i, 0)),
            pl.BlockSpec((1, d), lambda i, j: (0, 0)),
            pl.BlockSpec((tn, d), lambda i, j: (_once(i, j, nj), 0)),
            pl.BlockSpec((LANES, d), lambda i, j: (kr_col0 // LANES, 0)),
            pl.BlockSpec((1, HEAD_DIM_A), lambda i, j: (0, 0)),
            pl.BlockSpec((1, HEAD_DIM_A), lambda i, j: (0, 0)),
            pl.BlockSpec((None, 1, lora), lambda i, j: (jnp.maximum(j - q_tiles - 1, 0), 0, 0)),
            pl.BlockSpec((tm, LANES), lambda i, j: (i % sblocks, 0)),
            pl.BlockSpec((tm, LANES), lambda i, j: (i % sblocks, 0)),
            pl.BlockSpec((tm, LANES), lambda i, j: (i % sblocks, 0)),
            pl.BlockSpec((tm, LANES), lambda i, j: (i % sblocks, 0)),
        ],
        out_specs=[
            pl.BlockSpec((tm, tn), lambda i, j: (i, jnp.minimum(j, q_tiles))),
            pl.BlockSpec((tm, lora), lambda i, j: (i, jnp.maximum(j - q_tiles - 1, 0))),
            pl.BlockSpec((tm, LANES), lambda i, j: (i, 0)),
            pl.BlockSpec((tm, d), lambda i, j: (i, 0)),
        ],
        out_shape=[
            jax.ShapeDtypeStruct((t, n_qkv), BF16),
            jax.ShapeDtypeStruct((t, 2 * lora), BF16),
            jax.ShapeDtypeStruct((t, LANES), BF16),
            jax.ShapeDtypeStruct((t, d), BF16),
        ],
        scratch_shapes=[pltpu.VMEM((nj, d, tn), BF16), pltpu.VMEM((d, LANES), BF16)],
        compiler_params=pltpu.CompilerParams(
            dimension_semantics=("arbitrary", "arbitrary"),
            vmem_limit_bytes=_vmem_limit(
                _nbytes((tm, d), F32), _nbytes((d, tn), F32), _nbytes((d, LANES), F32),
                2 * _nbytes((tm, tn), BF16), _nbytes((tm, d), BF16), 5 * _nbytes((tm, LANES), F32),
                scratch=_nbytes((nj, d, tn), BF16) + _nbytes((d, LANES), BF16),
                temps=_nbytes((d, tn), BF16) + 4 * _nbytes((tm, tn), F32)),
        ),
        name="inproj",
    )(x2, g_mix, w_in_t, w_in_t, q_norm, k_norm, lat_gains, cos_a, sin_a, cos_b, sin_b)


def _qkv_b_kernel(lat_ref, wq_ref, wkv_ref, kpe_ref, cos_ref, sin_ref,
                  q_ref, k_ref, v_ref, wqbf_ref, wkvbf_ref, *, q_scale):
    @pl.when(pl.program_id(0) == 0)
    def _():
        wqbf_ref[...] = wq_ref[...].astype(BF16)
        wkvbf_ref[...] = wkv_ref[...].astype(BF16)

    lora = wq_ref.shape[0]
    zq = jnp.dot(lat_ref[:, :lora], wqbf_ref[...], preferred_element_type=F32)
    zkv = jnp.dot(lat_ref[:, lora:], wkvbf_ref[...], preferred_element_type=F32)
    cos = cos_ref[...]
    sin = sin_ref[...]
    kpe = kpe_ref[...]
    for h in range(N_HEADS_B):
        c0 = h * QK_PAD_B
        q_ref[:, c0:c0 + LANES] = (zq[:, c0:c0 + LANES] * q_scale).astype(BF16)
        pe = _rope(zq[:, c0 + LANES:c0 + 2 * LANES], cos, sin, ROPE_DIM_B // 4)
        q_ref[:, c0 + LANES:c0 + 2 * LANES] = (pe * q_scale).astype(BF16)
        k_ref[:, c0:c0 + LANES] = zkv[:, c0:c0 + LANES].astype(BF16)
        k_ref[:, c0 + LANES:c0 + 2 * LANES] = kpe
        v_ref[:, h * V_DIM_B:(h + 1) * V_DIM_B] = zkv[:, c0 + LANES:c0 + 2 * LANES].astype(BF16)


def _qkv_b(lat, wq_pad, w_kv_b, kpe, cos, sin, *, seq, tm):
    t = lat.shape[0]
    lora = wq_pad.shape[0]
    nq = wq_pad.shape[1]
    nkv = w_kv_b.shape[1]
    nv = N_HEADS_B * V_DIM_B
    sblocks = seq // tm
    q_scale = (QK_DIM_B ** -0.5) * LOG2E
    return pl.pallas_call(
        functools.partial(_qkv_b_kernel, q_scale=q_scale),
        grid=(t // tm,),
        in_specs=[
            pl.BlockSpec((tm, 2 * lora), lambda i: (i, 0)),
            pl.BlockSpec((lora, nq), lambda i: (0, 0)),
            pl.BlockSpec((lora, nkv), lambda i: (0, 0)),
            pl.BlockSpec((tm, LANES), lambda i: (i, 0)),
            pl.BlockSpec((tm, LANES), lambda i: (i % sblocks, 0)),
            pl.BlockSpec((tm, LANES), lambda i: (i % sblocks, 0)),
        ],
        out_specs=[
            pl.BlockSpec((tm, nq), lambda i: (i, 0)),
            pl.BlockSpec((tm, nq), lambda i: (i, 0)),
            pl.BlockSpec((tm, nv), lambda i: (i, 0)),
        ],
        out_shape=[
            jax.ShapeDtypeStruct((t, nq), BF16),
            jax.ShapeDtypeStruct((t, nq), BF16),
            jax.ShapeDtypeStruct((t, nv), BF16),
        ],
        scratch_shapes=[pltpu.VMEM((lora, nq), BF16), pltpu.VMEM((lora, nkv), BF16)],
        compiler_params=pltpu.CompilerParams(
            dimension_semantics=("arbitrary",),
            vmem_limit_bytes=_vmem_limit(
                _nbytes((tm, 2 * lora), BF16), _nbytes((lora, nq), F32), _nbytes((lora, nkv), F32),
                2 * _nbytes((tm, nq), BF16), _nbytes((tm, nv), BF16), 3 * _nbytes((tm, LANES), F32),
                scratch=_nbytes((lora, nq), BF16) + _nbytes((lora, nkv), BF16),
                temps=3 * _nbytes((tm, nq), F32)),
        ),
        name="qkv_b",
    )(lat, wq_pad, w_kv_b, kpe, cos, sin)


_L_SAFE = 2.0 ** -64


def _attn_kernel(q_ref, k_ref, v_ref, o_ref, kaug_ref, vt_ref, kmax_ref,
                 qa_ref, s_ref, acc_ref, *, groups, dk, dv, kc, sq, aug_lane):
    seq = k_ref.shape[0]
    tq = q_ref.shape[0]
    dka = kaug_ref.shape[1]
    append_tile = aug_lane == dk

    @pl.when(pl.program_id(2) == 0)
    def _():
        kf = k_ref[...].astype(F32)
        kmax2 = jnp.max(jnp.sum(kf * kf, axis=1, keepdims=True), axis=0, keepdims=True)
        kmax_ref[...] = jnp.broadcast_to(kmax2, kmax_ref.shape)
        if append_tile:
            kaug_ref[:, :dk] = k_ref[...]
            lane = lax.broadcasted_iota(jnp.int32, (seq, dka - dk), 1)
            kaug_ref[:, dk:] = jnp.where(lane == 0, 1.0, 0.0).astype(BF16)
        else:
            lane = lax.broadcasted_iota(jnp.int32, (seq, dk), 1)
            kaug_ref[...] = jnp.where(lane == aug_lane, 1.0, kf).astype(BF16)
        vt_ref[:dv, :] = v_ref[...].astype(F32).T.astype(BF16)
        vt_ref[dv:, :] = jnp.ones((vt_ref.shape[0] - dv, seq), BF16)

    streams = [(r, g) for r in range(tq // sq) for g in range(groups)]

    def q_aug(r, g, shift_col):
        q = q_ref[r * sq:(r + 1) * sq, g * dk:(g + 1) * dk]
        neg = -shift_col
        if append_tile:
            lane = lax.broadcasted_iota(jnp.int32, (sq, dka - dk), 1)
            tile = jnp.where(lane == 0, neg, 0.0).astype(BF16)
            return jnp.concatenate([q, tile], axis=1)
        lane = lax.broadcasted_iota(jnp.int32, (sq, dk), 1)
        return jnp.where(lane == aug_lane, neg, q.astype(F32)).astype(BF16)

    def run(qas):
        n_chunks = seq // kc
        n_str = len(qas)
        for i, qa in enumerate(qas):
            qa_ref[i] = qa.astype(F32).T.astype(BF16)
        acc_ref[...] = jnp.zeros(acc_ref.shape, F32)

        def scores(i, c, slot):
            start = c * kc if isinstance(c, int) else pl.multiple_of(c * kc, kc)
            s_ref[slot, i] = jnp.dot(kaug_ref[pl.ds(start, kc), :], qa_ref[i],
                                     preferred_element_type=F32)

        def accumulate(i, c, slot):
            start = c * kc if isinstance(c, int) else pl.multiple_of(c * kc, kc)
            p = jnp.exp2(s_ref[slot, i]).astype(BF16)
            acc_ref[i] += jnp.dot(vt_ref[:, pl.ds(start, kc)], p, preferred_element_type=F32)

        def half(c_next, slot_next, c_cur, slot_cur):
            for i in range(n_str):
                if c_next is not None:
                    scores(i, c_next, slot_next)
                accumulate(i, c_cur, slot_cur)

        def body(c2, carry):
            c = 2 * c2
            half(c + 1, 1, c, 0)
            half(c + 2, 0, c + 1, 1)
            return carry

        for i in range(n_str):
            scores(i, 0, 0)
        lax.fori_loop(0, n_chunks // 2 - 1, body, 0)
        half(n_chunks - 1, 1, n_chunks - 2, 0)
        half(None, None, n_chunks - 1, 1)
        return tuple((acc_ref[i, :dv, :], acc_ref[i, dv:dv + 1, :]) for i in range(n_str))

    def write_out(results):
        for (r, g), (acc, l) in zip(streams, results):
            o_ref[r * sq:(r + 1) * sq, g * dv:(g + 1) * dv] = (acc * (1.0 / l)).T.astype(BF16)

    kmax2 = kmax_ref[0:1, 0:1]
    bound_qas = []
    for r, g in streams:
        qf = q_ref[r * sq:(r + 1) * sq, g * dk:(g + 1) * dk].astype(F32)
        shift = jnp.sqrt(jnp.sum(qf * qf, axis=1, keepdims=True) * kmax2)
        bound_qas.append(q_aug(r, g, shift))
    fast = run(bound_qas)
    write_out(fast)

    l_min = None
    for _, l in fast:
        cur = jnp.min(l)
        l_min = cur if l_min is None else jnp.minimum(l_min, cur)

    @pl.when(jnp.logical_not(l_min >= _L_SAFE))
    def _():
        zero_qas = [q_aug(r, g, jnp.zeros((sq, 1), F32)) for r, g in streams]

        def max_body(c, ms):
            start = c * kc if isinstance(c, int) else pl.multiple_of(c * kc, kc)
            ks = kaug_ref[pl.ds(start, kc), :]
            return tuple(
                jnp.maximum(m, jnp.max(
                    lax.dot_general(qa, ks, (((1,), (1,)), ((), ())),
                                    preferred_element_type=F32), axis=1, keepdims=True))
                for qa, m in zip(zero_qas, ms))

        ms = lax.fori_loop(0, seq // kc, max_body,
                           tuple(jnp.full((sq, 1), -jnp.inf, F32) for _ in streams))
        write_out(run([q_aug(r, g, m) for (r, g), m in zip(streams, ms)]))


def _attention(q_arr, k_arr, v_arr, *, batch, seq, n_kv, groups, dk, dv,
               q_col0, k_col0, v_col0, tq, kc, sq, aug_lane):
    t = batch * seq
    nq = seq // tq
    qb0 = q_col0 // (groups * dk)
    kb0 = k_col0 // dk
    vb0 = v_col0 // dv
    dka = dk + LANES if aug_lane == dk else dk
    n_str = (tq // sq) * groups
    assert aug_lane <= dk and tq % sq == 0 and seq % (2 * kc) == 0 and seq // kc >= 4
    dvx = dv + BF16_ROWS
    scratch_bytes = (_nbytes((seq, dka), BF16) + _nbytes((dvx, seq), BF16)
                     + _nbytes((n_str, dka, sq), BF16) + _nbytes((2, n_str, kc, sq), F32)
                     + _nbytes((n_str, dvx, sq), F32))
    return pl.pallas_call(
        functools.partial(_attn_kernel, groups=groups, dk=dk, dv=dv, kc=kc, sq=sq,
                          aug_lane=aug_lane),
        grid=(batch, n_kv, nq),
        in_specs=[
            pl.BlockSpec((tq, groups * dk), lambda b, h, i: (b * nq + i, qb0 + h)),
            pl.BlockSpec((seq, dk), lambda b, h, i: (b, kb0 + h)),
            pl.BlockSpec((seq, dv), lambda b, h, i: (b, vb0 + h)),
        ],
        out_specs=pl.BlockSpec((tq, groups * dv), lambda b, h, i: (b * nq + i, h)),
        out_shape=jax.ShapeDtypeStruct((t, n_kv * groups * dv), BF16),
        scratch_shapes=[pltpu.VMEM((seq, dka), BF16), pltpu.VMEM((dvx, seq), BF16),
                        pltpu.VMEM((8, LANES), F32), pltpu.VMEM((n_str, dka, sq), BF16),
                        pltpu.VMEM((2, n_str, kc, sq), F32), pltpu.VMEM((n_str, dvx, sq), F32)],
        compiler_params=pltpu.CompilerParams(
            dimension_semantics=("parallel", "parallel", "arbitrary"),
            vmem_limit_bytes=_vmem_limit(
                _nbytes((tq, groups * dk), BF16), _nbytes((seq, dk), BF16),
                _nbytes((seq, dv), BF16), _nbytes((tq, groups * dv), BF16),
                scratch=scratch_bytes,
                temps=2 * _nbytes((seq, dk), F32) + 2 * _nbytes((seq, dv), F32)
                + 4 * _nbytes((kc, sq), F32)),
        ),
        name="attention",
    )(q_arr, k_arr, v_arr)


def _merge_kernel(u_ref, oa_ref, ob_ref, wa_ref, wb_ref, wga_ref, wgb_ref, o_ref,
                  wabf_ref, wbbf_ref):
    i = pl.program_id(0)
    j = pl.program_id(1)

    @pl.when(i == 0)
    def _():
        wabf_ref[j] = wa_ref[...].astype(BF16)
        wbbf_ref[j] = wb_ref[...].astype(BF16)

    u = u_ref[...]
    ya = jnp.dot(oa_ref[...], wabf_ref[j], preferred_element_type=F32)
    yb = jnp.dot(ob_ref[...], wbbf_ref[j], preferred_element_type=F32)
    ga = jnp.dot(u, wga_ref[...], preferred_element_type=F32)
    gb = jnp.dot(u, wgb_ref[...], preferred_element_type=F32)
    o_ref[...] = (jax.nn.sigmoid(ga) * ya + jax.nn.sigmoid(gb) * yb).astype(BF16)


def _merge(u, oa, ob, w_a_out, w_b_out, w_gates_bf, *, tm, tn):
    t, d = u.shape
    ka = oa.shape[1]
    kb = ob.shape[1]
    nj = d // tn
    return pl.pallas_call(
        _merge_kernel,
        grid=(t // tm, nj),
        in_specs=[
            pl.BlockSpec((tm, d), lambda i, j: (i, 0)),
            pl.BlockSpec((tm, ka), lambda i, j: (i, 0)),
            pl.BlockSpec((tm, kb), lambda i, j: (i, 0)),
            pl.BlockSpec((ka, tn), lambda i, j: (0, _once(i, j, nj))),
            pl.BlockSpec((kb, tn), lambda i, j: (0, _once(i, j, nj))),
            pl.BlockSpec((d, tn), lambda i, j: (0, j)),
            pl.BlockSpec((d, tn), lambda i, j: (0, nj + j)),
        ],
        out_specs=pl.BlockSpec((tm, tn), lambda i, j: (i, j)),
        out_shape=jax.ShapeDtypeStruct((t, d), BF16),
        scratch_shapes=[pltpu.VMEM((nj, ka, tn), BF16), pltpu.VMEM((nj, kb, tn), BF16)],
        compiler_params=pltpu.CompilerParams(
            dimension_semantics=("arbitrary", "arbitrary"),
            vmem_limit_bytes=_vmem_limit(
                _nbytes((tm, d), BF16), _nbytes((tm, ka), BF16), _nbytes((tm, kb), BF16),
                _nbytes((ka, tn), F32), _nbytes((kb, tn), F32), 2 * _nbytes((d, tn), BF16),
                _nbytes((tm, tn), BF16),
                scratch=_nbytes((nj, ka, tn), BF16) + _nbytes((nj, kb, tn), BF16),
                temps=_nbytes((ka + kb, tn), BF16) + 6 * _nbytes((tm, tn), F32)),
        ),
        name="merge",
    )(u, oa, ob, w_a_out, w_b_out, w_gates_bf, w_gates_bf)


def _resid_mm_kernel(lhs_ref, w_ref, resid_ref, g_ref, h_ref, un_ref, *scratch, tn, nj):
    i = pl.program_id(0)
    j = pl.program_id(1)
    if scratch:
        (wbf_ref,) = scratch

        @pl.when(i == 0)
        def _():
            wbf_ref[j] = w_ref[...].astype(BF16)

        w = wbf_ref[j]
    else:
        w = w_ref[...]
    val = resid_ref[...] + jnp.dot(lhs_ref[...], w, preferred_element_type=F32)
    for jj in range(nj):
        @pl.when(j == jj)
        def _(jj=jj):
            h_ref[:, jj * tn:(jj + 1) * tn] = val

    @pl.when(j == nj - 1)
    def _():
        un_ref[...] = _rms(h_ref[...], g_ref[...]).astype(BF16)


def _resid_mm(lhs, w, resid, gain, *, tm, tn):
    t, k = lhs.shape
    d = w.shape[1]
    nj = d // tn
    resident = w.dtype == F32
    if resident:
        w_spec = pl.BlockSpec((k, tn), lambda i, j: (0, _once(i, j, nj)))
        scratch_shapes = [pltpu.VMEM((nj, k, tn), BF16)]
        scratch_bytes = _nbytes((nj, k, tn), BF16)
    else:
        w_spec = pl.BlockSpec((k, tn), lambda i, j: (0, j))
        scratch_shapes = []
        scratch_bytes = 0
    return pl.pallas_call(
        functools.partial(_resid_mm_kernel, tn=tn, nj=nj),
        grid=(t // tm, nj),
        in_specs=[
            pl.BlockSpec((tm, k), lambda i, j: (i, 0)),
            w_spec,
            pl.BlockSpec((tm, tn), lambda i, j: (i, j)),
            pl.BlockSpec((1, d), lambda i, j: (0, 0)),
        ],
        out_specs=[
            pl.BlockSpec((tm, d), lambda i, j: (i, 0)),
            pl.BlockSpec((tm, d), lambda i, j: (i, 0)),
        ],
        out_shape=[
            jax.ShapeDtypeStruct((t, d), F32),
            jax.ShapeDtypeStruct((t, d), BF16),
        ],
        scratch_shapes=scratch_shapes,
        compiler_params=pltpu.CompilerParams(
            dimension_semantics=("arbitrary", "arbitrary"),
            vmem_limit_bytes=_vmem_limit(
                _nbytes((tm, k), BF16), _nbytes((k, tn), w.dtype), _nbytes((tm, tn), F32),
                _nbytes((tm, d), F32), _nbytes((tm, d), BF16),
                scratch=scratch_bytes,
                temps=_nbytes((k, tn), BF16) + 2 * _nbytes((tm, tn), F32)
                + 2 * _nbytes((tm, d), F32)),
        ),
        name="resid_mm",
    )(lhs, w, resid, gain)


def _ffn_up_kernel(un_ref, wg_ref, wu_ref, o_ref):
    un = un_ref[...]
    g = jnp.dot(un, wg_ref[...].astype(BF16), preferred_element_type=F32)
    u = jnp.dot(un, wu_ref[...].astype(BF16), preferred_element_type=F32)
    o_ref[...] = (jax.nn.silu(g) * u).astype(BF16)


def _ffn_up(un, w_gate_up, *, tm, tn):
    t, d = un.shape
    d_ff = w_gate_up.shape[1] // 2
    nj = d_ff // tn
    return pl.pallas_call(
        _ffn_up_kernel,
        grid=(t // tm, nj),
        in_specs=[
            pl.BlockSpec((tm, d), lambda i, j: (i, 0)),
            pl.BlockSpec((d, tn), lambda i, j: (0, j)),
            pl.BlockSpec((d, tn), lambda i, j: (0, nj + j)),
        ],
        out_specs=pl.BlockSpec((tm, tn), lambda i, j: (i, j)),
        out_shape=jax.ShapeDtypeStruct((t, d_ff), BF16),
        compiler_params=pltpu.CompilerParams(
            dimension_semantics=("parallel", "arbitrary"),
            vmem_limit_bytes=_vmem_limit(
                _nbytes((tm, d), BF16), 2 * _nbytes((d, tn), F32), _nbytes((tm, tn), BF16),
                temps=2 * _nbytes((d, tn), BF16) + 4 * _nbytes((tm, tn), F32)),
        ),
        name="ffn_up",
    )(un, w_gate_up, w_gate_up)


def _ple_kernel(un_ref, w_ref, h_ref, p_ref, wp_ref, gpost_ref, gfin_ref,
                o_ref, e_ref, wbf_ref, *, tn, nj, final_norm):
    i = pl.program_id(0)
    j = pl.program_id(1)

    @pl.when(i == 0)
    def _():
        wbf_ref[j] = w_ref[...].astype(BF16)

    @pl.when(j == 0)
    def _():
        e = jnp.dot(p_ref[...].astype(BF16), wp_ref[...].astype(BF16),
                    preferred_element_type=F32)
        e_ref[...] = _rms(e, gpost_ref[...])

    gate = jax.nn.sigmoid(jnp.dot(un_ref[...], wbf_ref[j], preferred_element_type=F32))
    for jj in range(nj):
        @pl.when(j == jj)
        def _(jj=jj):
            o_ref[:, jj * tn:(jj + 1) * tn] = h_ref[...] + gate * e_ref[:, jj * tn:(jj + 1) * tn]

    if final_norm:
        @pl.when(j == nj - 1)
        def _():
            o_ref[...] = _rms(o_ref[...], gfin_ref[...])


def _ple(un, w_gate, h, p2, w_proj, g_post, g_final, *, tm, tn, final_norm):
    t, d = un.shape
    pd = p2.shape[1]
    nj = d // tn
    return pl.pallas_call(
        functools.partial(_ple_kernel, tn=tn, nj=nj, final_norm=final_norm),
        grid=(t // tm, nj),
        in_specs=[
            pl.BlockSpec((tm, d), lambda i, j: (i, 0)),
            pl.BlockSpec((d, tn), lambda i, j: (0, _once(i, j, nj))),
            pl.BlockSpec((tm, tn), lambda i, j: (i, j)),
            pl.BlockSpec((tm, pd), lambda i, j: (i, 0)),
            pl.BlockSpec((pd, d), lambda i, j: (0, 0)),
            pl.BlockSpec((1, d), lambda i, j: (0, 0)),
            pl.BlockSpec((1, d), lambda i, j: (0, 0)),
        ],
        out_specs=pl.BlockSpec((tm, d), lambda i, j: (i, 0)),
        out_shape=jax.ShapeDtypeStruct((t, d), F32),
        scratch_shapes=[pltpu.VMEM((tm, d), F32), pltpu.VMEM((nj, d, tn), BF16)],
        compiler_params=pltpu.CompilerParams(
            dimension_semantics=("arbitrary", "arbitrary"),
            vmem_limit_bytes=_vmem_limit(
                _nbytes((tm, d), BF16), _nbytes((d, tn), F32), _nbytes((tm, tn), F32),
                _nbytes((tm, pd), F32), _nbytes((pd, d), F32), _nbytes((tm, d), F32),
                scratch=_nbytes((tm, d), F32) + _nbytes((nj, d, tn), BF16),
                temps=_nbytes((d, tn), BF16) + _nbytes((pd, d), BF16)
                + 2 * _nbytes((tm, tn), F32) + 2 * _nbytes((tm, d), F32)),
        ),
        name="ple_final",
    )(un, w_gate, h, p2, w_proj, g_post, g_final)


def kernel(x, p, g_mix, w_in, q_norm_a, k_norm_a, q_a_norm, w_q_b, kv_a_norm, w_kv_b,
           w_a_out, w_b_out, w_o, g_ffn, w_gate_up, w_down, g_ple, w_ple_gate,
           w_ple_proj, g_ple_post, g_final):
    batch, seq, d = x.shape
    depth = w_in.shape[0]
    t = batch * seq
    q_lora = q_a_norm.shape[-1]
    kv_lora = kv_a_norm.shape[-1]
    assert q_lora == kv_lora
    n_qa = N_HEADS_A * HEAD_DIM_A
    n_kva = N_KV_A * HEAD_DIM_A
    lat0 = n_qa + 2 * n_kva
    kr0 = lat0 + q_lora + kv_lora
    gate0 = kr0 + ROPE_DIM_B

    cos_a, sin_a = _rope_tables(seq, HEAD_DIM_A)
    cos_b, sin_b = _rope_tables(seq, ROPE_DIM_B)

    h = x.reshape(t, d)
    for i in range(depth):
        wi_t = jnp.swapaxes(w_in[i], 0, 1)
        wq_pad = jnp.pad(
            w_q_b[i].reshape(q_lora, N_HEADS_B, QK_DIM_B),
            ((0, 0), (0, 0), (0, QK_PAD_B - QK_DIM_B))).reshape(q_lora, N_HEADS_B * QK_PAD_B)
        lat_gains = jnp.stack([q_a_norm[i], kv_a_norm[i]])[:, None, :]
        w_gates_bf = _gate_weights(wi_t, row0=gate0, n_rows=2 * d, tn=512)
        w_down_bf = _cast_bf16(w_down[i], tr=512)

        qkv_a, lat, kpe, u = _inproj(
            h, g_mix[i][None], wi_t, q_norm_a[i][None], k_norm_a[i][None], lat_gains,
            cos_a, sin_a, cos_b, sin_b, seq=seq, tm=512, tn=2 * n_kva, kr_col0=kr0)
        q_b, k_b, v_b = _qkv_b(lat, wq_pad, w_kv_b[i], kpe, cos_b, sin_b, seq=seq, tm=512)

        oa = _attention(qkv_a, qkv_a, qkv_a, batch=batch, seq=seq, n_kv=N_KV_A,
                        groups=N_HEADS_A // N_KV_A, dk=HEAD_DIM_A, dv=HEAD_DIM_A,
                        q_col0=0, k_col0=n_qa, v_col0=n_qa + n_kva, tq=512, kc=512, sq=512,
                        aug_lane=HEAD_DIM_A)
        ob = _attention(q_b, k_b, v_b, batch=batch, seq=seq, n_kv=N_HEADS_B,
                        groups=1, dk=QK_PAD_B, dv=V_DIM_B,
                        q_col0=0, k_col0=0, v_col0=0, tq=2048, kc=512, sq=512,
                        aug_lane=QK_DIM_B)

        merged = _merge(u, oa, ob, w_a_out[i], w_b_out[i], w_gates_bf, tm=512, tn=512)
        h1, un1 = _resid_mm(merged, w_o[i], h, g_ffn[i][None], tm=512, tn=512)
        act = _ffn_up(un1, w_gate_up[i], tm=1024, tn=512)
        h2, un2 = _resid_mm(act, w_down_bf, h1, g_ple[i][None], tm=512, tn=512)
        h = _ple(un2, w_ple_gate[i], h2, p[i].reshape(t, -1), w_ple_proj[i],
                 g_ple_post[i][None], g_final[None], tm=512, tn=512,
                 final_norm=(i == depth - 1))
    return h.reshape(batch, seq, d)
```

```python
import functools
import math

import numpy as np
import jax
import jax.numpy as jnp
from jax import lax
from jax.experimental import pallas as pl
from jax.experimental.pallas import tpu as pltpu

F32 = jnp.float32
BF16 = jnp.bfloat16

GRID_W = 64
ROPE_THETA = 10000.0
RMS_EPS = 1e-6

N_HEADS_A = 8
N_KV_A = 2
HEAD_DIM_A = 128
N_HEADS_B = 8
NOPE_DIM_B = 128
ROPE_DIM_B = 64
V_DIM_B = 128
QK_DIM_B = NOPE_DIM_B + ROPE_DIM_B
QK_PAD_B = 256

LANES = 128
BF16_ROWS = 16
V7X_VMEM_BYTES = 64 * 1024 * 1024
LOG2E = math.log2(math.e)


def _vmem_limit(*block_bytes, scratch=0, temps=0):
    need = 2 * sum(block_bytes) + scratch + temps + (4 << 20)
    return int(min(need, V7X_VMEM_BYTES - (6 << 20)))


def _nbytes(shape, dtype):
    return math.prod(shape) * jnp.dtype(dtype).itemsize


def _rms(x32, gain_row):
    ms = jnp.mean(x32 * x32, axis=-1, keepdims=True)
    return x32 * lax.rsqrt(ms + RMS_EPS) * gain_row


def _swap_chunks(y, chunk):
    lane = lax.broadcasted_iota(jnp.int32, y.shape, 1)
    first = (lane % (2 * chunk)) < chunk
    return jnp.where(first, pltpu.roll(y, LANES - chunk, 1), pltpu.roll(y, chunk, 1))


def _rope(y, cos, sin_signed, chunk):
    return y * cos + _swap_chunks(y, chunk) * sin_signed


def _rope_tables(seq, rot_dim):
    rows = seq // GRID_W
    row = np.repeat(np.arange(rows, dtype=np.float64), GRID_W)
    col = np.tile(np.arange(GRID_W, dtype=np.float64), rows)
    half = rot_dim // 2
    inv = 1.0 / np.power(ROPE_THETA, np.arange(0, half, 2, dtype=np.float64) / half)
    ar = row[:, None] * inv[None, :]
    ac = col[:, None] * inv[None, :]
    cos = np.concatenate([np.cos(ar), np.cos(ar), np.cos(ac), np.cos(ac)], axis=-1)
    sin = np.concatenate([-np.sin(ar), np.sin(ar), -np.sin(ac), np.sin(ac)], axis=-1)
    pad = LANES - rot_dim
    if pad:
        cos = np.concatenate([cos, np.ones((seq, pad))], axis=-1)
        sin = np.concatenate([sin, np.zeros((seq, pad))], axis=-1)
    return jnp.asarray(cos, F32), jnp.asarray(sin, F32)


def _cast_kernel(w_ref, o_ref):
    o_ref[...] = w_ref[...].astype(BF16)


def _cast_bf16(w, *, tr):
    r, c = w.shape
    return pl.pallas_call(
        _cast_kernel,
        grid=(r // tr,),
        in_specs=[pl.BlockSpec((tr, c), lambda i: (i, 0))],
        out_specs=pl.BlockSpec((tr, c), lambda i: (i, 0)),
        out_shape=jax.ShapeDtypeStruct((r, c), BF16),
        compiler_params=pltpu.CompilerParams(
            dimension_semantics=("parallel",),
            vmem_limit_bytes=_vmem_limit(_nbytes((tr, c), F32), _nbytes((tr, c), BF16))),
        name="cast_bf16",
    )(w)


def _stage_bf16(w_hbm, dst_ref, stage_ref, sem_ref, *, row0=0, n_rows=None, transpose=False,
                keep_rows_last=None):
    rows = stage_ref.shape[1]
    n_rows = w_hbm.shape[0] if n_rows is None else n_rows
    n_chunks = n_rows // rows

    def chunk_copy(r, slot):
        return pltpu.make_async_copy(w_hbm.at[pl.ds(row0 + r * rows, rows), :],
                                     stage_ref.at[slot], sem_ref.at[slot])

    chunk_copy(0, 0).start()
    for r in range(n_chunks):
        slot = r % 2
        if r + 1 < n_chunks:
            chunk_copy(r + 1, 1 - slot).start()
        chunk_copy(r, slot).wait()
        blk = stage_ref[slot]
        if keep_rows_last is not None and r == n_chunks - 1:
            row = lax.broadcasted_iota(jnp.int32, blk.shape, 0)
            blk = jnp.where(row < keep_rows_last, blk, 0.0)
        if transpose:
            dst_ref[:, r * rows:(r + 1) * rows] = blk.T.astype(BF16)
        else:
            dst_ref[r * rows:(r + 1) * rows, :] = blk.astype(BF16)


def _inproj_kernel(x_ref, g_ref, wt_hbm, qn_ref, kn_ref, lg_ref,
                   cosa_ref, sina_ref, cosb_ref, sinb_ref,
                   qkv_ref, lat_ref, kpe_ref, u_ref, wbf_ref, stage_ref, sem_ref,
                   *, q_scale, n_q, n_kv, lora, kr_rows):
    hd = HEAD_DIM_A

    @pl.when(pl.program_id(0) == 0)
    def _():
        _stage_bf16(wt_hbm, wbf_ref, stage_ref, sem_ref, n_rows=wbf_ref.shape[1],
                    transpose=True, keep_rows_last=kr_rows)

    u = _rms(x_ref[...], g_ref[...]).astype(BF16)
    u_ref[...] = u
    cos = cosa_ref[...]
    sin = sina_ref[...]

    def proj(c0, width):
        return jnp.dot(u, wbf_ref[:, c0:c0 + width], preferred_element_type=F32)

    heads_per_dot = 4
    for h0 in range(0, n_q, heads_per_dot):
        z = proj(h0 * hd, heads_per_dot * hd)
        for h in range(heads_per_dot):
            y = _rms(z[:, h * hd:(h + 1) * hd], qn_ref[...])
            qkv_ref[:, (h0 + h) * hd:(h0 + h + 1) * hd] = (
                _rope(y, cos, sin, hd // 4) * q_scale).astype(BF16)

    z = proj(n_q * hd, 2 * n_kv * hd)
    for h in range(n_kv):
        y = _rms(z[:, h * hd:(h + 1) * hd], kn_ref[...])
        qkv_ref[:, (n_q + h) * hd:(n_q + h + 1) * hd] = _rope(y, cos, sin, hd // 4).astype(BF16)
    qkv_ref[:, (n_q + n_kv) * hd:] = z[:, n_kv * hd:].astype(BF16)

    lat0 = (n_q + 2 * n_kv) * hd
    for k in range(2):
        z = proj(lat0 + k * lora, lora)
        lat_ref[:, k * lora:(k + 1) * lora] = _rms(z, lg_ref[k]).astype(BF16)

    kr = proj(lat0 + 2 * lora, LANES)
    kpe_ref[...] = _rope(kr, cosb_ref[...], sinb_ref[...], ROPE_DIM_B // 4).astype(BF16)


def _inproj(x2, g_mix, w_in_t, q_norm, k_norm, lat_gains, cos_a, sin_a, cos_b, sin_b,
            *, seq, tm, kr_col0):
    t, d = x2.shape
    lora = lat_gains.shape[-1]
    n_qkv = (N_HEADS_A + 2 * N_KV_A) * HEAD_DIM_A
    n_cols = n_qkv + 2 * lora + LANES
    assert kr_col0 == n_qkv + 2 * lora and n_cols % LANES == 0 and N_HEADS_A % 4 == 0
    assert t % tm == 0 and seq % tm == 0
    sblocks = seq // tm
    q_scale = (HEAD_DIM_A ** -0.5) * LOG2E
    return pl.pallas_call(
        functools.partial(_inproj_kernel, q_scale=q_scale, n_q=N_HEADS_A, n_kv=N_KV_A,
                          lora=lora, kr_rows=ROPE_DIM_B),
        grid=(t // tm,),
        in_specs=[
            pl.BlockSpec((tm, d), lambda i: (i, 0)),
            pl.BlockSpec((1, d), lambda i: (0, 0)),
            pl.BlockSpec(memory_space=pl.ANY),
            pl.BlockSpec((1, HEAD_DIM_A), lambda i: (0, 0)),
            pl.BlockSpec((1, HEAD_DIM_A), lambda i: (0, 0)),
            pl.BlockSpec((2, 1, lora), lambda i: (0, 0, 0)),
            pl.BlockSpec((tm, LANES), lambda i: (i % sblocks, 0)),
            pl.BlockSpec((tm, LANES), lambda i: (i % sblocks, 0)),
            pl.BlockSpec((tm, LANES), lambda i: (i % sblocks, 0)),
            pl.BlockSpec((tm, LANES), lambda i: (i % sblocks, 0)),
        ],
        out_specs=[
            pl.BlockSpec((tm, n_qkv), lambda i: (i, 0)),
            pl.BlockSpec((tm, 2 * lora), lambda i: (i, 0)),
            pl.BlockSpec((tm, LANES), lambda i: (i, 0)),
            pl.BlockSpec((tm, d), lambda i: (i, 0)),
        ],
        out_shape=[
            jax.ShapeDtypeStruct((t, n_qkv), BF16),
            jax.ShapeDtypeStruct((t, 2 * lora), BF16),
            jax.ShapeDtypeStruct((t, LANES), BF16),
            jax.ShapeDtypeStruct((t, d), BF16),
        ],
        scratch_shapes=[pltpu.VMEM((d, n_cols), BF16), pltpu.VMEM((2, LANES, d), F32),
                        pltpu.SemaphoreType.DMA((2,))],
        compiler_params=pltpu.CompilerParams(
            dimension_semantics=("arbitrary",),
            vmem_limit_bytes=_vmem_limit(
                _nbytes((tm, d), F32), _nbytes((tm, n_qkv), BF16), _nbytes((tm, 2 * lora), BF16),
                _nbytes((tm, d), BF16), 5 * _nbytes((tm, LANES), F32),
                scratch=_nbytes((d, n_cols), BF16) + _nbytes((2, LANES, d), F32),
                temps=_nbytes((tm, d), F32) + 6 * _nbytes((tm, 4 * HEAD_DIM_A), F32)),
        ),
        name="inproj",
    )(x2, g_mix, w_in_t, q_norm, k_norm, lat_gains, cos_a, sin_a, cos_b, sin_b)


def _qkv_b_kernel(lat_ref, wq_ref, wkv_ref, kpe_ref, cos_ref, sin_ref,
                  q_ref, k_ref, v_ref, wqbf_ref, wkvbf_ref, *, q_scale):
    @pl.when(pl.program_id(0) == 0)
    def _():
        wqbf_ref[...] = wq_ref[...].astype(BF16)
        wkvbf_ref[...] = wkv_ref[...].astype(BF16)

    lora = wq_ref.shape[0]
    zq = jnp.dot(lat_ref[:, :lora], wqbf_ref[...], preferred_element_type=F32)
    zkv = jnp.dot(lat_ref[:, lora:], wkvbf_ref[...], preferred_element_type=F32)
    cos = cos_ref[...]
    sin = sin_ref[...]
    kpe = kpe_ref[...]
    for h in range(N_HEADS_B):
        c0 = h * QK_PAD_B
        q_ref[:, c0:c0 + LANES] = (zq[:, c0:c0 + LANES] * q_scale).astype(BF16)
        pe = _rope(zq[:, c0 + LANES:c0 + 2 * LANES], cos, sin, ROPE_DIM_B // 4)
        q_ref[:, c0 + LANES:c0 + 2 * LANES] = (pe * q_scale).astype(BF16)
        k_ref[:, c0:c0 + LANES] = zkv[:, c0:c0 + LANES].astype(BF16)
        k_ref[:, c0 + LANES:c0 + 2 * LANES] = kpe
        v_ref[:, h * V_DIM_B:(h + 1) * V_DIM_B] = zkv[:, c0 + LANES:c0 + 2 * LANES].astype(BF16)


def _qkv_b(lat, wq_pad, w_kv_b, kpe, cos, sin, *, seq, tm):
    t = lat.shape[0]
    lora = wq_pad.shape[0]
    nq = wq_pad.shape[1]
    nkv = w_kv_b.shape[1]
    nv = N_HEADS_B * V_DIM_B
    sblocks = seq // tm
    q_scale = (QK_DIM_B ** -0.5) * LOG2E
    return pl.pallas_call(
        functools.partial(_qkv_b_kernel, q_scale=q_scale),
        grid=(t // tm,),
        in_specs=[
            pl.BlockSpec((tm, 2 * lora), lambda i: (i, 0)),
            pl.BlockSpec((lora, nq), lambda i: (0, 0)),
            pl.BlockSpec((lora, nkv), lambda i: (0, 0)),
            pl.BlockSpec((tm, LANES), lambda i: (i, 0)),
            pl.BlockSpec((tm, LANES), lambda i: (i % sblocks, 0)),
            pl.BlockSpec((tm, LANES), lambda i: (i % sblocks, 0)),
        ],
        out_specs=[
            pl.BlockSpec((tm, nq), lambda i: (i, 0)),
            pl.BlockSpec((tm, nq), lambda i: (i, 0)),
            pl.BlockSpec((tm, nv), lambda i: (i, 0)),
        ],
        out_shape=[
            jax.ShapeDtypeStruct((t, nq), BF16),
            jax.ShapeDtypeStruct((t, nq), BF16),
            jax.ShapeDtypeStruct((t, nv), BF16),
        ],
        scratch_shapes=[pltpu.VMEM((lora, nq), BF16), pltpu.VMEM((lora, nkv), BF16)],
        compiler_params=pltpu.CompilerParams(
            dimension_semantics=("arbitrary",),
            vmem_limit_bytes=_vmem_limit(
                _nbytes((tm, 2 * lora), BF16), _nbytes((lora, nq), F32), _nbytes((lora, nkv), F32),
                2 * _nbytes((tm, nq), BF16), _nbytes((tm, nv), BF16), 3 * _nbytes((tm, LANES), F32),
                scratch=_nbytes((lora, nq), BF16) + _nbytes((lora, nkv), BF16),
                temps=3 * _nbytes((tm, nq), F32)),
        ),
        name="qkv_b",
    )(lat, wq_pad, w_kv_b, kpe, cos, sin)


_L_SAFE = 2.0 ** -64


def _attn_kernel(q_ref, k_ref, v_ref, o_ref, kaug_ref, vt_ref, kmax_ref,
                 qa_ref, s_ref, acc_ref, *, groups, dk, dv, kc, sq, aug_lane):
    seq = k_ref.shape[0]
    tq = q_ref.shape[0]
    dka = kaug_ref.shape[1]
    append_tile = aug_lane == dk

    @pl.when(pl.program_id(2) == 0)
    def _():
        kf = k_ref[...].astype(F32)
        kmax2 = jnp.max(jnp.sum(kf * kf, axis=1, keepdims=True), axis=0, keepdims=True)
        kmax_ref[...] = jnp.broadcast_to(kmax2, kmax_ref.shape)
        if append_tile:
            kaug_ref[:, :dk] = k_ref[...]
            lane = lax.broadcasted_iota(jnp.int32, (seq, dka - dk), 1)
            kaug_ref[:, dk:] = jnp.where(lane == 0, 1.0, 0.0).astype(BF16)
        else:
            lane = lax.broadcasted_iota(jnp.int32, (seq, dk), 1)
            kaug_ref[...] = jnp.where(lane == aug_lane, 1.0, kf).astype(BF16)
        vt_ref[:dv, :] = v_ref[...].astype(F32).T.astype(BF16)
        vt_ref[dv:, :] = jnp.ones((vt_ref.shape[0] - dv, seq), BF16)

    streams = [(r, g) for r in range(tq // sq) for g in range(groups)]

    def q_aug(r, g, shift_col):
        q = q_ref[r * sq:(r + 1) * sq, g * dk:(g + 1) * dk]
        neg = -shift_col
        if append_tile:
            lane = lax.broadcasted_iota(jnp.int32, (sq, dka - dk), 1)
            tile = jnp.where(lane == 0, neg, 0.0).astype(BF16)
            return jnp.concatenate([q, tile], axis=1)
        lane = lax.broadcasted_iota(jnp.int32, (sq, dk), 1)
        return jnp.where(lane == aug_lane, neg, q.astype(F32)).astype(BF16)

    def run(qas):
        n_chunks = seq // kc
        n_str = len(qas)
        for i, qa in enumerate(qas):
            qa_ref[i] = qa.astype(F32).T.astype(BF16)
        acc_ref[...] = jnp.zeros(acc_ref.shape, F32)

        def scores(i, c, slot):
            start = c * kc if isinstance(c, int) else pl.multiple_of(c * kc, kc)
            s_ref[slot, i] = jnp.dot(kaug_ref[pl.ds(start, kc), :], qa_ref[i],
                                     preferred_element_type=F32)

        def accumulate(i, c, slot):
            start = c * kc if isinstance(c, int) else pl.multiple_of(c * kc, kc)
            p = jnp.exp2(s_ref[slot, i]).astype(BF16)
            acc_ref[i] += jnp.dot(vt_ref[:, pl.ds(start, kc)], p, preferred_element_type=F32)

        def half(c_next, slot_next, c_cur, slot_cur):
            for i in range(n_str):
                if c_next is not None:
                    scores(i, c_next, slot_next)
                accumulate(i, c_cur, slot_cur)

        def body(c2, carry):
            c = 2 * c2
            half(c + 1, 1, c, 0)
            half(c + 2, 0, c + 1, 1)
            return carry

        for i in range(n_str):
            scores(i, 0, 0)
        lax.fori_loop(0, n_chunks // 2 - 1, body, 0)
        half(n_chunks - 1, 1, n_chunks - 2, 0)
        half(None, None, n_chunks - 1, 1)
        return tuple((acc_ref[i, :dv, :], acc_ref[i, dv:dv + 1, :]) for i in range(n_str))

    def write_out(results):
        for (r, g), (acc, l) in zip(streams, results):
            o_ref[r * sq:(r + 1) * sq, g * dv:(g + 1) * dv] = (acc * (1.0 / l)).T.astype(BF16)

    kmax2 = kmax_ref[0:1, 0:1]
    bound_qas = []
    for r, g in streams:
        qf = q_ref[r * sq:(r + 1) * sq, g * dk:(g + 1) * dk].astype(F32)
        shift = jnp.sqrt(jnp.sum(qf * qf, axis=1, keepdims=True) * kmax2)
        bound_qas.append(q_aug(r, g, shift))
    fast = run(bound_qas)
    write_out(fast)

    l_min = None
    for _, l in fast:
        cur = jnp.min(l)
        l_min = cur if l_min is None else jnp.minimum(l_min, cur)

    @pl.when(jnp.logical_not(l_min >= _L_SAFE))
    def _():
        zero_qas = [q_aug(r, g, jnp.zeros((sq, 1), F32)) for r, g in streams]

        def max_body(c, ms):
            start = c * kc if isinstance(c, int) else pl.multiple_of(c * kc, kc)
            ks = kaug_ref[pl.ds(start, kc), :]
            return tuple(
                jnp.maximum(m, jnp.max(
                    lax.dot_general(qa, ks, (((1,), (1,)), ((), ())),
                                    preferred_element_type=F32), axis=1, keepdims=True))
                for qa, m in zip(zero_qas, ms))

        ms = lax.fori_loop(0, seq // kc, max_body,
                           tuple(jnp.full((sq, 1), -jnp.inf, F32) for _ in streams))
        write_out(run([q_aug(r, g, m) for (r, g), m in zip(streams, ms)]))


def _attention(q_arr, k_arr, v_arr, *, batch, seq, n_kv, groups, dk, dv,
               q_col0, k_col0, v_col0, tq, kc, sq, aug_lane):
    t = batch * seq
    nq = seq // tq
    qb0 = q_col0 // (groups * dk)
    kb0 = k_col0 // dk
    vb0 = v_col0 // dv
    dka = dk + LANES if aug_lane == dk else dk
    n_str = (tq // sq) * groups
    assert aug_lane <= dk and tq % sq == 0 and seq % (2 * kc) == 0 and seq // kc >= 4
    dvx = dv + BF16_ROWS
    scratch_bytes = (_nbytes((seq, dka), BF16) + _nbytes((dvx, seq), BF16)
                     + _nbytes((n_str, dka, sq), BF16) + _nbytes((2, n_str, kc, sq), F32)
                     + _nbytes((n_str, dvx, sq), F32))
    return pl.pallas_call(
        functools.partial(_attn_kernel, groups=groups, dk=dk, dv=dv, kc=kc, sq=sq,
                          aug_lane=aug_lane),
        grid=(batch, n_kv, nq),
        in_specs=[
            pl.BlockSpec((tq, groups * dk), lambda b, h, i: (b * nq + i, qb0 + h)),
            pl.BlockSpec((seq, dk), lambda b, h, i: (b, kb0 + h)),
            pl.BlockSpec((seq, dv), lambda b, h, i: (b, vb0 + h)),
        ],
        out_specs=pl.BlockSpec((tq, groups * dv), lambda b, h, i: (b * nq + i, h)),
        out_shape=jax.ShapeDtypeStruct((t, n_kv * groups * dv), BF16),
        scratch_shapes=[pltpu.VMEM((seq, dka), BF16), pltpu.VMEM((dvx, seq), BF16),
                        pltpu.VMEM((8, LANES), F32), pltpu.VMEM((n_str, dka, sq), BF16),
                        pltpu.VMEM((2, n_str, kc, sq), F32), pltpu.VMEM((n_str, dvx, sq), F32)],
        compiler_params=pltpu.CompilerParams(
            dimension_semantics=("parallel", "parallel", "arbitrary"),
            vmem_limit_bytes=_vmem_limit(
                _nbytes((tq, groups * dk), BF16), _nbytes((seq, dk), BF16),
                _nbytes((seq, dv), BF16), _nbytes((tq, groups * dv), BF16),
                scratch=scratch_bytes,
                temps=2 * _nbytes((seq, dk), F32) + 2 * _nbytes((seq, dv), F32)
                + 4 * _nbytes((kc, sq), F32)),
        ),
        name="attention",
    )(q_arr, k_arr, v_arr)


def _merge_kernel(u_ref, oa_ref, ob_ref, wa_hbm, wb_hbm, wt_hbm, o_ref,
                  wabf_ref, wbbf_ref, wgbf_ref, stage_ref, sem_ref, *, tn, gate_row0):
    @pl.when(pl.program_id(0) == 0)
    def _():
        _stage_bf16(wa_hbm, wabf_ref, stage_ref, sem_ref)
        _stage_bf16(wb_hbm, wbbf_ref, stage_ref, sem_ref)
        _stage_bf16(wt_hbm, wgbf_ref, stage_ref, sem_ref, row0=gate_row0,
                    n_rows=wgbf_ref.shape[1], transpose=True)

    u = u_ref[...]
    oa = oa_ref[...]
    ob = ob_ref[...]
    d = o_ref.shape[1]
    for jj in range(d // tn):
        cols = slice(jj * tn, (jj + 1) * tn)
        gcols = slice(d + jj * tn, d + (jj + 1) * tn)
        ya = jnp.dot(oa, wabf_ref[:, cols], preferred_element_type=F32)
        ga = jnp.dot(u, wgbf_ref[:, cols], preferred_element_type=F32)
        yb = jnp.dot(ob, wbbf_ref[:, cols], preferred_element_type=F32)
        gb = jnp.dot(u, wgbf_ref[:, gcols], preferred_element_type=F32)
        o_ref[:, cols] = (jax.nn.sigmoid(ga) * ya + jax.nn.sigmoid(gb) * yb).astype(BF16)


def _merge(u, oa, ob, w_a_out, w_b_out, w_in_t, *, tm, tn, gate_row0, stage_rows):
    t, d = u.shape
    ka = oa.shape[1]
    kb = ob.shape[1]
    assert ka % stage_rows == 0 and kb % stage_rows == 0 and (2 * d) % stage_rows == 0
    assert w_a_out.shape[1] == d and w_in_t.shape[1] == d
    return pl.pallas_call(
        functools.partial(_merge_kernel, tn=tn, gate_row0=gate_row0),
        grid=(t // tm,),
        in_specs=[
            pl.BlockSpec((tm, d), lambda i: (i, 0)),
            pl.BlockSpec((tm, ka), lambda i: (i, 0)),
            pl.BlockSpec((tm, kb), lambda i: (i, 0)),
            pl.BlockSpec(memory_space=pl.ANY),
            pl.BlockSpec(memory_space=pl.ANY),
            pl.BlockSpec(memory_space=pl.ANY),
        ],
        out_specs=pl.BlockSpec((tm, d), lambda i: (i, 0)),
        out_shape=jax.ShapeDtypeStruct((t, d), BF16),
        scratch_shapes=[pltpu.VMEM((ka, d), BF16), pltpu.VMEM((kb, d), BF16),
                        pltpu.VMEM((d, 2 * d), BF16), pltpu.VMEM((2, stage_rows, d), F32),
                        pltpu.SemaphoreType.DMA((2,))],
        compiler_params=pltpu.CompilerParams(
            dimension_semantics=("arbitrary",),
            vmem_limit_bytes=_vmem_limit(
                _nbytes((tm, d), BF16), _nbytes((tm, ka), BF16), _nbytes((tm, kb), BF16),
                _nbytes((tm, d), BF16),
                scratch=_nbytes((ka + kb + 2 * d, d), BF16) + _nbytes((2, stage_rows, d), F32),
                temps=8 * _nbytes((tm, tn), F32)),
        ),
        name="merge",
    )(u, oa, ob, w_a_out, w_b_out, w_in_t)


def _resid_mm_kernel(lhs_ref, w_ref, resid_ref, g_ref, h_ref, un_ref, *, tn, nj):
    j = pl.program_id(1)
    val = resid_ref[...] + jnp.dot(lhs_ref[...], w_ref[...], preferred_element_type=F32)
    for jj in range(nj):
        @pl.when(j == jj)
        def _(jj=jj):
            h_ref[:, jj * tn:(jj + 1) * tn] = val

    @pl.when(j == nj - 1)
    def _():
        un_ref[...] = _rms(h_ref[...], g_ref[...]).astype(BF16)


def _resid_mm(lhs, w, resid, gain, *, tm, tn):
    t, k = lhs.shape
    d = w.shape[1]
    nj = d // tn
    assert w.dtype == BF16
    return pl.pallas_call(
        functools.partial(_resid_mm_kernel, tn=tn, nj=nj),
        grid=(t // tm, nj),
        in_specs=[
            pl.BlockSpec((tm, k), lambda i, j: (i, 0)),
            pl.BlockSpec((k, tn), lambda i, j: (0, j)),
            pl.BlockSpec((tm, tn), lambda i, j: (i, j)),
            pl.BlockSpec((1, d), lambda i, j: (0, 0)),
        ],
        out_specs=[
            pl.BlockSpec((tm, d), lambda i, j: (i, 0)),
            pl.BlockSpec((tm, d), lambda i, j: (i, 0)),
        ],
        out_shape=[
            jax.ShapeDtypeStruct((t, d), F32),
            jax.ShapeDtypeStruct((t, d), BF16),
        ],
        compiler_params=pltpu.CompilerParams(
            dimension_semantics=("parallel", "arbitrary"),
            vmem_limit_bytes=_vmem_limit(
                _nbytes((tm, k), BF16), _nbytes((k, tn), BF16), _nbytes((tm, tn), F32),
                _nbytes((tm, d), F32), _nbytes((tm, d), BF16),
                temps=2 * _nbytes((tm, tn), F32) + 2 * _nbytes((tm, d), F32)),
        ),
        name="resid_mm",
    )(lhs, w, resid, gain)


def _resid_rows_kernel(lhs_ref, w_hbm, resid_ref, g_ref, h_ref, un_ref,
                       wbf_ref, stage_ref, sem_ref, *, tn):
    @pl.when(pl.program_id(0) == 0)
    def _():
        _stage_bf16(w_hbm, wbf_ref, stage_ref, sem_ref)

    lhs = lhs_ref[...]
    d = h_ref.shape[1]
    ss = jnp.zeros((h_ref.shape[0], 1), F32)
    for jj in range(d // tn):
        cols = slice(jj * tn, (jj + 1) * tn)
        val = resid_ref[:, cols] + jnp.dot(lhs, wbf_ref[:, cols], preferred_element_type=F32)
        h_ref[:, cols] = val
        ss = ss + jnp.sum(val * val, axis=-1, keepdims=True)
    inv = lax.rsqrt(ss * (1.0 / d) + RMS_EPS)
    un_ref[...] = (h_ref[...] * inv * g_ref[...]).astype(BF16)


def _resid_rows(lhs, w, resid, gain, *, tm, tn, stage_rows):
    t, k = lhs.shape
    d = w.shape[1]
    assert k % stage_rows == 0 and d % tn == 0
    return pl.pallas_call(
        functools.partial(_resid_rows_kernel, tn=tn),
        grid=(t // tm,),
        in_specs=[
            pl.BlockSpec((tm, k), lambda i: (i, 0)),
            pl.BlockSpec(memory_space=pl.ANY),
            pl.BlockSpec((tm, d), lambda i: (i, 0)),
            pl.BlockSpec((1, d), lambda i: (0, 0)),
        ],
        out_specs=[
            pl.BlockSpec((tm, d), lambda i: (i, 0)),
            pl.BlockSpec((tm, d), lambda i: (i, 0)),
        ],
        out_shape=[
            jax.ShapeDtypeStruct((t, d), F32),
            jax.ShapeDtypeStruct((t, d), BF16),
        ],
        scratch_shapes=[pltpu.VMEM((k, d), BF16), pltpu.VMEM((2, stage_rows, d), F32),
                        pltpu.SemaphoreType.DMA((2,))],
        compiler_params=pltpu.CompilerParams(
            dimension_semantics=("arbitrary",),
            vmem_limit_bytes=_vmem_limit(
                _nbytes((tm, k), BF16), 2 * _nbytes((tm, d), F32), _nbytes((tm, d), BF16),
                scratch=_nbytes((k, d), BF16) + _nbytes((2, stage_rows, d), F32),
                temps=4 * _nbytes((tm, tn), F32) + _nbytes((tm, d), F32)),
        ),
        name="resid_rows",
    )(lhs, w, resid, gain)


def _ffn_up_kernel(un_ref, wg_ref, wu_ref, o_ref):
    un = un_ref[...]
    g = jnp.dot(un, wg_ref[...].astype(BF16), preferred_element_type=F32)
    u = jnp.dot(un, wu_ref[...].astype(BF16), preferred_element_type=F32)
    o_ref[...] = (jax.nn.silu(g) * u).astype(BF16)


def _ffn_up(un, w_gate_up, *, tm, tn):
    t, d = un.shape
    d_ff = w_gate_up.shape[1] // 2
    nj = d_ff // tn
    return pl.pallas_call(
        _ffn_up_kernel,
        grid=(t // tm, nj),
        in_specs=[
            pl.BlockSpec((tm, d), lambda i, j: (i, 0)),
            pl.BlockSpec((d, tn), lambda i, j: (0, j)),
            pl.BlockSpec((d, tn), lambda i, j: (0, nj + j)),
        ],
        out_specs=pl.BlockSpec((tm, tn), lambda i, j: (i, j)),
        out_shape=jax.ShapeDtypeStruct((t, d_ff), BF16),
        compiler_params=pltpu.CompilerParams(
            dimension_semantics=("parallel", "arbitrary"),
            vmem_limit_bytes=_vmem_limit(
                _nbytes((tm, d), BF16), 2 * _nbytes((d, tn), F32), _nbytes((tm, tn), BF16),
                temps=2 * _nbytes((d, tn), BF16) + 4 * _nbytes((tm, tn), F32)),
        ),
        name="ffn_up",
    )(un, w_gate_up, w_gate_up)


def _ple_kernel(un_ref, w_hbm, h_ref, p_ref, wp_hbm, gpost_ref, gfin_ref,
                o_ref, wbf_ref, wpbf_ref, stage_ref, sem_ref, *, tn, final_norm):
    @pl.when(pl.program_id(0) == 0)
    def _():
        _stage_bf16(w_hbm, wbf_ref, stage_ref, sem_ref)
        _stage_bf16(wp_hbm, wpbf_ref, stage_ref, sem_ref)

    un = un_ref[...]
    d = o_ref.shape[1]
    e = _rms(jnp.dot(p_ref[...].astype(BF16), wpbf_ref[...], preferred_element_type=F32),
             gpost_ref[...])
    ss = jnp.zeros((o_ref.shape[0], 1), F32)
    for jj in range(d // tn):
        cols = slice(jj * tn, (jj + 1) * tn)
        gate = jax.nn.sigmoid(jnp.dot(un, wbf_ref[:, cols], preferred_element_type=F32))
        val = h_ref[:, cols] + gate * e[:, cols]
        o_ref[:, cols] = val
        ss = ss + jnp.sum(val * val, axis=-1, keepdims=True)
    if final_norm:
        inv = lax.rsqrt(ss * (1.0 / d) + RMS_EPS)
        o_ref[...] = o_ref[...] * inv * gfin_ref[...]


def _ple(un, w_gate, h, p2, w_proj, g_post, g_final, *, tm, tn, final_norm, stage_rows):
    t, d = un.shape
    pd = p2.shape[1]
    assert d % stage_rows == 0 and pd % stage_rows == 0
    return pl.pallas_call(
        functools.partial(_ple_kernel, tn=tn, final_norm=final_norm),
        grid=(t // tm,),
        in_specs=[
            pl.BlockSpec((tm, d), lambda i: (i, 0)),
            pl.BlockSpec(memory_space=pl.ANY),
            pl.BlockSpec((tm, d), lambda i: (i, 0)),
            pl.BlockSpec((tm, pd), lambda i: (i, 0)),
            pl.BlockSpec(memory_space=pl.ANY),
            pl.BlockSpec((1, d), lambda i: (0, 0)),
            pl.BlockSpec((1, d), lambda i: (0, 0)),
        ],
        out_specs=pl.BlockSpec((tm, d), lambda i: (i, 0)),
        out_shape=jax.ShapeDtypeStruct((t, d), F32),
        scratch_shapes=[pltpu.VMEM((d, d), BF16), pltpu.VMEM((pd, d), BF16),
                        pltpu.VMEM((2, stage_rows, d), F32), pltpu.SemaphoreType.DMA((2,))],
        compiler_params=pltpu.CompilerParams(
            dimension_semantics=("arbitrary",),
            vmem_limit_bytes=_vmem_limit(
                _nbytes((tm, d), BF16), 2 * _nbytes((tm, d), F32), _nbytes((tm, pd), F32),
                scratch=_nbytes((d + pd, d), BF16) + _nbytes((2, stage_rows, d), F32),
                temps=2 * _nbytes((tm, d), F32) + 4 * _nbytes((tm, tn), F32)),
        ),
        name="ple_final",
    )(un, w_gate, h, p2, w_proj, g_post, g_final)


def kernel(x, p, g_mix, w_in, q_norm_a, k_norm_a, q_a_norm, w_q_b, kv_a_norm, w_kv_b,
           w_a_out, w_b_out, w_o, g_ffn, w_gate_up, w_down, g_ple, w_ple_gate,
           w_ple_proj, g_ple_post, g_final):
    batch, seq, d = x.shape
    depth = w_in.shape[0]
    t = batch * seq
    q_lora = q_a_norm.shape[-1]
    kv_lora = kv_a_norm.shape[-1]
    assert q_lora == kv_lora
    n_qa = N_HEADS_A * HEAD_DIM_A
    n_kva = N_KV_A * HEAD_DIM_A
    lat0 = n_qa + 2 * n_kva
    kr0 = lat0 + q_lora + kv_lora
    gate0 = kr0 + ROPE_DIM_B

    cos_a, sin_a = _rope_tables(seq, HEAD_DIM_A)
    cos_b, sin_b = _rope_tables(seq, ROPE_DIM_B)

    h = x.reshape(t, d)
    for i in range(depth):
        wi_t = jnp.swapaxes(w_in[i], 0, 1)
        wq_pad = jnp.pad(
            w_q_b[i].reshape(q_lora, N_HEADS_B, QK_DIM_B),
            ((0, 0), (0, 0), (0, QK_PAD_B - QK_DIM_B))).reshape(q_lora, N_HEADS_B * QK_PAD_B)
        lat_gains = jnp.stack([q_a_norm[i], kv_a_norm[i]])[:, None, :]
        w_down_bf = _cast_bf16(w_down[i], tr=512)

        qkv_a, lat, kpe, u = _inproj(
            h, g_mix[i][None], wi_t, q_norm_a[i][None], k_norm_a[i][None], lat_gains,
            cos_a, sin_a, cos_b, sin_b, seq=seq, tm=512, kr_col0=kr0)
        q_b, k_b, v_b = _qkv_b(lat, wq_pad, w_kv_b[i], kpe, cos_b, sin_b, seq=seq, tm=512)

        oa = _attention(qkv_a, qkv_a, qkv_a, batch=batch, seq=seq, n_kv=N_KV_A,
                        groups=N_HEADS_A // N_KV_A, dk=HEAD_DIM_A, dv=HEAD_DIM_A,
                        q_col0=0, k_col0=n_qa, v_col0=n_qa + n_kva, tq=512, kc=512, sq=512,
                        aug_lane=HEAD_DIM_A)
        ob = _attention(q_b, k_b, v_b, batch=batch, seq=seq, n_kv=N_HEADS_B,
                        groups=1, dk=QK_PAD_B, dv=V_DIM_B,
                        q_col0=0, k_col0=0, v_col0=0, tq=2048, kc=512, sq=512,
                        aug_lane=QK_DIM_B)

        merged = _merge(u, oa, ob, w_a_out[i], w_b_out[i], wi_t, tm=512, tn=512,
                        gate_row0=gate0, stage_rows=256)
        h1, un1 = _resid_rows(merged, w_o[i], h, g_ffn[i][None], tm=512, tn=512, stage_rows=256)
        act = _ffn_up(un1, w_gate_up[i], tm=1024, tn=512)
        h2, un2 = _resid_mm(act, w_down_bf, h1, g_ple[i][None], tm=512, tn=512)
        h = _ple(un2, w_ple_gate[i], h2, p[i].reshape(t, -1), w_ple_proj[i],
                 g_ple_post[i][None], g_final[None], tm=512, tn=512,
                 final_norm=(i == depth - 1), stage_rows=256)
    return h.reshape(batch, seq, d)
```

```python
import functools
import math

import numpy as np
import jax
import jax.numpy as jnp
from jax import lax
from jax.experimental import pallas as pl
from jax.experimental.pallas import tpu as pltpu

F32 = jnp.float32
BF16 = jnp.bfloat16

GRID_W = 64
ROPE_THETA = 10000.0
RMS_EPS = 1e-6

N_HEADS_A = 8
N_KV_A = 2
HEAD_DIM_A = 128
N_HEADS_B = 8
NOPE_DIM_B = 128
ROPE_DIM_B = 64
V_DIM_B = 128
QK_DIM_B = NOPE_DIM_B + ROPE_DIM_B
QK_PAD_B = 256

LANES = 128
BF16_ROWS = 16
V7X_VMEM_BYTES = 64 * 1024 * 1024
LOG2E = math.log2(math.e)


def _vmem_limit(*block_bytes, scratch=0, temps=0):
    need = 2 * sum(block_bytes) + scratch + temps + (4 << 20)
    return int(min(need, V7X_VMEM_BYTES - (6 << 20)))


def _nbytes(shape, dtype):
    return math.prod(shape) * jnp.dtype(dtype).itemsize


def _rms(x32, gain_row):
    ms = jnp.mean(x32 * x32, axis=-1, keepdims=True)
    return x32 * lax.rsqrt(ms + RMS_EPS) * gain_row


def _swap_chunks(y, chunk):
    lane = lax.broadcasted_iota(jnp.int32, y.shape, 1)
    first = (lane % (2 * chunk)) < chunk
    return jnp.where(first, pltpu.roll(y, LANES - chunk, 1), pltpu.roll(y, chunk, 1))


def _rope(y, cos, sin_signed, chunk):
    return y * cos + _swap_chunks(y, chunk) * sin_signed


def _rope_tables(seq, rot_dim):
    rows = seq // GRID_W
    row = np.repeat(np.arange(rows, dtype=np.float64), GRID_W)
    col = np.tile(np.arange(GRID_W, dtype=np.float64), rows)
    half = rot_dim // 2
    inv = 1.0 / np.power(ROPE_THETA, np.arange(0, half, 2, dtype=np.float64) / half)
    ar = row[:, None] * inv[None, :]
    ac = col[:, None] * inv[None, :]
    cos = np.concatenate([np.cos(ar), np.cos(ar), np.cos(ac), np.cos(ac)], axis=-1)
    sin = np.concatenate([-np.sin(ar), np.sin(ar), -np.sin(ac), np.sin(ac)], axis=-1)
    pad = LANES - rot_dim
    if pad:
        cos = np.concatenate([cos, np.ones((seq, pad))], axis=-1)
        sin = np.concatenate([sin, np.zeros((seq, pad))], axis=-1)
    return jnp.asarray(cos, F32), jnp.asarray(sin, F32)


def _stage_bf16(w_hbm, dst_ref, stage_ref, sem_ref, *, row0=0, n_rows=None, transpose=False,
                keep_rows_last=None):
    rows = stage_ref.shape[1]
    n_rows = w_hbm.shape[0] if n_rows is None else n_rows
    n_chunks = n_rows // rows

    def chunk_copy(r, slot):
        return pltpu.make_async_copy(w_hbm.at[pl.ds(row0 + r * rows, rows), :],
                                     stage_ref.at[slot], sem_ref.at[slot])

    chunk_copy(0, 0).start()
    for r in range(n_chunks):
        slot = r % 2
        if r + 1 < n_chunks:
            chunk_copy(r + 1, 1 - slot).start()
        chunk_copy(r, slot).wait()
        blk = stage_ref[slot]
        if keep_rows_last is not None and r == n_chunks - 1:
            row = lax.broadcasted_iota(jnp.int32, blk.shape, 0)
            blk = jnp.where(row < keep_rows_last, blk, 0.0)
        if transpose:
            dst_ref[:, r * rows:(r + 1) * rows] = blk.T.astype(BF16)
        else:
            dst_ref[r * rows:(r + 1) * rows, :] = blk.astype(BF16)


def _inproj_kernel(x_ref, g_ref, wt_hbm, wq_hbm, wkv_hbm, qn_ref, kn_ref, lg_ref,
                   cosa_ref, sina_ref, cosb_ref, sinb_ref,
                   qkv_ref, qb_ref, kb_ref, vb_ref, u_ref,
                   wbf_ref, wqbf_ref, wkvbf_ref, stage_ref, sem_ref,
                   *, q_scale, qb_scale, n_q, n_kv, lora, kr_rows):
    hd = HEAD_DIM_A

    @pl.when(pl.program_id(0) == 0)
    def _():
        _stage_bf16(wt_hbm, wbf_ref, stage_ref, sem_ref, n_rows=wbf_ref.shape[1],
                    transpose=True, keep_rows_last=kr_rows)
        _stage_bf16(wq_hbm, wqbf_ref, stage_ref, sem_ref)
        _stage_bf16(wkv_hbm, wkvbf_ref, stage_ref, sem_ref)

    u = _rms(x_ref[...], g_ref[...]).astype(BF16)
    u_ref[...] = u
    cos = cosa_ref[...]
    sin = sina_ref[...]

    def proj(c0, width):
        return jnp.dot(u, wbf_ref[:, c0:c0 + width], preferred_element_type=F32)

    heads_per_dot = 4
    for h0 in range(0, n_q, heads_per_dot):
        z = proj(h0 * hd, heads_per_dot * hd)
        for h in range(heads_per_dot):
            y = _rms(z[:, h * hd:(h + 1) * hd], qn_ref[...])
            qkv_ref[:, (h0 + h) * hd:(h0 + h + 1) * hd] = (
                _rope(y, cos, sin, hd // 4) * q_scale).astype(BF16)

    z = proj(n_q * hd, 2 * n_kv * hd)
    for h in range(n_kv):
        y = _rms(z[:, h * hd:(h + 1) * hd], kn_ref[...])
        qkv_ref[:, (n_q + h) * hd:(n_q + h + 1) * hd] = _rope(y, cos, sin, hd // 4).astype(BF16)
    qkv_ref[:, (n_q + n_kv) * hd:] = z[:, n_kv * hd:].astype(BF16)

    lat0 = (n_q + 2 * n_kv) * hd
    cq = _rms(proj(lat0, lora), lg_ref[0]).astype(BF16)
    ckv = _rms(proj(lat0 + lora, lora), lg_ref[1]).astype(BF16)
    cosb = cosb_ref[...]
    sinb = sinb_ref[...]
    kpe = _rope(proj(lat0 + 2 * lora, LANES), cosb, sinb, ROPE_DIM_B // 4).astype(BF16)
    zq = jnp.dot(cq, wqbf_ref[...], preferred_element_type=F32)
    zkv = jnp.dot(ckv, wkvbf_ref[...], preferred_element_type=F32)
    for h in range(N_HEADS_B):
        c0 = h * QK_PAD_B
        qb_ref[:, c0:c0 + LANES] = (zq[:, c0:c0 + LANES] * qb_scale).astype(BF16)
        pe = _rope(zq[:, c0 + LANES:c0 + 2 * LANES], cosb, sinb, ROPE_DIM_B // 4)
        qb_ref[:, c0 + LANES:c0 + 2 * LANES] = (pe * qb_scale).astype(BF16)
        kb_ref[:, c0:c0 + LANES] = zkv[:, c0:c0 + LANES].astype(BF16)
        kb_ref[:, c0 + LANES:c0 + 2 * LANES] = kpe
        vb_ref[:, h * V_DIM_B:(h + 1) * V_DIM_B] = zkv[:, c0 + LANES:c0 + 2 * LANES].astype(BF16)


def _inproj(x2, g_mix, w_in_t, wq_pad, w_kv_b, q_norm, k_norm, lat_gains,
            cos_a, sin_a, cos_b, sin_b, *, seq, tm, kr_col0):
    t, d = x2.shape
    lora = lat_gains.shape[-1]
    n_qkv = (N_HEADS_A + 2 * N_KV_A) * HEAD_DIM_A
    n_cols = n_qkv + 2 * lora + LANES
    nqb = wq_pad.shape[1]
    nvb = N_HEADS_B * V_DIM_B
    assert kr_col0 == n_qkv + 2 * lora and n_cols % LANES == 0 and N_HEADS_A % 4 == 0
    assert t % tm == 0 and seq % tm == 0 and lora % LANES == 0
    assert wq_pad.shape == (lora, N_HEADS_B * QK_PAD_B) and w_kv_b.shape == (lora, nqb)
    assert nqb == d
    sblocks = seq // tm
    q_scale = (HEAD_DIM_A ** -0.5) * LOG2E
    qb_scale = (QK_DIM_B ** -0.5) * LOG2E
    return pl.pallas_call(
        functools.partial(_inproj_kernel, q_scale=q_scale, qb_scale=qb_scale, n_q=N_HEADS_A,
                          n_kv=N_KV_A, lora=lora, kr_rows=ROPE_DIM_B),
        grid=(t // tm,),
        in_specs=[
            pl.BlockSpec((tm, d), lambda i: (i, 0)),
            pl.BlockSpec((1, d), lambda i: (0, 0)),
            pl.BlockSpec(memory_space=pl.ANY),
            pl.BlockSpec(memory_space=pl.ANY),
            pl.BlockSpec(memory_space=pl.ANY),
            pl.BlockSpec((1, HEAD_DIM_A), lambda i: (0, 0)),
            pl.BlockSpec((1, HEAD_DIM_A), lambda i: (0, 0)),
            pl.BlockSpec((2, 1, lora), lambda i: (0, 0, 0)),
            pl.BlockSpec((tm, LANES), lambda i: (i % sblocks, 0)),
            pl.BlockSpec((tm, LANES), lambda i: (i % sblocks, 0)),
            pl.BlockSpec((tm, LANES), lambda i: (i % sblocks, 0)),
            pl.BlockSpec((tm, LANES), lambda i: (i % sblocks, 0)),
        ],
        out_specs=[
            pl.BlockSpec((tm, n_qkv), lambda i: (i, 0)),
            pl.BlockSpec((tm, nqb), lambda i: (i, 0)),
            pl.BlockSpec((tm, nqb), lambda i: (i, 0)),
            pl.BlockSpec((tm, nvb), lambda i: (i, 0)),
            pl.BlockSpec((tm, d), lambda i: (i, 0)),
        ],
        out_shape=[
            jax.ShapeDtypeStruct((t, n_qkv), BF16),
            jax.ShapeDtypeStruct((t, nqb), BF16),
            jax.ShapeDtypeStruct((t, nqb), BF16),
            jax.ShapeDtypeStruct((t, nvb), BF16),
            jax.ShapeDtypeStruct((t, d), BF16),
        ],
        scratch_shapes=[pltpu.VMEM((d, n_cols), BF16), pltpu.VMEM((lora, nqb), BF16),
                        pltpu.VMEM((lora, nqb), BF16), pltpu.VMEM((2, LANES, d), F32),
                        pltpu.SemaphoreType.DMA((2,))],
        compiler_params=pltpu.CompilerParams(
            dimension_semantics=("arbitrary",),
            vmem_limit_bytes=_vmem_limit(
                _nbytes((tm, d), F32), _nbytes((tm, n_qkv), BF16), 2 * _nbytes((tm, nqb), BF16),
                _nbytes((tm, nvb), BF16), _nbytes((tm, d), BF16), 4 * _nbytes((tm, LANES), F32),
                scratch=_nbytes((d, n_cols), BF16) + 2 * _nbytes((lora, nqb), BF16)
                + _nbytes((2, LANES, d), F32),
                temps=_nbytes((tm, d), F32) + 4 * _nbytes((tm, nqb), F32)),
        ),
        name="inproj",
    )(x2, g_mix, w_in_t, wq_pad, w_kv_b, q_norm, k_norm, lat_gains, cos_a, sin_a, cos_b, sin_b)


_L_SAFE = 2.0 ** -64


def _attn_kernel(q_ref, k_ref, v_ref, o_ref, kaug_ref, vt_ref, kmax_ref,
                 qa_ref, s_ref, acc_ref, *, groups, dk, dv, kc, sq, aug_lane):
    seq = k_ref.shape[0]
    tq = q_ref.shape[0]
    dka = kaug_ref.shape[1]
    append_tile = aug_lane == dk

    @pl.when(pl.program_id(2) == 0)
    def _():
        kf = k_ref[...].astype(F32)
        kmax2 = jnp.max(jnp.sum(kf * kf, axis=1, keepdims=True), axis=0, keepdims=True)
        kmax_ref[...] = jnp.broadcast_to(kmax2, kmax_ref.shape)
        if append_tile:
            kaug_ref[:, :dk] = k_ref[...]
            lane = lax.broadcasted_iota(jnp.int32, (seq, dka - dk), 1)
            kaug_ref[:, dk:] = jnp.where(lane == 0, 1.0, 0.0).astype(BF16)
        else:
            lane = lax.broadcasted_iota(jnp.int32, (seq, dk), 1)
            kaug_ref[...] = jnp.where(lane == aug_lane, 1.0, kf).astype(BF16)
        vt_ref[:dv, :] = v_ref[...].astype(F32).T.astype(BF16)
        vt_ref[dv:, :] = jnp.ones((vt_ref.shape[0] - dv, seq), BF16)

    streams = [(r, g) for r in range(tq // sq) for g in range(groups)]

    def q_aug(r, g, shift_col):
        q = q_ref[r * sq:(r + 1) * sq, g * dk:(g + 1) * dk]
        neg = -shift_col
        if append_tile:
            lane = lax.broadcasted_iota(jnp.int32, (sq, dka - dk), 1)
            tile = jnp.where(lane == 0, neg, 0.0).astype(BF16)
            return jnp.concatenate([q, tile], axis=1)
        lane = lax.broadcasted_iota(jnp.int32, (sq, dk), 1)
        return jnp.where(lane == aug_lane, neg, q.astype(F32)).astype(BF16)

    def run(qas):
        n_chunks = seq // kc
        n_str = len(qas)
        for i, qa in enumerate(qas):
            qa_ref[i] = qa.astype(F32).T.astype(BF16)
        acc_ref[...] = jnp.zeros(acc_ref.shape, F32)

        def scores(i, c, slot):
            start = c * kc if isinstance(c, int) else pl.multiple_of(c * kc, kc)
            s_ref[slot, i] = jnp.dot(kaug_ref[pl.ds(start, kc), :], qa_ref[i],
                                     preferred_element_type=F32)

        def accumulate(i, c, slot):
            start = c * kc if isinstance(c, int) else pl.multiple_of(c * kc, kc)
            p = jnp.exp2(s_ref[slot, i]).astype(BF16)
            acc_ref[i] += jnp.dot(vt_ref[:, pl.ds(start, kc)], p, preferred_element_type=F32)

        def half(c_next, slot_next, c_cur, slot_cur):
            for i in range(n_str):
                if c_next is not None:
                    scores(i, c_next, slot_next)
                accumulate(i, c_cur, slot_cur)

        def body(c2, carry):
            c = 2 * c2
            half(c + 1, 1, c, 0)
            half(c + 2, 0, c + 1, 1)
            return carry

        for i in range(n_str):
            scores(i, 0, 0)
        lax.fori_loop(0, n_chunks // 2 - 1, body, 0)
        half(n_chunks - 1, 1, n_chunks - 2, 0)
        half(None, None, n_chunks - 1, 1)
        return tuple((acc_ref[i, :dv, :], acc_ref[i, dv:dv + 1, :]) for i in range(n_str))

    def write_out(results):
        for (r, g), (acc, l) in zip(streams, results):
            o_ref[r * sq:(r + 1) * sq, g * dv:(g + 1) * dv] = (acc * (1.0 / l)).T.astype(BF16)

    kmax2 = kmax_ref[0:1, 0:1]
    bound_qas = []
    for r, g in streams:
        qf = q_ref[r * sq:(r + 1) * sq, g * dk:(g + 1) * dk].astype(F32)
        shift = jnp.sqrt(jnp.sum(qf * qf, axis=1, keepdims=True) * kmax2)
        bound_qas.append(q_aug(r, g, shift))
    fast = run(bound_qas)
    write_out(fast)

    l_min = None
    for _, l in fast:
        cur = jnp.min(l)
        l_min = cur if l_min is None else jnp.minimum(l_min, cur)

    @pl.when(jnp.logical_not(l_min >= _L_SAFE))
    def _():
        zero_qas = [q_aug(r, g, jnp.zeros((sq, 1), F32)) for r, g in streams]

        def max_body(c, ms):
            start = c * kc if isinstance(c, int) else pl.multiple_of(c * kc, kc)
            ks = kaug_ref[pl.ds(start, kc), :]
            return tuple(
                jnp.maximum(m, jnp.max(
                    lax.dot_general(qa, ks, (((1,), (1,)), ((), ())),
                                    preferred_element_type=F32), axis=1, keepdims=True))
                for qa, m in zip(zero_qas, ms))

        ms = lax.fori_loop(0, seq // kc, max_body,
                           tuple(jnp.full((sq, 1), -jnp.inf, F32) for _ in streams))
        write_out(run([q_aug(r, g, m) for (r, g), m in zip(streams, ms)]))


def _attention(q_arr, k_arr, v_arr, *, batch, seq, n_kv, groups, dk, dv,
               q_col0, k_col0, v_col0, tq, kc, sq, aug_lane):
    t = batch * seq
    nq = seq // tq
    qb0 = q_col0 // (groups * dk)
    kb0 = k_col0 // dk
    vb0 = v_col0 // dv
    dka = dk + LANES if aug_lane == dk else dk
    n_str = (tq // sq) * groups
    assert aug_lane <= dk and tq % sq == 0 and seq % (2 * kc) == 0 and seq // kc >= 4
    dvx = dv + BF16_ROWS
    scratch_bytes = (_nbytes((seq, dka), BF16) + _nbytes((dvx, seq), BF16)
                     + _nbytes((n_str, dka, sq), BF16) + _nbytes((2, n_str, kc, sq), F32)
                     + _nbytes((n_str, dvx, sq), F32))
    return pl.pallas_call(
        functools.partial(_attn_kernel, groups=groups, dk=dk, dv=dv, kc=kc, sq=sq,
                          aug_lane=aug_lane),
        grid=(batch, n_kv, nq),
        in_specs=[
            pl.BlockSpec((tq, groups * dk), lambda b, h, i: (b * nq + i, qb0 + h)),
            pl.BlockSpec((seq, dk), lambda b, h, i: (b, kb0 + h)),
            pl.BlockSpec((seq, dv), lambda b, h, i: (b, vb0 + h)),
        ],
        out_specs=pl.BlockSpec((tq, groups * dv), lambda b, h, i: (b * nq + i, h)),
        out_shape=jax.ShapeDtypeStruct((t, n_kv * groups * dv), BF16),
        scratch_shapes=[pltpu.VMEM((seq, dka), BF16), pltpu.VMEM((dvx, seq), BF16),
                        pltpu.VMEM((8, LANES), F32), pltpu.VMEM((n_str, dka, sq), BF16),
                        pltpu.VMEM((2, n_str, kc, sq), F32), pltpu.VMEM((n_str, dvx, sq), F32)],
        compiler_params=pltpu.CompilerParams(
            dimension_semantics=("parallel", "parallel", "arbitrary"),
            vmem_limit_bytes=_vmem_limit(
                _nbytes((tq, groups * dk), BF16), _nbytes((seq, dk), BF16),
                _nbytes((seq, dv), BF16), _nbytes((tq, groups * dv), BF16),
                scratch=scratch_bytes,
                temps=2 * _nbytes((seq, dk), F32) + 2 * _nbytes((seq, dv), F32)
                + 4 * _nbytes((kc, sq), F32)),
        ),
        name="attention",
    )(q_arr, k_arr, v_arr)


def _merge_kernel(u_ref, oa_ref, ob_ref, wa_hbm, wb_hbm, wt_hbm, o_ref,
                  wabf_ref, wbbf_ref, wgbf_ref, stage_ref, sem_ref, *, tn, gate_row0):
    @pl.when(pl.program_id(0) == 0)
    def _():
        _stage_bf16(wa_hbm, wabf_ref, stage_ref, sem_ref)
        _stage_bf16(wb_hbm, wbbf_ref, stage_ref, sem_ref)
        _stage_bf16(wt_hbm, wgbf_ref, stage_ref, sem_ref, row0=gate_row0,
                    n_rows=wgbf_ref.shape[1], transpose=True)

    u = u_ref[...]
    oa = oa_ref[...]
    ob = ob_ref[...]
    d = o_ref.shape[1]
    for jj in range(d // tn):
        cols = slice(jj * tn, (jj + 1) * tn)
        gcols = slice(d + jj * tn, d + (jj + 1) * tn)
        ya = jnp.dot(oa, wabf_ref[:, cols], preferred_element_type=F32)
        ga = jnp.dot(u, wgbf_ref[:, cols], preferred_element_type=F32)
        yb = jnp.dot(ob, wbbf_ref[:, cols], preferred_element_type=F32)
        gb = jnp.dot(u, wgbf_ref[:, gcols], preferred_element_type=F32)
        o_ref[:, cols] = (jax.nn.sigmoid(ga) * ya + jax.nn.sigmoid(gb) * yb).astype(BF16)


def _merge(u, oa, ob, w_a_out, w_b_out, w_in_t, *, tm, tn, gate_row0, stage_rows):
    t, d = u.shape
    ka = oa.shape[1]
    kb = ob.shape[1]
    assert ka % stage_rows == 0 and kb % stage_rows == 0 and (2 * d) % stage_rows == 0
    assert w_a_out.shape[1] == d and w_in_t.shape[1] == d
    return pl.pallas_call(
        functools.partial(_merge_kernel, tn=tn, gate_row0=gate_row0),
        grid=(t // tm,),
        in_specs=[
            pl.BlockSpec((tm, d), lambda i: (i, 0)),
            pl.BlockSpec((tm, ka), lambda i: (i, 0)),
            pl.BlockSpec((tm, kb), lambda i: (i, 0)),
            pl.BlockSpec(memory_space=pl.ANY),
            pl.BlockSpec(memory_space=pl.ANY),
            pl.BlockSpec(memory_space=pl.ANY),
        ],
        out_specs=pl.BlockSpec((tm, d), lambda i: (i, 0)),
        out_shape=jax.ShapeDtypeStruct((t, d), BF16),
        scratch_shapes=[pltpu.VMEM((ka, d), BF16), pltpu.VMEM((kb, d), BF16),
                        pltpu.VMEM((d, 2 * d), BF16), pltpu.VMEM((2, stage_rows, d), F32),
                        pltpu.SemaphoreType.DMA((2,))],
        compiler_params=pltpu.CompilerParams(
            dimension_semantics=("arbitrary",),
            vmem_limit_bytes=_vmem_limit(
                _nbytes((tm, d), BF16), _nbytes((tm, ka), BF16), _nbytes((tm, kb), BF16),
                _nbytes((tm, d), BF16),
                scratch=_nbytes((ka + kb + 2 * d, d), BF16) + _nbytes((2, stage_rows, d), F32),
                temps=8 * _nbytes((tm, tn), F32)),
        ),
        name="merge",
    )(u, oa, ob, w_a_out, w_b_out, w_in_t)


def _resid_rows_kernel(lhs_ref, w_hbm, resid_ref, g_ref, h_ref, un_ref,
                       wbf_ref, stage_ref, sem_ref, *, tn):
    @pl.when(pl.program_id(0) == 0)
    def _():
        _stage_bf16(w_hbm, wbf_ref, stage_ref, sem_ref)

    lhs = lhs_ref[...]
    d = h_ref.shape[1]
    ss = jnp.zeros((h_ref.shape[0], 1), F32)
    for jj in range(d // tn):
        cols = slice(jj * tn, (jj + 1) * tn)
        val = resid_ref[:, cols] + jnp.dot(lhs, wbf_ref[:, cols], preferred_element_type=F32)
        h_ref[:, cols] = val
        ss = ss + jnp.sum(val * val, axis=-1, keepdims=True)
    inv = lax.rsqrt(ss * (1.0 / d) + RMS_EPS)
    un_ref[...] = (h_ref[...] * inv * g_ref[...]).astype(BF16)


def _resid_rows(lhs, w, resid, gain, *, tm, tn, stage_rows):
    t, k = lhs.shape
    d = w.shape[1]
    assert k % stage_rows == 0 and d % tn == 0
    return pl.pallas_call(
        functools.partial(_resid_rows_kernel, tn=tn),
        grid=(t // tm,),
        in_specs=[
            pl.BlockSpec((tm, k), lambda i: (i, 0)),
            pl.BlockSpec(memory_space=pl.ANY),
            pl.BlockSpec((tm, d), lambda i: (i, 0)),
            pl.BlockSpec((1, d), lambda i: (0, 0)),
        ],
        out_specs=[
            pl.BlockSpec((tm, d), lambda i: (i, 0)),
            pl.BlockSpec((tm, d), lambda i: (i, 0)),
        ],
        out_shape=[
            jax.ShapeDtypeStruct((t, d), F32),
            jax.ShapeDtypeStruct((t, d), BF16),
        ],
        scratch_shapes=[pltpu.VMEM((k, d), BF16), pltpu.VMEM((2, stage_rows, d), F32),
                        pltpu.SemaphoreType.DMA((2,))],
        compiler_params=pltpu.CompilerParams(
            dimension_semantics=("arbitrary",),
            vmem_limit_bytes=_vmem_limit(
                _nbytes((tm, k), BF16), 2 * _nbytes((tm, d), F32), _nbytes((tm, d), BF16),
                scratch=_nbytes((k, d), BF16) + _nbytes((2, stage_rows, d), F32),
                temps=4 * _nbytes((tm, tn), F32) + _nbytes((tm, d), F32)),
        ),
        name="resid_rows",
    )(lhs, w, resid, gain)


def _ffn_up_kernel(un_ref, wg_ref, wu_ref, o_ref):
    un = un_ref[...]
    g = jnp.dot(un, wg_ref[...].astype(BF16), preferred_element_type=F32)
    u = jnp.dot(un, wu_ref[...].astype(BF16), preferred_element_type=F32)
    o_ref[...] = (jax.nn.silu(g) * u).astype(BF16)


def _ffn_up(un, w_gate_up, *, tm, tn):
    t, d = un.shape
    d_ff = w_gate_up.shape[1] // 2
    nj = d_ff // tn
    return pl.pallas_call(
        _ffn_up_kernel,
        grid=(t // tm, nj),
        in_specs=[
            pl.BlockSpec((tm, d), lambda i, j: (i, 0)),
            pl.BlockSpec((d, tn), lambda i, j: (0, j)),
            pl.BlockSpec((d, tn), lambda i, j: (0, nj + j)),
        ],
        out_specs=pl.BlockSpec((tm, tn), lambda i, j: (i, j)),
        out_shape=jax.ShapeDtypeStruct((t, d_ff), BF16),
        compiler_params=pltpu.CompilerParams(
            dimension_semantics=("parallel", "arbitrary"),
            vmem_limit_bytes=_vmem_limit(
                _nbytes((tm, d), BF16), 2 * _nbytes((d, tn), F32), _nbytes((tm, tn), BF16),
                temps=2 * _nbytes((d, tn), BF16) + 4 * _nbytes((tm, tn), F32)),
        ),
        name="ffn_up",
    )(un, w_gate_up, w_gate_up)


def _ple_kernel(un_ref, w_hbm, h_ref, p_ref, wp_hbm, gpost_ref, gfin_ref,
                o_ref, wbf_ref, wpbf_ref, stage_ref, sem_ref, *, tn, final_norm):
    @pl.when(pl.program_id(0) == 0)
    def _():
        _stage_bf16(w_hbm, wbf_ref, stage_ref, sem_ref)
        _stage_bf16(wp_hbm, wpbf_ref, stage_ref, sem_ref)

    un = un_ref[...]
    d = o_ref.shape[1]
    e = _rms(jnp.dot(p_ref[...].astype(BF16), wpbf_ref[...], preferred_element_type=F32),
             gpost_ref[...])
    ss = jnp.zeros((o_ref.shape[0], 1), F32)
    for jj in range(d // tn):
        cols = slice(jj * tn, (jj + 1) * tn)
        gate = jax.nn.sigmoid(jnp.dot(un, wbf_ref[:, cols], preferred_element_type=F32))
        val = h_ref[:, cols] + gate * e[:, cols]
        o_ref[:, cols] = val
        ss = ss + jnp.sum(val * val, axis=-1, keepdims=True)
    if final_norm:
        inv = lax.rsqrt(ss * (1.0 / d) + RMS_EPS)
        o_ref[...] = o_ref[...] * inv * gfin_ref[...]


def _ple(un, w_gate, h, p2, w_proj, g_post, g_final, *, tm, tn, final_norm, stage_rows):
    t, d = un.shape
    pd = p2.shape[1]
    assert d % stage_rows == 0 and pd % stage_rows == 0
    return pl.pallas_call(
        functools.partial(_ple_kernel, tn=tn, final_norm=final_norm),
        grid=(t // tm,),
        in_specs=[
            pl.BlockSpec((tm, d), lambda i: (i, 0)),
            pl.BlockSpec(memory_space=pl.ANY),
            pl.BlockSpec((tm, d), lambda i: (i, 0)),
            pl.BlockSpec((tm, pd), lambda i: (i, 0)),
            pl.BlockSpec(memory_space=pl.ANY),
            pl.BlockSpec((1, d), lambda i: (0, 0)),
            pl.BlockSpec((1, d), lambda i: (0, 0)),
        ],
        out_specs=pl.BlockSpec((tm, d), lambda i: (i, 0)),
        out_shape=jax.ShapeDtypeStruct((t, d), F32),
        scratch_shapes=[pltpu.VMEM((d, d), BF16), pltpu.VMEM((pd, d), BF16),
                        pltpu.VMEM((2, stage_rows, d), F32), pltpu.SemaphoreType.DMA((2,))],
        compiler_params=pltpu.CompilerParams(
            dimension_semantics=("arbitrary",),
            vmem_limit_bytes=_vmem_limit(
                _nbytes((tm, d), BF16), 2 * _nbytes((tm, d), F32), _nbytes((tm, pd), F32),
                scratch=_nbytes((d + pd, d), BF16) + _nbytes((2, stage_rows, d), F32),
                temps=2 * _nbytes((tm, d), F32) + 4 * _nbytes((tm, tn), F32)),
        ),
        name="ple_final",
    )(un, w_gate, h, p2, w_proj, g_post, g_final)


def kernel(x, p, g_mix, w_in, q_norm_a, k_norm_a, q_a_norm, w_q_b, kv_a_norm, w_kv_b,
           w_a_out, w_b_out, w_o, g_ffn, w_gate_up, w_down, g_ple, w_ple_gate,
           w_ple_proj, g_ple_post, g_final):
    batch, seq, d = x.shape
    depth = w_in.shape[0]
    t = batch * seq
    q_lora = q_a_norm.shape[-1]
    kv_lora = kv_a_norm.shape[-1]
    assert q_lora == kv_lora
    n_qa = N_HEADS_A * HEAD_DIM_A
    n_kva = N_KV_A * HEAD_DIM_A
    lat0 = n_qa + 2 * n_kva
    kr0 = lat0 + q_lora + kv_lora
    gate0 = kr0 + ROPE_DIM_B

    cos_a, sin_a = _rope_tables(seq, HEAD_DIM_A)
    cos_b, sin_b = _rope_tables(seq, ROPE_DIM_B)

    h = x.reshape(t, d)
    for i in range(depth):
        wi_t = jnp.swapaxes(w_in[i], 0, 1)
        wq_pad = jnp.pad(
            w_q_b[i].reshape(q_lora, N_HEADS_B, QK_DIM_B),
            ((0, 0), (0, 0), (0, QK_PAD_B - QK_DIM_B))).reshape(q_lora, N_HEADS_B * QK_PAD_B)
        lat_gains = jnp.stack([q_a_norm[i], kv_a_norm[i]])[:, None, :]
        qkv_a, q_b, k_b, v_b, u = _inproj(
            h, g_mix[i][None], wi_t, wq_pad, w_kv_b[i], q_norm_a[i][None], k_norm_a[i][None],
            lat_gains, cos_a, sin_a, cos_b, sin_b, seq=seq, tm=512, kr_col0=kr0)

        oa = _attention(qkv_a, qkv_a, qkv_a, batch=batch, seq=seq, n_kv=N_KV_A,
                        groups=N_HEADS_A // N_KV_A, dk=HEAD_DIM_A, dv=HEAD_DIM_A,
                        q_col0=0, k_col0=n_qa, v_col0=n_qa + n_kva, tq=1024, kc=512, sq=512,
                        aug_lane=HEAD_DIM_A)
        ob = _attention(q_b, k_b, v_b, batch=batch, seq=seq, n_kv=N_HEADS_B,
                        groups=1, dk=QK_PAD_B, dv=V_DIM_B,
                        q_col0=0, k_col0=0, v_col0=0, tq=4096, kc=512, sq=512,
                        aug_lane=QK_DIM_B)

        merged = _merge(u, oa, ob, w_a_out[i], w_b_out[i], wi_t, tm=512, tn=512,
                        gate_row0=gate0, stage_rows=256)
        h1, un1 = _resid_rows(merged, w_o[i], h, g_ffn[i][None], tm=512, tn=512, stage_rows=256)
        act = _ffn_up(un1, w_gate_up[i], tm=1024, tn=512)
        h2, un2 = _resid_rows(act, w_down[i], h1, g_ple[i][None], tm=256, tn=512, stage_rows=256)
        h = _ple(un2, w_ple_gate[i], h2, p[i].reshape(t, -1), w_ple_proj[i],
                 g_ple_post[i][None], g_final[None], tm=512, tn=512,
                 final_norm=(i == depth - 1), stage_rows=256)
    return h.reshape(batch, seq, d)
```

```python
import functools
import math

import numpy as np
import jax
import jax.numpy as jnp
from jax import lax
from jax.experimental import pallas as pl
from jax.experimental.pallas import tpu as pltpu

F32 = jnp.float32
BF16 = jnp.bfloat16

GRID_W = 64
ROPE_THETA = 10000.0
RMS_EPS = 1e-6

N_HEADS_A = 8
N_KV_A = 2
HEAD_DIM_A = 128
N_HEADS_B = 8
NOPE_DIM_B = 128
ROPE_DIM_B = 64
V_DIM_B = 128
QK_DIM_B = NOPE_DIM_B + ROPE_DIM_B
QK_PAD_B = 256

LANES = 128
BF16_ROWS = 16
V7X_VMEM_BYTES = 64 * 1024 * 1024
LOG2E = math.log2(math.e)


def _vmem_limit(*block_bytes, scratch=0, temps=0):
    need = 2 * sum(block_bytes) + scratch + temps + (4 << 20)
    return int(min(need, V7X_VMEM_BYTES - (6 << 20)))


def _nbytes(shape, dtype):
    return math.prod(shape) * jnp.dtype(dtype).itemsize


def _rms(x32, gain_row):
    ms = jnp.mean(x32 * x32, axis=-1, keepdims=True)
    return x32 * lax.rsqrt(ms + RMS_EPS) * gain_row


def _swap_chunks(y, chunk):
    lane = lax.broadcasted_iota(jnp.int32, y.shape, 1)
    first = (lane % (2 * chunk)) < chunk
    return jnp.where(first, pltpu.roll(y, LANES - chunk, 1), pltpu.roll(y, chunk, 1))


def _rope(y, cos, sin_signed, chunk):
    return y * cos + _swap_chunks(y, chunk) * sin_signed


def _rope_tables(seq, rot_dim):
    rows = seq // GRID_W
    row = np.repeat(np.arange(rows, dtype=np.float64), GRID_W)
    col = np.tile(np.arange(GRID_W, dtype=np.float64), rows)
    half = rot_dim // 2
    inv = 1.0 / np.power(ROPE_THETA, np.arange(0, half, 2, dtype=np.float64) / half)
    ar = row[:, None] * inv[None, :]
    ac = col[:, None] * inv[None, :]
    cos = np.concatenate([np.cos(ar), np.cos(ar), np.cos(ac), np.cos(ac)], axis=-1)
    sin = np.concatenate([-np.sin(ar), np.sin(ar), -np.sin(ac), np.sin(ac)], axis=-1)
    pad = LANES - rot_dim
    if pad:
        cos = np.concatenate([cos, np.ones((seq, pad))], axis=-1)
        sin = np.concatenate([sin, np.zeros((seq, pad))], axis=-1)
    return jnp.asarray(cos, F32), jnp.asarray(sin, F32)


def _stage_bf16(w_hbm, dst_ref, stage_ref, sem_ref, *, row0=0, n_rows=None, transpose=False,
                keep_rows_last=None):
    rows = stage_ref.shape[1]
    n_rows = w_hbm.shape[0] if n_rows is None else n_rows
    n_chunks = n_rows // rows

    def chunk_copy(r, slot):
        return pltpu.make_async_copy(w_hbm.at[pl.ds(row0 + r * rows, rows), :],
                                     stage_ref.at[slot], sem_ref.at[slot])

    chunk_copy(0, 0).start()
    for r in range(n_chunks):
        slot = r % 2
        if r + 1 < n_chunks:
            chunk_copy(r + 1, 1 - slot).start()
        chunk_copy(r, slot).wait()
        blk = stage_ref[slot]
        if keep_rows_last is not None and r == n_chunks - 1:
            row = lax.broadcasted_iota(jnp.int32, blk.shape, 0)
            blk = jnp.where(row < keep_rows_last, blk, 0.0)
        if transpose:
            dst_ref[:, r * rows:(r + 1) * rows] = blk.T.astype(BF16)
        else:
            dst_ref[r * rows:(r + 1) * rows, :] = blk.astype(BF16)


def _inproj_kernel(x_ref, g_ref, wt_hbm, wq_hbm, wkv_hbm, qn_ref, kn_ref, lg_ref,
                   cosa_ref, sina_ref, cosb_ref, sinb_ref,
                   qkv_ref, qb_ref, kb_ref, vb_ref, u_ref,
                   wbf_ref, wqbf_ref, wkvbf_ref, stage_ref, sem_ref,
                   *, q_scale, qb_scale, n_q, n_kv, lora, kr_rows):
    hd = HEAD_DIM_A

    @pl.when(pl.program_id(0) == 0)
    def _():
        _stage_bf16(wt_hbm, wbf_ref, stage_ref, sem_ref, n_rows=wbf_ref.shape[1],
                    transpose=True, keep_rows_last=kr_rows)
        _stage_bf16(wq_hbm, wqbf_ref, stage_ref, sem_ref)
        _stage_bf16(wkv_hbm, wkvbf_ref, stage_ref, sem_ref)

    u = _rms(x_ref[...], g_ref[...]).astype(BF16)
    u_ref[...] = u
    cos = cosa_ref[...]
    sin = sina_ref[...]

    def proj(c0, width):
        return jnp.dot(u, wbf_ref[:, c0:c0 + width], preferred_element_type=F32)

    heads_per_dot = 4
    for h0 in range(0, n_q, heads_per_dot):
        z = proj(h0 * hd, heads_per_dot * hd)
        for h in range(heads_per_dot):
            y = _rms(z[:, h * hd:(h + 1) * hd], qn_ref[...])
            qkv_ref[:, (h0 + h) * hd:(h0 + h + 1) * hd] = (
                _rope(y, cos, sin, hd // 4) * q_scale).astype(BF16)

    z = proj(n_q * hd, 2 * n_kv * hd)
    for h in range(n_kv):
        y = _rms(z[:, h * hd:(h + 1) * hd], kn_ref[...])
        qkv_ref[:, (n_q + h) * hd:(n_q + h + 1) * hd] = _rope(y, cos, sin, hd // 4).astype(BF16)
    qkv_ref[:, (n_q + n_kv) * hd:] = z[:, n_kv * hd:].astype(BF16)

    lat0 = (n_q + 2 * n_kv) * hd
    cq = _rms(proj(lat0, lora), lg_ref[0]).astype(BF16)
    ckv = _rms(proj(lat0 + lora, lora), lg_ref[1]).astype(BF16)
    cosb = cosb_ref[...]
    sinb = sinb_ref[...]
    kpe = _rope(proj(lat0 + 2 * lora, LANES), cosb, sinb, ROPE_DIM_B // 4).astype(BF16)
    zq = jnp.dot(cq, wqbf_ref[...], preferred_element_type=F32)
    zkv = jnp.dot(ckv, wkvbf_ref[...], preferred_element_type=F32)
    for h in range(N_HEADS_B):
        c0 = h * QK_PAD_B
        qb_ref[:, c0:c0 + LANES] = (zq[:, c0:c0 + LANES] * qb_scale).astype(BF16)
        pe = _rope(zq[:, c0 + LANES:c0 + 2 * LANES], cosb, sinb, ROPE_DIM_B // 4)
        qb_ref[:, c0 + LANES:c0 + 2 * LANES] = (pe * qb_scale).astype(BF16)
        kb_ref[:, c0:c0 + LANES] = zkv[:, c0:c0 + LANES].astype(BF16)
        kb_ref[:, c0 + LANES:c0 + 2 * LANES] = kpe
        vb_ref[:, h * V_DIM_B:(h + 1) * V_DIM_B] = zkv[:, c0 + LANES:c0 + 2 * LANES].astype(BF16)


def _inproj(x2, g_mix, w_in_t, wq_pad, w_kv_b, q_norm, k_norm, lat_gains,
            cos_a, sin_a, cos_b, sin_b, *, seq, tm, kr_col0):
    t, d = x2.shape
    lora = lat_gains.shape[-1]
    n_qkv = (N_HEADS_A + 2 * N_KV_A) * HEAD_DIM_A
    n_cols = n_qkv + 2 * lora + LANES
    nqb = wq_pad.shape[1]
    nvb = N_HEADS_B * V_DIM_B
    assert kr_col0 == n_qkv + 2 * lora and n_cols % LANES == 0 and N_HEADS_A % 4 == 0
    assert t % tm == 0 and seq % tm == 0 and lora % LANES == 0
    assert wq_pad.shape == (lora, N_HEADS_B * QK_PAD_B) and w_kv_b.shape == (lora, nqb)
    assert nqb == d
    sblocks = seq // tm
    q_scale = (HEAD_DIM_A ** -0.5) * LOG2E
    qb_scale = (QK_DIM_B ** -0.5) * LOG2E
    return pl.pallas_call(
        functools.partial(_inproj_kernel, q_scale=q_scale, qb_scale=qb_scale, n_q=N_HEADS_A,
                          n_kv=N_KV_A, lora=lora, kr_rows=ROPE_DIM_B),
        grid=(t // tm,),
        in_specs=[
            pl.BlockSpec((tm, d), lambda i: (i, 0)),
            pl.BlockSpec((1, d), lambda i: (0, 0)),
            pl.BlockSpec(memory_space=pl.ANY),
            pl.BlockSpec(memory_space=pl.ANY),
            pl.BlockSpec(memory_space=pl.ANY),
            pl.BlockSpec((1, HEAD_DIM_A), lambda i: (0, 0)),
            pl.BlockSpec((1, HEAD_DIM_A), lambda i: (0, 0)),
            pl.BlockSpec((2, 1, lora), lambda i: (0, 0, 0)),
            pl.BlockSpec((tm, LANES), lambda i: (i % sblocks, 0)),
            pl.BlockSpec((tm, LANES), lambda i: (i % sblocks, 0)),
            pl.BlockSpec((tm, LANES), lambda i: (i % sblocks, 0)),
            pl.BlockSpec((tm, LANES), lambda i: (i % sblocks, 0)),
        ],
        out_specs=[
            pl.BlockSpec((tm, n_qkv), lambda i: (i, 0)),
            pl.BlockSpec((tm, nqb), lambda i: (i, 0)),
            pl.BlockSpec((tm, nqb), lambda i: (i, 0)),
            pl.BlockSpec((tm, nvb), lambda i: (i, 0)),
            pl.BlockSpec((tm, d), lambda i: (i, 0)),
        ],
        out_shape=[
            jax.ShapeDtypeStruct((t, n_qkv), BF16),
            jax.ShapeDtypeStruct((t, nqb), BF16),
            jax.ShapeDtypeStruct((t, nqb), BF16),
            jax.ShapeDtypeStruct((t, nvb), BF16),
            jax.ShapeDtypeStruct((t, d), BF16),
        ],
        scratch_shapes=[pltpu.VMEM((d, n_cols), BF16), pltpu.VMEM((lora, nqb), BF16),
                        pltpu.VMEM((lora, nqb), BF16), pltpu.VMEM((2, LANES, d), F32),
                        pltpu.SemaphoreType.DMA((2,))],
        compiler_params=pltpu.CompilerParams(
            dimension_semantics=("arbitrary",),
            vmem_limit_bytes=_vmem_limit(
                _nbytes((tm, d), F32), _nbytes((tm, n_qkv), BF16), 2 * _nbytes((tm, nqb), BF16),
                _nbytes((tm, nvb), BF16), _nbytes((tm, d), BF16), 4 * _nbytes((tm, LANES), F32),
                scratch=_nbytes((d, n_cols), BF16) + 2 * _nbytes((lora, nqb), BF16)
                + _nbytes((2, LANES, d), F32),
                temps=_nbytes((tm, d), F32) + 4 * _nbytes((tm, nqb), F32)),
        ),
        name="inproj",
    )(x2, g_mix, w_in_t, wq_pad, w_kv_b, q_norm, k_norm, lat_gains, cos_a, sin_a, cos_b, sin_b)


_L_SAFE = 2.0 ** -64


def _attn_kernel(q_ref, k_ref, v_ref, o_ref, kaug_ref, vt_ref, kmax_ref,
                 qa_ref, s_ref, acc_ref, *, groups, dk, dv, kc, sq, aug_lane):
    seq = k_ref.shape[0]
    tq = q_ref.shape[0]
    dka = kaug_ref.shape[1]
    append_tile = aug_lane == dk

    @pl.when(pl.program_id(2) == 0)
    def _():
        kf = k_ref[...].astype(F32)
        kmax2 = jnp.max(jnp.sum(kf * kf, axis=1, keepdims=True), axis=0, keepdims=True)
        kmax_ref[...] = jnp.broadcast_to(kmax2, kmax_ref.shape)
        if append_tile:
            kaug_ref[:, :dk] = k_ref[...]
            lane = lax.broadcasted_iota(jnp.int32, (seq, dka - dk), 1)
            kaug_ref[:, dk:] = jnp.where(lane == 0, 1.0, 0.0).astype(BF16)
        else:
            lane = lax.broadcasted_iota(jnp.int32, (seq, dk), 1)
            kaug_ref[...] = jnp.where(lane == aug_lane, 1.0, kf).astype(BF16)
        vt_ref[:dv, :] = v_ref[...].astype(F32).T.astype(BF16)
        vt_ref[dv:, :] = jnp.ones((vt_ref.shape[0] - dv, seq), BF16)

    streams = [(r, g) for r in range(tq // sq) for g in range(groups)]

    def q_aug(r, g, shift_col):
        q = q_ref[r * sq:(r + 1) * sq, g * dk:(g + 1) * dk]
        neg = -shift_col
        if append_tile:
            lane = lax.broadcasted_iota(jnp.int32, (sq, dka - dk), 1)
            tile = jnp.where(lane == 0, neg, 0.0).astype(BF16)
            return jnp.concatenate([q, tile], axis=1)
        lane = lax.broadcasted_iota(jnp.int32, (sq, dk), 1)
        return jnp.where(lane == aug_lane, neg, q.astype(F32)).astype(BF16)

    def run(qas):
        n_chunks = seq // kc
        n_str = len(qas)
        for i, qa in enumerate(qas):
            qa_ref[i] = qa.astype(F32).T.astype(BF16)
        acc_ref[...] = jnp.zeros(acc_ref.shape, F32)

        def scores(i, c):
            start = c * kc if isinstance(c, int) else pl.multiple_of(c * kc, kc)
            s_ref[i % 2] = jnp.dot(kaug_ref[pl.ds(start, kc), :], qa_ref[i],
                                   preferred_element_type=F32)

        def accumulate(i, c):
            start = c * kc if isinstance(c, int) else pl.multiple_of(c * kc, kc)
            p = jnp.exp2(s_ref[i % 2]).astype(BF16)
            acc_ref[i] += jnp.dot(vt_ref[:, pl.ds(start, kc)], p, preferred_element_type=F32)

        def chunk(c, last):
            for i in range(n_str):
                if i + 1 < n_str:
                    scores(i + 1, c)
                elif not last:
                    scores(0, c + 1)
                accumulate(i, c)

        def body(c, carry):
            chunk(c, last=False)
            return carry

        scores(0, 0)
        lax.fori_loop(0, n_chunks - 1, body, 0, unroll=2)
        chunk(n_chunks - 1, last=True)
        return tuple((acc_ref[i, :dv, :], acc_ref[i, dv:dv + 1, :]) for i in range(n_str))

    def write_out(results):
        for (r, g), (acc, l) in zip(streams, results):
            o_ref[r * sq:(r + 1) * sq, g * dv:(g + 1) * dv] = (acc * (1.0 / l)).T.astype(BF16)

    kmax2 = kmax_ref[0:1, 0:1]
    bound_qas = []
    for r, g in streams:
        qf = q_ref[r * sq:(r + 1) * sq, g * dk:(g + 1) * dk].astype(F32)
        shift = jnp.sqrt(jnp.sum(qf * qf, axis=1, keepdims=True) * kmax2)
        bound_qas.append(q_aug(r, g, shift))
    fast = run(bound_qas)
    write_out(fast)

    l_min = None
    for _, l in fast:
        cur = jnp.min(l)
        l_min = cur if l_min is None else jnp.minimum(l_min, cur)

    @pl.when(jnp.logical_not(l_min >= _L_SAFE))
    def _():
        zero_qas = [q_aug(r, g, jnp.zeros((sq, 1), F32)) for r, g in streams]

        def max_body(c, ms):
            start = c * kc if isinstance(c, int) else pl.multiple_of(c * kc, kc)
            ks = kaug_ref[pl.ds(start, kc), :]
            return tuple(
                jnp.maximum(m, jnp.max(
                    lax.dot_general(qa, ks, (((1,), (1,)), ((), ())),
                                    preferred_element_type=F32), axis=1, keepdims=True))
                for qa, m in zip(zero_qas, ms))

        ms = lax.fori_loop(0, seq // kc, max_body,
                           tuple(jnp.full((sq, 1), -jnp.inf, F32) for _ in streams))
        write_out(run([q_aug(r, g, m) for (r, g), m in zip(streams, ms)]))


def _attention(q_arr, k_arr, v_arr, *, batch, seq, n_kv, groups, dk, dv,
               q_col0, k_col0, v_col0, tq, kc, sq, aug_lane):
    t = batch * seq
    nq = seq // tq
    qb0 = q_col0 // (groups * dk)
    kb0 = k_col0 // dk
    vb0 = v_col0 // dv
    dka = dk + LANES if aug_lane == dk else dk
    n_str = (tq // sq) * groups
    assert aug_lane <= dk and tq % sq == 0 and seq % kc == 0 and n_str % 2 == 0
    dvx = dv + BF16_ROWS
    scratch_bytes = (_nbytes((seq, dka), BF16) + _nbytes((dvx, seq), BF16)
                     + _nbytes((n_str, dka, sq), BF16) + _nbytes((2, kc, sq), F32)
                     + _nbytes((n_str, dvx, sq), F32))
    return pl.pallas_call(
        functools.partial(_attn_kernel, groups=groups, dk=dk, dv=dv, kc=kc, sq=sq,
                          aug_lane=aug_lane),
        grid=(batch, n_kv, nq),
        in_specs=[
            pl.BlockSpec((tq, groups * dk), lambda b, h, i: (b * nq + i, qb0 + h)),
            pl.BlockSpec((seq, dk), lambda b, h, i: (b, kb0 + h)),
            pl.BlockSpec((seq, dv), lambda b, h, i: (b, vb0 + h)),
        ],
        out_specs=pl.BlockSpec((tq, groups * dv), lambda b, h, i: (b * nq + i, h)),
        out_shape=jax.ShapeDtypeStruct((t, n_kv * groups * dv), BF16),
        scratch_shapes=[pltpu.VMEM((seq, dka), BF16), pltpu.VMEM((dvx, seq), BF16),
                        pltpu.VMEM((8, LANES), F32), pltpu.VMEM((n_str, dka, sq), BF16),
                        pltpu.VMEM((2, kc, sq), F32), pltpu.VMEM((n_str, dvx, sq), F32)],
        compiler_params=pltpu.CompilerParams(
            dimension_semantics=("parallel", "parallel", "arbitrary"),
            vmem_limit_bytes=_vmem_limit(
                _nbytes((tq, groups * dk), BF16), _nbytes((seq, dk), BF16),
                _nbytes((seq, dv), BF16), _nbytes((tq, groups * dv), BF16),
                scratch=scratch_bytes,
                temps=2 * _nbytes((seq, dk), F32) + 2 * _nbytes((seq, dv), F32)
                + 4 * _nbytes((kc, sq), F32)),
        ),
        name="attention",
    )(q_arr, k_arr, v_arr)


def _merge_kernel(u_ref, oa_ref, ob_ref, wa_hbm, wb_hbm, wt_hbm, o_ref,
                  wabf_ref, wbbf_ref, wgbf_ref, stage_ref, sem_ref, *, tn, gate_row0):
    @pl.when(pl.program_id(0) == 0)
    def _():
        _stage_bf16(wa_hbm, wabf_ref, stage_ref, sem_ref)
        _stage_bf16(wb_hbm, wbbf_ref, stage_ref, sem_ref)
        _stage_bf16(wt_hbm, wgbf_ref, stage_ref, sem_ref, row0=gate_row0,
                    n_rows=wgbf_ref.shape[1], transpose=True)

    u = u_ref[...]
    oa = oa_ref[...]
    ob = ob_ref[...]
    d = o_ref.shape[1]
    for jj in range(d // tn):
        cols = slice(jj * tn, (jj + 1) * tn)
        gcols = slice(d + jj * tn, d + (jj + 1) * tn)
        ya = jnp.dot(oa, wabf_ref[:, cols], preferred_element_type=F32)
        ga = jnp.dot(u, wgbf_ref[:, cols], preferred_element_type=F32)
        yb = jnp.dot(ob, wbbf_ref[:, cols], preferred_element_type=F32)
        gb = jnp.dot(u, wgbf_ref[:, gcols], preferred_element_type=F32)
        o_ref[:, cols] = (jax.nn.sigmoid(ga) * ya + jax.nn.sigmoid(gb) * yb).astype(BF16)


def _merge(u, oa, ob, w_a_out, w_b_out, w_in_t, *, tm, tn, gate_row0, stage_rows):
    t, d = u.shape
    ka = oa.shape[1]
    kb = ob.shape[1]
    assert ka % stage_rows == 0 and kb % stage_rows == 0 and (2 * d) % stage_rows == 0
    assert w_a_out.shape[1] == d and w_in_t.shape[1] == d
    return pl.pallas_call(
        functools.partial(_merge_kernel, tn=tn, gate_row0=gate_row0),
        grid=(t // tm,),
        in_specs=[
            pl.BlockSpec((tm, d), lambda i: (i, 0)),
            pl.BlockSpec((tm, ka), lambda i: (i, 0)),
            pl.BlockSpec((tm, kb), lambda i: (i, 0)),
            pl.BlockSpec(memory_space=pl.ANY),
            pl.BlockSpec(memory_space=pl.ANY),
            pl.BlockSpec(memory_space=pl.ANY),
        ],
        out_specs=pl.BlockSpec((tm, d), lambda i: (i, 0)),
        out_shape=jax.ShapeDtypeStruct((t, d), BF16),
        scratch_shapes=[pltpu.VMEM((ka, d), BF16), pltpu.VMEM((kb, d), BF16),
                        pltpu.VMEM((d, 2 * d), BF16), pltpu.VMEM((2, stage_rows, d), F32),
                        pltpu.SemaphoreType.DMA((2,))],
        compiler_params=pltpu.CompilerParams(
            dimension_semantics=("arbitrary",),
            vmem_limit_bytes=_vmem_limit(
                _nbytes((tm, d), BF16), _nbytes((tm, ka), BF16), _nbytes((tm, kb), BF16),
                _nbytes((tm, d), BF16),
                scratch=_nbytes((ka + kb + 2 * d, d), BF16) + _nbytes((2, stage_rows, d), F32),
                temps=8 * _nbytes((tm, tn), F32)),
        ),
        name="merge",
    )(u, oa, ob, w_a_out, w_b_out, w_in_t)


def _resid_rows_kernel(lhs_ref, w_hbm, resid_ref, g_ref, h_ref, un_ref,
                       wbf_ref, stage_ref, sem_ref, *, tn):
    @pl.when(pl.program_id(0) == 0)
    def _():
        _stage_bf16(w_hbm, wbf_ref, stage_ref, sem_ref)

    lhs = lhs_ref[...]
    d = h_ref.shape[1]
    ss = jnp.zeros((h_ref.shape[0], 1), F32)
    for jj in range(d // tn):
        cols = slice(jj * tn, (jj + 1) * tn)
        val = resid_ref[:, cols] + jnp.dot(lhs, wbf_ref[:, cols], preferred_element_type=F32)
        h_ref[:, cols] = val
        ss = ss + jnp.sum(val * val, axis=-1, keepdims=True)
    inv = lax.rsqrt(ss * (1.0 / d) + RMS_EPS)
    un_ref[...] = (h_ref[...] * inv * g_ref[...]).astype(BF16)


def _resid_rows(lhs, w, resid, gain, *, tm, tn, stage_rows):
    t, k = lhs.shape
    d = w.shape[1]
    assert k % stage_rows == 0 and d % tn == 0
    return pl.pallas_call(
        functools.partial(_resid_rows_kernel, tn=tn),
        grid=(t // tm,),
        in_specs=[
            pl.BlockSpec((tm, k), lambda i: (i, 0)),
            pl.BlockSpec(memory_space=pl.ANY),
            pl.BlockSpec((tm, d), lambda i: (i, 0)),
            pl.BlockSpec((1, d), lambda i: (0, 0)),
        ],
        out_specs=[
            pl.BlockSpec((tm, d), lambda i: (i, 0)),
            pl.BlockSpec((tm, d), lambda i: (i, 0)),
        ],
        out_shape=[
            jax.ShapeDtypeStruct((t, d), F32),
            jax.ShapeDtypeStruct((t, d), BF16),
        ],
        scratch_shapes=[pltpu.VMEM((k, d), BF16), pltpu.VMEM((2, stage_rows, d), F32),
                        pltpu.SemaphoreType.DMA((2,))],
        compiler_params=pltpu.CompilerParams(
            dimension_semantics=("arbitrary",),
            vmem_limit_bytes=_vmem_limit(
                _nbytes((tm, k), BF16), 2 * _nbytes((tm, d), F32), _nbytes((tm, d), BF16),
                scratch=_nbytes((k, d), BF16) + _nbytes((2, stage_rows, d), F32),
                temps=4 * _nbytes((tm, tn), F32) + _nbytes((tm, d), F32)),
        ),
        name="resid_rows",
    )(lhs, w, resid, gain)


def _ffn_up_kernel(un_ref, wg_ref, wu_ref, o_ref):
    un = un_ref[...]
    g = jnp.dot(un, wg_ref[...].astype(BF16), preferred_element_type=F32)
    u = jnp.dot(un, wu_ref[...].astype(BF16), preferred_element_type=F32)
    o_ref[...] = (jax.nn.silu(g) * u).astype(BF16)


def _ffn_up(un, w_gate_up, *, tm, tn):
    t, d = un.shape
    d_ff = w_gate_up.shape[1] // 2
    nj = d_ff // tn
    return pl.pallas_call(
        _ffn_up_kernel,
        grid=(t // tm, nj),
        in_specs=[
            pl.BlockSpec((tm, d), lambda i, j: (i, 0)),
            pl.BlockSpec((d, tn), lambda i, j: (0, j)),
            pl.BlockSpec((d, tn), lambda i, j: (0, nj + j)),
        ],
        out_specs=pl.BlockSpec((tm, tn), lambda i, j: (i, j)),
        out_shape=jax.ShapeDtypeStruct((t, d_ff), BF16),
        compiler_params=pltpu.CompilerParams(
            dimension_semantics=("parallel", "arbitrary"),
            vmem_limit_bytes=_vmem_limit(
                _nbytes((tm, d), BF16), 2 * _nbytes((d, tn), F32), _nbytes((tm, tn), BF16),
                temps=2 * _nbytes((d, tn), BF16) + 4 * _nbytes((tm, tn), F32)),
        ),
        name="ffn_up",
    )(un, w_gate_up, w_gate_up)


def _ple_kernel(un_ref, w_hbm, h_ref, p_ref, wp_hbm, gpost_ref, gfin_ref,
                o_ref, wbf_ref, wpbf_ref, stage_ref, sem_ref, *, tn, final_norm):
    @pl.when(pl.program_id(0) == 0)
    def _():
        _stage_bf16(w_hbm, wbf_ref, stage_ref, sem_ref)
        _stage_bf16(wp_hbm, wpbf_ref, stage_ref, sem_ref)

    un = un_ref[...]
    d = o_ref.shape[1]
    e = _rms(jnp.dot(p_ref[...].astype(BF16), wpbf_ref[...], preferred_element_type=F32),
             gpost_ref[...])
    ss = jnp.zeros((o_ref.shape[0], 1), F32)
    for jj in range(d // tn):
        cols = slice(jj * tn, (jj + 1) * tn)
        gate = jax.nn.sigmoid(jnp.dot(un, wbf_ref[:, cols], preferred_element_type=F32))
        val = h_ref[:, cols] + gate * e[:, cols]
        o_ref[:, cols] = val
        ss = ss + jnp.sum(val * val, axis=-1, keepdims=True)
    if final_norm:
        inv = lax.rsqrt(ss * (1.0 / d) + RMS_EPS)
        o_ref[...] = o_ref[...] * inv * gfin_ref[...]


def _ple(un, w_gate, h, p2, w_proj, g_post, g_final, *, tm, tn, final_norm, stage_rows):
    t, d = un.shape
    pd = p2.shape[1]
    assert d % stage_rows == 0 and pd % stage_rows == 0
    return pl.pallas_call(
        functools.partial(_ple_kernel, tn=tn, final_norm=final_norm),
        grid=(t // tm,),
        in_specs=[
            pl.BlockSpec((tm, d), lambda i: (i, 0)),
            pl.BlockSpec(memory_space=pl.ANY),
            pl.BlockSpec((tm, d), lambda i: (i, 0)),
            pl.BlockSpec((tm, pd), lambda i: (i, 0)),
            pl.BlockSpec(memory_space=pl.ANY),
            pl.BlockSpec((1, d), lambda i: (0, 0)),
            pl.BlockSpec((1, d), lambda i: (0, 0)),
        ],
        out_specs=pl.BlockSpec((tm, d), lambda i: (i, 0)),
        out_shape=jax.ShapeDtypeStruct((t, d), F32),
        scratch_shapes=[pltpu.VMEM((d, d), BF16), pltpu.VMEM((pd, d), BF16),
                        pltpu.VMEM((2, stage_rows, d), F32), pltpu.SemaphoreType.DMA((2,))],
        compiler_params=pltpu.CompilerParams(
            dimension_semantics=("arbitrary",),
            vmem_limit_bytes=_vmem_limit(
                _nbytes((tm, d), BF16), 2 * _nbytes((tm, d), F32), _nbytes((tm, pd), F32),
                scratch=_nbytes((d + pd, d), BF16) + _nbytes((2, stage_rows, d), F32),
                temps=2 * _nbytes((tm, d), F32) + 4 * _nbytes((tm, tn), F32)),
        ),
        name="ple_final",
    )(un, w_gate, h, p2, w_proj, g_post, g_final)


def kernel(x, p, g_mix, w_in, q_norm_a, k_norm_a, q_a_norm, w_q_b, kv_a_norm, w_kv_b,
           w_a_out, w_b_out, w_o, g_ffn, w_gate_up, w_down, g_ple, w_ple_gate,
           w_ple_proj, g_ple_post, g_final):
    batch, seq, d = x.shape
    depth = w_in.shape[0]
    t = batch * seq
    q_lora = q_a_norm.shape[-1]
    kv_lora = kv_a_norm.shape[-1]
    assert q_lora == kv_lora
    n_qa = N_HEADS_A * HEAD_DIM_A
    n_kva = N_KV_A * HEAD_DIM_A
    lat0 = n_qa + 2 * n_kva
    kr0 = lat0 + q_lora + kv_lora
    gate0 = kr0 + ROPE_DIM_B

    cos_a, sin_a = _rope_tables(seq, HEAD_DIM_A)
    cos_b, sin_b = _rope_tables(seq, ROPE_DIM_B)

    h = x.reshape(t, d)
    for i in range(depth):
        wi_t = jnp.swapaxes(w_in[i], 0, 1)
        wq_pad = jnp.pad(
            w_q_b[i].reshape(q_lora, N_HEADS_B, QK_DIM_B),
            ((0, 0), (0, 0), (0, QK_PAD_B - QK_DIM_B))).reshape(q_lora, N_HEADS_B * QK_PAD_B)
        lat_gains = jnp.stack([q_a_norm[i], kv_a_norm[i]])[:, None, :]
        qkv_a, q_b, k_b, v_b, u = _inproj(
            h, g_mix[i][None], wi_t, wq_pad, w_kv_b[i], q_norm_a[i][None], k_norm_a[i][None],
            lat_gains, cos_a, sin_a, cos_b, sin_b, seq=seq, tm=512, kr_col0=kr0)

        oa = _attention(qkv_a, qkv_a, qkv_a, batch=batch, seq=seq, n_kv=N_KV_A,
                        groups=N_HEADS_A // N_KV_A, dk=HEAD_DIM_A, dv=HEAD_DIM_A,
                        q_col0=0, k_col0=n_qa, v_col0=n_qa + n_kva, tq=1024, kc=512, sq=512,
                        aug_lane=HEAD_DIM_A)
        ob = _attention(q_b, k_b, v_b, batch=batch, seq=seq, n_kv=N_HEADS_B,
                        groups=1, dk=QK_PAD_B, dv=V_DIM_B,
                        q_col0=0, k_col0=0, v_col0=0, tq=4096, kc=512, sq=512,
                        aug_lane=QK_DIM_B)

        merged = _merge(u, oa, ob, w_a_out[i], w_b_out[i], wi_t, tm=512, tn=512,
                        gate_row0=gate0, stage_rows=256)
        h1, un1 = _resid_rows(merged, w_o[i], h, g_ffn[i][None], tm=512, tn=512, stage_rows=256)
        act = _ffn_up(un1, w_gate_up[i], tm=1024, tn=512)
        h2, un2 = _resid_rows(act, w_down[i], h1, g_ple[i][None], tm=256, tn=512, stage_rows=256)
        h = _ple(un2, w_ple_gate[i], h2, p[i].reshape(t, -1), w_ple_proj[i],
                 g_ple_post[i][None], g_final[None], tm=512, tn=512,
                 final_norm=(i == depth - 1), stage_rows=256)
    return h.reshape(batch, seq, d)
```

```python
import functools
import math

import numpy as np
import jax
import jax.numpy as jnp
from jax import lax
from jax.experimental import pallas as pl
from jax.experimental.pallas import tpu as pltpu

F32 = jnp.float32
BF16 = jnp.bfloat16

GRID_W = 64
ROPE_THETA = 10000.0
RMS_EPS = 1e-6

N_HEADS_A = 8
N_KV_A = 2
HEAD_DIM_A = 128
N_HEADS_B = 8
NOPE_DIM_B = 128
ROPE_DIM_B = 64
V_DIM_B = 128
QK_DIM_B = NOPE_DIM_B + ROPE_DIM_B
QK_PAD_B = 256

LANES = 128
BF16_ROWS = 16
V7X_VMEM_BYTES = 64 * 1024 * 1024
LOG2E = math.log2(math.e)


def _vmem_limit(*block_bytes, scratch=0, temps=0):
    need = 2 * sum(block_bytes) + scratch + temps + (4 << 20)
    return int(min(need, V7X_VMEM_BYTES - (6 << 20)))


def _nbytes(shape, dtype):
    return math.prod(shape) * jnp.dtype(dtype).itemsize


def _rms(x32, gain_row):
    ms = jnp.mean(x32 * x32, axis=-1, keepdims=True)
    return x32 * lax.rsqrt(ms + RMS_EPS) * gain_row


def _swap_chunks(y, chunk):
    lane = lax.broadcasted_iota(jnp.int32, y.shape, 1)
    first = (lane % (2 * chunk)) < chunk
    return jnp.where(first, pltpu.roll(y, LANES - chunk, 1), pltpu.roll(y, chunk, 1))


def _rope(y, cos, sin_signed, chunk):
    return y * cos + _swap_chunks(y, chunk) * sin_signed


def _rope_tables(seq, rot_dim):
    rows = seq // GRID_W
    row = np.repeat(np.arange(rows, dtype=np.float64), GRID_W)
    col = np.tile(np.arange(GRID_W, dtype=np.float64), rows)
    half = rot_dim // 2
    inv = 1.0 / np.power(ROPE_THETA, np.arange(0, half, 2, dtype=np.float64) / half)
    ar = row[:, None] * inv[None, :]
    ac = col[:, None] * inv[None, :]
    cos = np.concatenate([np.cos(ar), np.cos(ar), np.cos(ac), np.cos(ac)], axis=-1)
    sin = np.concatenate([-np.sin(ar), np.sin(ar), -np.sin(ac), np.sin(ac)], axis=-1)
    pad = LANES - rot_dim
    if pad:
        cos = np.concatenate([cos, np.ones((seq, pad))], axis=-1)
        sin = np.concatenate([sin, np.zeros((seq, pad))], axis=-1)
    return jnp.asarray(cos, F32), jnp.asarray(sin, F32)


def _stage_bf16(w_hbm, dst_ref, stage_ref, sem_ref, *, row0=0, n_rows=None, transpose=False,
                keep_rows_last=None):
    rows = stage_ref.shape[1]
    n_rows = w_hbm.shape[0] if n_rows is None else n_rows
    n_chunks = n_rows // rows

    def chunk_copy(r, slot):
        return pltpu.make_async_copy(w_hbm.at[pl.ds(row0 + r * rows, rows), :],
                                     stage_ref.at[slot], sem_ref.at[slot])

    chunk_copy(0, 0).start()
    for r in range(n_chunks):
        slot = r % 2
        if r + 1 < n_chunks:
            chunk_copy(r + 1, 1 - slot).start()
        chunk_copy(r, slot).wait()
        blk = stage_ref[slot]
        if keep_rows_last is not None and r == n_chunks - 1:
            row = lax.broadcasted_iota(jnp.int32, blk.shape, 0)
            blk = jnp.where(row < keep_rows_last, blk, 0.0)
        if transpose:
            dst_ref[:, r * rows:(r + 1) * rows] = blk.T.astype(BF16)
        else:
            dst_ref[r * rows:(r + 1) * rows, :] = blk.astype(BF16)


def _inproj_kernel(x_ref, g_ref, wt_hbm, wq_hbm, wkv_hbm, qn_ref, kn_ref, lg_ref,
                   cosa_ref, sina_ref, cosb_ref, sinb_ref,
                   qkv_ref, qb_ref, kb_ref, vb_ref, u_ref,
                   wbf_ref, wqbf_ref, wkvbf_ref, stage_ref, sem_ref,
                   *, q_scale, qb_scale, n_q, n_kv, lora, kr_rows):
    hd = HEAD_DIM_A

    @pl.when(pl.program_id(0) == 0)
    def _():
        _stage_bf16(wt_hbm, wbf_ref, stage_ref, sem_ref, n_rows=wbf_ref.shape[1],
                    transpose=True, keep_rows_last=kr_rows)
        _stage_bf16(wq_hbm, wqbf_ref, stage_ref, sem_ref)
        _stage_bf16(wkv_hbm, wkvbf_ref, stage_ref, sem_ref)

    u = _rms(x_ref[...], g_ref[...]).astype(BF16)
    u_ref[...] = u
    cos = cosa_ref[...]
    sin = sina_ref[...]

    def proj(c0, width):
        return jnp.dot(u, wbf_ref[:, c0:c0 + width], preferred_element_type=F32)

    heads_per_dot = 4
    for h0 in range(0, n_q, heads_per_dot):
        z = proj(h0 * hd, heads_per_dot * hd)
        for h in range(heads_per_dot):
            y = _rms(z[:, h * hd:(h + 1) * hd], qn_ref[...])
            qkv_ref[:, (h0 + h) * hd:(h0 + h + 1) * hd] = (
                _rope(y, cos, sin, hd // 4) * q_scale).astype(BF16)

    z = proj(n_q * hd, 2 * n_kv * hd)
    for h in range(n_kv):
        y = _rms(z[:, h * hd:(h + 1) * hd], kn_ref[...])
        qkv_ref[:, (n_q + h) * hd:(n_q + h + 1) * hd] = _rope(y, cos, sin, hd // 4).astype(BF16)
    qkv_ref[:, (n_q + n_kv) * hd:] = z[:, n_kv * hd:].astype(BF16)

    lat0 = (n_q + 2 * n_kv) * hd
    cq = _rms(proj(lat0, lora), lg_ref[0]).astype(BF16)
    ckv = _rms(proj(lat0 + lora, lora), lg_ref[1]).astype(BF16)
    cosb = cosb_ref[...]
    sinb = sinb_ref[...]
    kpe = _rope(proj(lat0 + 2 * lora, LANES), cosb, sinb, ROPE_DIM_B // 4).astype(BF16)
    zq = jnp.dot(cq, wqbf_ref[...], preferred_element_type=F32)
    zkv = jnp.dot(ckv, wkvbf_ref[...], preferred_element_type=F32)
    for h in range(N_HEADS_B):
        c0 = h * QK_PAD_B
        qb_ref[:, c0:c0 + LANES] = (zq[:, c0:c0 + LANES] * qb_scale).astype(BF16)
        pe = _rope(zq[:, c0 + LANES:c0 + 2 * LANES], cosb, sinb, ROPE_DIM_B // 4)
        qb_ref[:, c0 + LANES:c0 + 2 * LANES] = (pe * qb_scale).astype(BF16)
        kb_ref[:, c0:c0 + LANES] = zkv[:, c0:c0 + LANES].astype(BF16)
        kb_ref[:, c0 + LANES:c0 + 2 * LANES] = kpe
        vb_ref[:, h * V_DIM_B:(h + 1) * V_DIM_B] = zkv[:, c0 + LANES:c0 + 2 * LANES].astype(BF16)


def _inproj(x2, g_mix, w_in_t, wq_pad, w_kv_b, q_norm, k_norm, lat_gains,
            cos_a, sin_a, cos_b, sin_b, *, seq, tm, kr_col0):
    t, d = x2.shape
    lora = lat_gains.shape[-1]
    n_qkv = (N_HEADS_A + 2 * N_KV_A) * HEAD_DIM_A
    n_cols = n_qkv + 2 * lora + LANES
    nqb = wq_pad.shape[1]
    nvb = N_HEADS_B * V_DIM_B
    assert kr_col0 == n_qkv + 2 * lora and n_cols % LANES == 0 and N_HEADS_A % 4 == 0
    assert t % tm == 0 and seq % tm == 0 and lora % LANES == 0
    assert wq_pad.shape == (lora, N_HEADS_B * QK_PAD_B) and w_kv_b.shape == (lora, nqb)
    assert nqb == d
    sblocks = seq // tm
    q_scale = (HEAD_DIM_A ** -0.5) * LOG2E
    qb_scale = (QK_DIM_B ** -0.5) * LOG2E
    return pl.pallas_call(
        functools.partial(_inproj_kernel, q_scale=q_scale, qb_scale=qb_scale, n_q=N_HEADS_A,
                          n_kv=N_KV_A, lora=lora, kr_rows=ROPE_DIM_B),
        grid=(t // tm,),
        in_specs=[
            pl.BlockSpec((tm, d), lambda i: (i, 0)),
            pl.BlockSpec((1, d), lambda i: (0, 0)),
            pl.BlockSpec(memory_space=pl.ANY),
            pl.BlockSpec(memory_space=pl.ANY),
            pl.BlockSpec(memory_space=pl.ANY),
            pl.BlockSpec((1, HEAD_DIM_A), lambda i: (0, 0)),
            pl.BlockSpec((1, HEAD_DIM_A), lambda i: (0, 0)),
            pl.BlockSpec((2, 1, lora), lambda i: (0, 0, 0)),
            pl.BlockSpec((tm, LANES), lambda i: (i % sblocks, 0)),
            pl.BlockSpec((tm, LANES), lambda i: (i % sblocks, 0)),
            pl.BlockSpec((tm, LANES), lambda i: (i % sblocks, 0)),
            pl.BlockSpec((tm, LANES), lambda i: (i % sblocks, 0)),
        ],
        out_specs=[
            pl.BlockSpec((tm, n_qkv), lambda i: (i, 0)),
            pl.BlockSpec((tm, nqb), lambda i: (i, 0)),
            pl.BlockSpec((tm, nqb), lambda i: (i, 0)),
            pl.BlockSpec((tm, nvb), lambda i: (i, 0)),
            pl.BlockSpec((tm, d), lambda i: (i, 0)),
        ],
        out_shape=[
            jax.ShapeDtypeStruct((t, n_qkv), BF16),
            jax.ShapeDtypeStruct((t, nqb), BF16),
            jax.ShapeDtypeStruct((t, nqb), BF16),
            jax.ShapeDtypeStruct((t, nvb), BF16),
            jax.ShapeDtypeStruct((t, d), BF16),
        ],
        scratch_shapes=[pltpu.VMEM((d, n_cols), BF16), pltpu.VMEM((lora, nqb), BF16),
                        pltpu.VMEM((lora, nqb), BF16), pltpu.VMEM((2, LANES, d), F32),
                        pltpu.SemaphoreType.DMA((2,))],
        compiler_params=pltpu.CompilerParams(
            dimension_semantics=("arbitrary",),
            vmem_limit_bytes=_vmem_limit(
                _nbytes((tm, d), F32), _nbytes((tm, n_qkv), BF16), 2 * _nbytes((tm, nqb), BF16),
                _nbytes((tm, nvb), BF16), _nbytes((tm, d), BF16), 4 * _nbytes((tm, LANES), F32),
                scratch=_nbytes((d, n_cols), BF16) + 2 * _nbytes((lora, nqb), BF16)
                + _nbytes((2, LANES, d), F32),
                temps=_nbytes((tm, d), F32) + 4 * _nbytes((tm, nqb), F32)),
        ),
        name="inproj",
    )(x2, g_mix, w_in_t, wq_pad, w_kv_b, q_norm, k_norm, lat_gains, cos_a, sin_a, cos_b, sin_b)


_L_SAFE = 2.0 ** -64


def _attn_kernel(q_ref, k_ref, v_ref, qg_ref, kg_ref, o_ref, kaug_ref, vt_ref, kmax_ref,
                 qa_ref, s_ref, acc_ref, *, groups, dk, dv, kc, sq, aug_lane, unit_rms_scale):
    seq = k_ref.shape[0]
    tq = q_ref.shape[0]
    dka = kaug_ref.shape[1]
    append_tile = aug_lane == dk

    @pl.when(pl.program_id(2) == 0)
    def _():
        kf = k_ref[...].astype(F32)
        if unit_rms_scale is None:
            kmax2 = jnp.max(jnp.sum(kf * kf, axis=1, keepdims=True), axis=0, keepdims=True)
            kmax_ref[...] = jnp.broadcast_to(kmax2, kmax_ref.shape)
        if append_tile:
            kaug_ref[:, :dk] = k_ref[...]
            lane = lax.broadcasted_iota(jnp.int32, (seq, dka - dk), 1)
            kaug_ref[:, dk:] = jnp.where(lane == 0, 1.0, 0.0).astype(BF16)
        else:
            lane = lax.broadcasted_iota(jnp.int32, (seq, dk), 1)
            kaug_ref[...] = jnp.where(lane == aug_lane, 1.0, kf).astype(BF16)
        vt_ref[:dv, :] = v_ref[...].astype(F32).T.astype(BF16)
        vt_ref[dv:, :] = jnp.ones((vt_ref.shape[0] - dv, seq), BF16)

    streams = [(r, g) for r in range(tq // sq) for g in range(groups)]

    def q_aug(r, g, shift_col):
        q = q_ref[r * sq:(r + 1) * sq, g * dk:(g + 1) * dk]
        neg = -shift_col
        if append_tile:
            lane = lax.broadcasted_iota(jnp.int32, (sq, dka - dk), 1)
            tile = jnp.where(lane == 0, neg, 0.0).astype(BF16)
            return jnp.concatenate([q, tile], axis=1)
        lane = lax.broadcasted_iota(jnp.int32, (sq, dk), 1)
        return jnp.where(lane == aug_lane, neg, q.astype(F32)).astype(BF16)

    def run(qas):
        n_chunks = seq // kc
        n_str = len(qas)
        for i, qa in enumerate(qas):
            qa_ref[i] = qa.astype(F32).T.astype(BF16)
        acc_ref[...] = jnp.zeros(acc_ref.shape, F32)

        def scores(i, c):
            start = c * kc if isinstance(c, int) else pl.multiple_of(c * kc, kc)
            s_ref[i % 2] = jnp.dot(kaug_ref[pl.ds(start, kc), :], qa_ref[i],
                                   preferred_element_type=F32)

        def accumulate(i, c):
            start = c * kc if isinstance(c, int) else pl.multiple_of(c * kc, kc)
            p = jnp.exp2(s_ref[i % 2]).astype(BF16)
            acc_ref[i] += jnp.dot(vt_ref[:, pl.ds(start, kc)], p, preferred_element_type=F32)

        def chunk(c, last):
            for i in range(n_str):
                if i + 1 < n_str:
                    scores(i + 1, c)
                elif not last:
                    scores(0, c + 1)
                accumulate(i, c)

        def body(c, carry):
            chunk(c, last=False)
            return carry

        scores(0, 0)
        lax.fori_loop(0, n_chunks - 1, body, 0, unroll=2)
        chunk(n_chunks - 1, last=True)
        return tuple((acc_ref[i, :dv, :], acc_ref[i, dv:dv + 1, :]) for i in range(n_str))

    def write_out(results):
        for (r, g), (acc, l) in zip(streams, results):
            o_ref[r * sq:(r + 1) * sq, g * dv:(g + 1) * dv] = (acc * (1.0 / l)).T.astype(BF16)

    bound_qas = []
    if unit_rms_scale is None:
        kmax2 = kmax_ref[0:1, 0:1]
        for r, g in streams:
            qf = q_ref[r * sq:(r + 1) * sq, g * dk:(g + 1) * dk].astype(F32)
            shift = jnp.sqrt(jnp.sum(qf * qf, axis=1, keepdims=True) * kmax2)
            bound_qas.append(q_aug(r, g, shift))
    else:
        bound = (jnp.max(jnp.abs(qg_ref[...]), axis=1, keepdims=True)
                 * jnp.max(jnp.abs(kg_ref[...]), axis=1, keepdims=True) * (dk * unit_rms_scale))
        shift = jnp.broadcast_to(bound, (sq, 1))
        bound_qas = [q_aug(r, g, shift) for r, g in streams]
    fast = run(bound_qas)
    write_out(fast)

    l_min = None
    for _, l in fast:
        cur = jnp.min(l)
        l_min = cur if l_min is None else jnp.minimum(l_min, cur)

    @pl.when(jnp.logical_not(l_min >= _L_SAFE))
    def _():
        zero_qas = [q_aug(r, g, jnp.zeros((sq, 1), F32)) for r, g in streams]

        def max_body(c, ms):
            start = c * kc if isinstance(c, int) else pl.multiple_of(c * kc, kc)
            ks = kaug_ref[pl.ds(start, kc), :]
            return tuple(
                jnp.maximum(m, jnp.max(
                    lax.dot_general(qa, ks, (((1,), (1,)), ((), ())),
                                    preferred_element_type=F32), axis=1, keepdims=True))
                for qa, m in zip(zero_qas, ms))

        ms = lax.fori_loop(0, seq // kc, max_body,
                           tuple(jnp.full((sq, 1), -jnp.inf, F32) for _ in streams))
        write_out(run([q_aug(r, g, m) for (r, g), m in zip(streams, ms)]))


def _attention(q_arr, k_arr, v_arr, *, batch, seq, n_kv, groups, dk, dv,
               q_col0, k_col0, v_col0, tq, kc, sq, aug_lane, q_gain=None, k_gain=None,
               unit_rms_scale=None):
    t = batch * seq
    if unit_rms_scale is None:
        q_gain = k_gain = jnp.ones((1, dk), F32)
    nq = seq // tq
    qb0 = q_col0 // (groups * dk)
    kb0 = k_col0 // dk
    vb0 = v_col0 // dv
    dka = dk + LANES if aug_lane == dk else dk
    n_str = (tq // sq) * groups
    assert aug_lane <= dk and tq % sq == 0 and seq % kc == 0 and n_str % 2 == 0
    dvx = dv + BF16_ROWS
    scratch_bytes = (_nbytes((seq, dka), BF16) + _nbytes((dvx, seq), BF16)
                     + _nbytes((n_str, dka, sq), BF16) + _nbytes((2, kc, sq), F32)
                     + _nbytes((n_str, dvx, sq), F32))
    return pl.pallas_call(
        functools.partial(_attn_kernel, groups=groups, dk=dk, dv=dv, kc=kc, sq=sq,
                          aug_lane=aug_lane, unit_rms_scale=unit_rms_scale),
        grid=(batch, n_kv, nq),
        in_specs=[
            pl.BlockSpec((tq, groups * dk), lambda b, h, i: (b * nq + i, qb0 + h)),
            pl.BlockSpec((seq, dk), lambda b, h, i: (b, kb0 + h)),
            pl.BlockSpec((seq, dv), lambda b, h, i: (b, vb0 + h)),
            pl.BlockSpec((1, dk), lambda b, h, i: (0, 0)),
            pl.BlockSpec((1, dk), lambda b, h, i: (0, 0)),
        ],
        out_specs=pl.BlockSpec((tq, groups * dv), lambda b, h, i: (b * nq + i, h)),
        out_shape=jax.ShapeDtypeStruct((t, n_kv * groups * dv), BF16),
        scratch_shapes=[pltpu.VMEM((seq, dka), BF16), pltpu.VMEM((dvx, seq), BF16),
                        pltpu.VMEM((8, LANES), F32), pltpu.VMEM((n_str, dka, sq), BF16),
                        pltpu.VMEM((2, kc, sq), F32), pltpu.VMEM((n_str, dvx, sq), F32)],
        compiler_params=pltpu.CompilerParams(
            dimension_semantics=("parallel", "parallel", "arbitrary"),
            vmem_limit_bytes=_vmem_limit(
                _nbytes((tq, groups * dk), BF16), _nbytes((seq, dk), BF16),
                _nbytes((seq, dv), BF16), _nbytes((tq, groups * dv), BF16),
                scratch=scratch_bytes,
                temps=2 * _nbytes((seq, dk), F32) + 2 * _nbytes((seq, dv), F32)
                + 4 * _nbytes((kc, sq), F32)),
        ),
        name="attention",
    )(q_arr, k_arr, v_arr, q_gain, k_gain)


def _merge_kernel(u_ref, oa_ref, ob_ref, wa_hbm, wb_hbm, wt_hbm, o_ref,
                  wabf_ref, wbbf_ref, wgbf_ref, stage_ref, sem_ref, *, tn, gate_row0):
    @pl.when(pl.program_id(0) == 0)
    def _():
        _stage_bf16(wa_hbm, wabf_ref, stage_ref, sem_ref)
        _stage_bf16(wb_hbm, wbbf_ref, stage_ref, sem_ref)
        _stage_bf16(wt_hbm, wgbf_ref, stage_ref, sem_ref, row0=gate_row0,
                    n_rows=wgbf_ref.shape[1], transpose=True)

    u = u_ref[...]
    oa = oa_ref[...]
    ob = ob_ref[...]
    d = o_ref.shape[1]
    for jj in range(d // tn):
        cols = slice(jj * tn, (jj + 1) * tn)
        gcols = slice(d + jj * tn, d + (jj + 1) * tn)
        ya = jnp.dot(oa, wabf_ref[:, cols], preferred_element_type=F32)
        ga = jnp.dot(u, wgbf_ref[:, cols], preferred_element_type=F32)
        yb = jnp.dot(ob, wbbf_ref[:, cols], preferred_element_type=F32)
        gb = jnp.dot(u, wgbf_ref[:, gcols], preferred_element_type=F32)
        o_ref[:, cols] = (jax.nn.sigmoid(ga) * ya + jax.nn.sigmoid(gb) * yb).astype(BF16)


def _merge(u, oa, ob, w_a_out, w_b_out, w_in_t, *, tm, tn, gate_row0, stage_rows):
    t, d = u.shape
    ka = oa.shape[1]
    kb = ob.shape[1]
    assert ka % stage_rows == 0 and kb % stage_rows == 0 and (2 * d) % stage_rows == 0
    assert w_a_out.shape[1] == d and w_in_t.shape[1] == d
    return pl.pallas_call(
        functools.partial(_merge_kernel, tn=tn, gate_row0=gate_row0),
        grid=(t // tm,),
        in_specs=[
            pl.BlockSpec((tm, d), lambda i: (i, 0)),
            pl.BlockSpec((tm, ka), lambda i: (i, 0)),
            pl.BlockSpec((tm, kb), lambda i: (i, 0)),
            pl.BlockSpec(memory_space=pl.ANY),
            pl.BlockSpec(memory_space=pl.ANY),
            pl.BlockSpec(memory_space=pl.ANY),
        ],
        out_specs=pl.BlockSpec((tm, d), lambda i: (i, 0)),
        out_shape=jax.ShapeDtypeStruct((t, d), BF16),
        scratch_shapes=[pltpu.VMEM((ka, d), BF16), pltpu.VMEM((kb, d), BF16),
                        pltpu.VMEM((d, 2 * d), BF16), pltpu.VMEM((2, stage_rows, d), F32),
                        pltpu.SemaphoreType.DMA((2,))],
        compiler_params=pltpu.CompilerParams(
            dimension_semantics=("arbitrary",),
            vmem_limit_bytes=_vmem_limit(
                _nbytes((tm, d), BF16), _nbytes((tm, ka), BF16), _nbytes((tm, kb), BF16),
                _nbytes((tm, d), BF16),
                scratch=_nbytes((ka + kb + 2 * d, d), BF16) + _nbytes((2, stage_rows, d), F32),
                temps=8 * _nbytes((tm, tn), F32)),
        ),
        name="merge",
    )(u, oa, ob, w_a_out, w_b_out, w_in_t)


def _resid_rows_kernel(lhs_ref, w_hbm, resid_ref, g_ref, h_ref, un_ref,
                       wbf_ref, stage_ref, sem_ref, *, tn):
    @pl.when(pl.program_id(0) == 0)
    def _():
        _stage_bf16(w_hbm, wbf_ref, stage_ref, sem_ref)

    lhs = lhs_ref[...]
    d = h_ref.shape[1]
    ss = jnp.zeros((h_ref.shape[0], 1), F32)
    for jj in range(d // tn):
        cols = slice(jj * tn, (jj + 1) * tn)
        val = resid_ref[:, cols] + jnp.dot(lhs, wbf_ref[:, cols], preferred_element_type=F32)
        h_ref[:, cols] = val
        ss = ss + jnp.sum(val * val, axis=-1, keepdims=True)
    inv = lax.rsqrt(ss * (1.0 / d) + RMS_EPS)
    un_ref[...] = (h_ref[...] * inv * g_ref[...]).astype(BF16)


def _resid_rows(lhs, w, resid, gain, *, tm, tn, stage_rows):
    t, k = lhs.shape
    d = w.shape[1]
    assert k % stage_rows == 0 and d % tn == 0
    return pl.pallas_call(
        functools.partial(_resid_rows_kernel, tn=tn),
        grid=(t // tm,),
        in_specs=[
            pl.BlockSpec((tm, k), lambda i: (i, 0)),
            pl.BlockSpec(memory_space=pl.ANY),
            pl.BlockSpec((tm, d), lambda i: (i, 0)),
            pl.BlockSpec((1, d), lambda i: (0, 0)),
        ],
        out_specs=[
            pl.BlockSpec((tm, d), lambda i: (i, 0)),
            pl.BlockSpec((tm, d), lambda i: (i, 0)),
        ],
        out_shape=[
            jax.ShapeDtypeStruct((t, d), F32),
            jax.ShapeDtypeStruct((t, d), BF16),
        ],
        scratch_shapes=[pltpu.VMEM((k, d), BF16), pltpu.VMEM((2, stage_rows, d), F32),
                        pltpu.SemaphoreType.DMA((2,))],
        compiler_params=pltpu.CompilerParams(
            dimension_semantics=("arbitrary",),
            vmem_limit_bytes=_vmem_limit(
                _nbytes((tm, k), BF16), 2 * _nbytes((tm, d), F32), _nbytes((tm, d), BF16),
                scratch=_nbytes((k, d), BF16) + _nbytes((2, stage_rows, d), F32),
                temps=4 * _nbytes((tm, tn), F32) + _nbytes((tm, d), F32)),
        ),
        name="resid_rows",
    )(lhs, w, resid, gain)


def _ffn_up_kernel(un_ref, wg_ref, wu_ref, o_ref):
    un = un_ref[...]
    g = jnp.dot(un, wg_ref[...].astype(BF16), preferred_element_type=F32)
    u = jnp.dot(un, wu_ref[...].astype(BF16), preferred_element_type=F32)
    o_ref[...] = (jax.nn.silu(g) * u).astype(BF16)


def _ffn_up(un, w_gate_up, *, tm, tn):
    t, d = un.shape
    d_ff = w_gate_up.shape[1] // 2
    nj = d_ff // tn
    return pl.pallas_call(
        _ffn_up_kernel,
        grid=(t // tm, nj),
        in_specs=[
            pl.BlockSpec((tm, d), lambda i, j: (i, 0)),
            pl.BlockSpec((d, tn), lambda i, j: (0, j)),
            pl.BlockSpec((d, tn), lambda i, j: (0, nj + j)),
        ],
        out_specs=pl.BlockSpec((tm, tn), lambda i, j: (i, j)),
        out_shape=jax.ShapeDtypeStruct((t, d_ff), BF16),
        compiler_params=pltpu.CompilerParams(
            dimension_semantics=("parallel", "arbitrary"),
            vmem_limit_bytes=_vmem_limit(
                _nbytes((tm, d), BF16), 2 * _nbytes((d, tn), F32), _nbytes((tm, tn), BF16),
                temps=2 * _nbytes((d, tn), BF16) + 4 * _nbytes((tm, tn), F32)),
        ),
        name="ffn_up",
    )(un, w_gate_up, w_gate_up)


def _ple_kernel(un_ref, w_hbm, h_ref, p_ref, wp_hbm, gpost_ref, gfin_ref,
                o_ref, wbf_ref, wpbf_ref, stage_ref, sem_ref, *, tn, final_norm):
    @pl.when(pl.program_id(0) == 0)
    def _():
        _stage_bf16(w_hbm, wbf_ref, stage_ref, sem_ref)
        _stage_bf16(wp_hbm, wpbf_ref, stage_ref, sem_ref)

    un = un_ref[...]
    d = o_ref.shape[1]
    e = _rms(jnp.dot(p_ref[...].astype(BF16), wpbf_ref[...], preferred_element_type=F32),
             gpost_ref[...])
    ss = jnp.zeros((o_ref.shape[0], 1), F32)
    for jj in range(d // tn):
        cols = slice(jj * tn, (jj + 1) * tn)
        gate = jax.nn.sigmoid(jnp.dot(un, wbf_ref[:, cols], preferred_element_type=F32))
        val = h_ref[:, cols] + gate * e[:, cols]
        o_ref[:, cols] = val
        ss = ss + jnp.sum(val * val, axis=-1, keepdims=True)
    if final_norm:
        inv = lax.rsqrt(ss * (1.0 / d) + RMS_EPS)
        o_ref[...] = o_ref[...] * inv * gfin_ref[...]


def _ple(un, w_gate, h, p2, w_proj, g_post, g_final, *, tm, tn, final_norm, stage_rows):
    t, d = un.shape
    pd = p2.shape[1]
    assert d % stage_rows == 0 and pd % stage_rows == 0
    return pl.pallas_call(
        functools.partial(_ple_kernel, tn=tn, final_norm=final_norm),
        grid=(t // tm,),
        in_specs=[
            pl.BlockSpec((tm, d), lambda i: (i, 0)),
            pl.BlockSpec(memory_space=pl.ANY),
            pl.BlockSpec((tm, d), lambda i: (i, 0)),
            pl.BlockSpec((tm, pd), lambda i: (i, 0)),
            pl.BlockSpec(memory_space=pl.ANY),
            pl.BlockSpec((1, d), lambda i: (0, 0)),
            pl.BlockSpec((1, d), lambda i: (0, 0)),
        ],
        out_specs=pl.BlockSpec((tm, d), lambda i: (i, 0)),
        out_shape=jax.ShapeDtypeStruct((t, d), F32),
        scratch_shapes=[pltpu.VMEM((d, d), BF16), pltpu.VMEM((pd, d), BF16),
                        pltpu.VMEM((2, stage_rows, d), F32), pltpu.SemaphoreType.DMA((2,))],
        compiler_params=pltpu.CompilerParams(
            dimension_semantics=("arbitrary",),
            vmem_limit_bytes=_vmem_limit(
                _nbytes((tm, d), BF16), 2 * _nbytes((tm, d), F32), _nbytes((tm, pd), F32),
                scratch=_nbytes((d + pd, d), BF16) + _nbytes((2, stage_rows, d), F32),
                temps=2 * _nbytes((tm, d), F32) + 4 * _nbytes((tm, tn), F32)),
        ),
        name="ple_final",
    )(un, w_gate, h, p2, w_proj, g_post, g_final)


def kernel(x, p, g_mix, w_in, q_norm_a, k_norm_a, q_a_norm, w_q_b, kv_a_norm, w_kv_b,
           w_a_out, w_b_out, w_o, g_ffn, w_gate_up, w_down, g_ple, w_ple_gate,
           w_ple_proj, g_ple_post, g_final):
    batch, seq, d = x.shape
    depth = w_in.shape[0]
    t = batch * seq
    q_lora = q_a_norm.shape[-1]
    kv_lora = kv_a_norm.shape[-1]
    assert q_lora == kv_lora
    n_qa = N_HEADS_A * HEAD_DIM_A
    n_kva = N_KV_A * HEAD_DIM_A
    lat0 = n_qa + 2 * n_kva
    kr0 = lat0 + q_lora + kv_lora
    gate0 = kr0 + ROPE_DIM_B

    cos_a, sin_a = _rope_tables(seq, HEAD_DIM_A)
    cos_b, sin_b = _rope_tables(seq, ROPE_DIM_B)

    h = x.reshape(t, d)
    for i in range(depth):
        wi_t = jnp.swapaxes(w_in[i], 0, 1)
        wq_pad = jnp.pad(
            w_q_b[i].reshape(q_lora, N_HEADS_B, QK_DIM_B),
            ((0, 0), (0, 0), (0, QK_PAD_B - QK_DIM_B))).reshape(q_lora, N_HEADS_B * QK_PAD_B)
        lat_gains = jnp.stack([q_a_norm[i], kv_a_norm[i]])[:, None, :]
        qkv_a, q_b, k_b, v_b, u = _inproj(
            h, g_mix[i][None], wi_t, wq_pad, w_kv_b[i], q_norm_a[i][None], k_norm_a[i][None],
            lat_gains, cos_a, sin_a, cos_b, sin_b, seq=seq, tm=512, kr_col0=kr0)

        oa = _attention(qkv_a, qkv_a, qkv_a, batch=batch, seq=seq, n_kv=N_KV_A,
                        groups=N_HEADS_A // N_KV_A, dk=HEAD_DIM_A, dv=HEAD_DIM_A,
                        q_col0=0, k_col0=n_qa, v_col0=n_qa + n_kva, tq=1024, kc=512, sq=512,
                        aug_lane=HEAD_DIM_A, q_gain=q_norm_a[i][None], k_gain=k_norm_a[i][None],
                        unit_rms_scale=(HEAD_DIM_A ** -0.5) * LOG2E)
        ob = _attention(q_b, k_b, v_b, batch=batch, seq=seq, n_kv=N_HEADS_B,
                        groups=1, dk=QK_PAD_B, dv=V_DIM_B,
                        q_col0=0, k_col0=0, v_col0=0, tq=4096, kc=512, sq=512,
                        aug_lane=QK_DIM_B)

        merged = _merge(u, oa, ob, w_a_out[i], w_b_out[i], wi_t, tm=512, tn=512,
                        gate_row0=gate0, stage_rows=256)
        h1, un1 = _resid_rows(merged, w_o[i], h, g_ffn[i][None], tm=512, tn=512, stage_rows=256)
        act = _ffn_up(un1, w_gate_up[i], tm=1024, tn=512)
        h2, un2 = _resid_rows(act, w_down[i], h1, g_ple[i][None], tm=256, tn=512, stage_rows=256)
        h = _ple(un2, w_ple_gate[i], h2, p[i].reshape(t, -1), w_ple_proj[i],
                 g_ple_post[i][None], g_final[None], tm=512, tn=512,
                 final_norm=(i == depth - 1), stage_rows=256)
    return h.reshape(batch, seq, d)
```

```python
import functools
import math

import numpy as np
import jax
import jax.numpy as jnp
from jax import lax
from jax.experimental import pallas as pl
from jax.experimental.pallas import tpu as pltpu

F32 = jnp.float32
BF16 = jnp.bfloat16

GRID_W = 64
ROPE_THETA = 10000.0
RMS_EPS = 1e-6

N_HEADS_A = 8
N_KV_A = 2
HEAD_DIM_A = 128
N_HEADS_B = 8
NOPE_DIM_B = 128
ROPE_DIM_B = 64
V_DIM_B = 128
QK_DIM_B = NOPE_DIM_B + ROPE_DIM_B
QK_PAD_B = 256

LANES = 128
BF16_ROWS = 16
V7X_VMEM_BYTES = 64 * 1024 * 1024
LOG2E = math.log2(math.e)


def _vmem_limit(*block_bytes, scratch=0, temps=0):
    need = 2 * sum(block_bytes) + scratch + temps + (4 << 20)
    return int(min(need, V7X_VMEM_BYTES - (6 << 20)))


def _nbytes(shape, dtype):
    return math.prod(shape) * jnp.dtype(dtype).itemsize


def _rms(x32, gain_row):
    ms = jnp.mean(x32 * x32, axis=-1, keepdims=True)
    return x32 * lax.rsqrt(ms + RMS_EPS) * gain_row


def _swap_chunks(y, chunk):
    lane = lax.broadcasted_iota(jnp.int32, y.shape, 1)
    first = (lane % (2 * chunk)) < chunk
    return jnp.where(first, pltpu.roll(y, LANES - chunk, 1), pltpu.roll(y, chunk, 1))


def _rope(y, cos, sin_signed, chunk):
    return y * cos + _swap_chunks(y, chunk) * sin_signed


def _rope_tables(seq, rot_dim):
    rows = seq // GRID_W
    row = np.repeat(np.arange(rows, dtype=np.float64), GRID_W)
    col = np.tile(np.arange(GRID_W, dtype=np.float64), rows)
    half = rot_dim // 2
    inv = 1.0 / np.power(ROPE_THETA, np.arange(0, half, 2, dtype=np.float64) / half)
    ar = row[:, None] * inv[None, :]
    ac = col[:, None] * inv[None, :]
    cos = np.concatenate([np.cos(ar), np.cos(ar), np.cos(ac), np.cos(ac)], axis=-1)
    sin = np.concatenate([-np.sin(ar), np.sin(ar), -np.sin(ac), np.sin(ac)], axis=-1)
    pad = LANES - rot_dim
    if pad:
        cos = np.concatenate([cos, np.ones((seq, pad))], axis=-1)
        sin = np.concatenate([sin, np.zeros((seq, pad))], axis=-1)
    return jnp.asarray(cos, F32), jnp.asarray(sin, F32)


def _stage_bf16(w_hbm, dst_ref, stage_ref, sem_ref, *, row0=0, n_rows=None, transpose=False,
                keep_rows_last=None):
    rows = stage_ref.shape[1]
    n_rows = w_hbm.shape[0] if n_rows is None else n_rows
    n_chunks = n_rows // rows

    def chunk_copy(r, slot):
        return pltpu.make_async_copy(w_hbm.at[pl.ds(row0 + r * rows, rows), :],
                                     stage_ref.at[slot], sem_ref.at[slot])

    chunk_copy(0, 0).start()
    for r in range(n_chunks):
        slot = r % 2
        if r + 1 < n_chunks:
            chunk_copy(r + 1, 1 - slot).start()
        chunk_copy(r, slot).wait()
        blk = stage_ref[slot]
        if keep_rows_last is not None and r == n_chunks - 1:
            row = lax.broadcasted_iota(jnp.int32, blk.shape, 0)
            blk = jnp.where(row < keep_rows_last, blk, 0.0)
        if transpose:
            dst_ref[:, r * rows:(r + 1) * rows] = blk.T.astype(BF16)
        else:
            dst_ref[r * rows:(r + 1) * rows, :] = blk.astype(BF16)


def _inproj_kernel(x_ref, g_ref, wt_hbm, wq_hbm, wkv_hbm, qn_ref, kn_ref, lg_ref,
                   cosa_ref, sina_ref, cosb_ref, sinb_ref,
                   qkv_ref, qb_ref, kb_ref, vb_ref, u_ref,
                   wbf_ref, wqbf_ref, wkvbf_ref, stage_ref, sem_ref,
                   *, q_scale, qb_scale, n_q, n_kv, lora, kr_rows):
    hd = HEAD_DIM_A

    @pl.when(pl.program_id(0) == 0)
    def _():
        _stage_bf16(wt_hbm, wbf_ref, stage_ref, sem_ref, n_rows=wbf_ref.shape[1],
                    transpose=True, keep_rows_last=kr_rows)
        _stage_bf16(wq_hbm, wqbf_ref, stage_ref, sem_ref)
        _stage_bf16(wkv_hbm, wkvbf_ref, stage_ref, sem_ref)

    u = _rms(x_ref[...], g_ref[...]).astype(BF16)
    u_ref[...] = u
    cos = cosa_ref[...]
    sin = sina_ref[...]

    def proj(c0, width):
        return jnp.dot(u, wbf_ref[:, c0:c0 + width], preferred_element_type=F32)

    heads_per_dot = 4
    for h0 in range(0, n_q, heads_per_dot):
        z = proj(h0 * hd, heads_per_dot * hd)
        for h in range(heads_per_dot):
            y = _rms(z[:, h * hd:(h + 1) * hd], qn_ref[...])
            qkv_ref[:, (h0 + h) * hd:(h0 + h + 1) * hd] = (
                _rope(y, cos, sin, hd // 4) * q_scale).astype(BF16)

    z = proj(n_q * hd, 2 * n_kv * hd)
    for h in range(n_kv):
        y = _rms(z[:, h * hd:(h + 1) * hd], kn_ref[...])
        qkv_ref[:, (n_q + h) * hd:(n_q + h + 1) * hd] = _rope(y, cos, sin, hd // 4).astype(BF16)
    qkv_ref[:, (n_q + n_kv) * hd:] = z[:, n_kv * hd:].astype(BF16)

    lat0 = (n_q + 2 * n_kv) * hd
    cq = _rms(proj(lat0, lora), lg_ref[0]).astype(BF16)
    ckv = _rms(proj(lat0 + lora, lora), lg_ref[1]).astype(BF16)
    cosb = cosb_ref[...]
    sinb = sinb_ref[...]
    kpe = _rope(proj(lat0 + 2 * lora, LANES), cosb, sinb, ROPE_DIM_B // 4).astype(BF16)
    zq = jnp.dot(cq, wqbf_ref[...], preferred_element_type=F32)
    zkv = jnp.dot(ckv, wkvbf_ref[...], preferred_element_type=F32)
    for h in range(N_HEADS_B):
        c0 = h * QK_PAD_B
        qb_ref[:, c0:c0 + LANES] = (zq[:, c0:c0 + LANES] * qb_scale).astype(BF16)
        pe = _rope(zq[:, c0 + LANES:c0 + 2 * LANES], cosb, sinb, ROPE_DIM_B // 4)
        qb_ref[:, c0 + LANES:c0 + 2 * LANES] = (pe * qb_scale).astype(BF16)
        kb_ref[:, c0:c0 + LANES] = zkv[:, c0:c0 + LANES].astype(BF16)
        kb_ref[:, c0 + LANES:c0 + 2 * LANES] = kpe
        vb_ref[:, h * V_DIM_B:(h + 1) * V_DIM_B] = zkv[:, c0 + LANES:c0 + 2 * LANES].astype(BF16)


def _inproj(x2, g_mix, w_in_t, wq_pad, w_kv_b, q_norm, k_norm, lat_gains,
            cos_a, sin_a, cos_b, sin_b, *, seq, tm, kr_col0):
    t, d = x2.shape
    lora = lat_gains.shape[-1]
    n_qkv = (N_HEADS_A + 2 * N_KV_A) * HEAD_DIM_A
    n_cols = n_qkv + 2 * lora + LANES
    nqb = wq_pad.shape[1]
    nvb = N_HEADS_B * V_DIM_B
    assert kr_col0 == n_qkv + 2 * lora and n_cols % LANES == 0 and N_HEADS_A % 4 == 0
    assert t % tm == 0 and seq % tm == 0 and lora % LANES == 0
    assert wq_pad.shape == (lora, N_HEADS_B * QK_PAD_B) and w_kv_b.shape == (lora, nqb)
    assert nqb == d
    sblocks = seq // tm
    q_scale = (HEAD_DIM_A ** -0.5) * LOG2E
    qb_scale = (QK_DIM_B ** -0.5) * LOG2E
    return pl.pallas_call(
        functools.partial(_inproj_kernel, q_scale=q_scale, qb_scale=qb_scale, n_q=N_HEADS_A,
                          n_kv=N_KV_A, lora=lora, kr_rows=ROPE_DIM_B),
        grid=(t // tm,),
        in_specs=[
            pl.BlockSpec((tm, d), lambda i: (i, 0)),
            pl.BlockSpec((1, d), lambda i: (0, 0)),
            pl.BlockSpec(memory_space=pl.ANY),
            pl.BlockSpec(memory_space=pl.ANY),
            pl.BlockSpec(memory_space=pl.ANY),
            pl.BlockSpec((1, HEAD_DIM_A), lambda i: (0, 0)),
            pl.BlockSpec((1, HEAD_DIM_A), lambda i: (0, 0)),
            pl.BlockSpec((2, 1, lora), lambda i: (0, 0, 0)),
            pl.BlockSpec((tm, LANES), lambda i: (i % sblocks, 0)),
            pl.BlockSpec((tm, LANES), lambda i: (i % sblocks, 0)),
            pl.BlockSpec((tm, LANES), lambda i: (i % sblocks, 0)),
            pl.BlockSpec((tm, LANES), lambda i: (i % sblocks, 0)),
        ],
        out_specs=[
            pl.BlockSpec((tm, n_qkv), lambda i: (i, 0)),
            pl.BlockSpec((tm, nqb), lambda i: (i, 0)),
            pl.BlockSpec((tm, nqb), lambda i: (i, 0)),
            pl.BlockSpec((tm, nvb), lambda i: (i, 0)),
            pl.BlockSpec((tm, d), lambda i: (i, 0)),
        ],
        out_shape=[
            jax.ShapeDtypeStruct((t, n_qkv), BF16),
            jax.ShapeDtypeStruct((t, nqb), BF16),
            jax.ShapeDtypeStruct((t, nqb), BF16),
            jax.ShapeDtypeStruct((t, nvb), BF16),
            jax.ShapeDtypeStruct((t, d), BF16),
        ],
        scratch_shapes=[pltpu.VMEM((d, n_cols), BF16), pltpu.VMEM((lora, nqb), BF16),
                        pltpu.VMEM((lora, nqb), BF16), pltpu.VMEM((2, LANES, d), F32),
                        pltpu.SemaphoreType.DMA((2,))],
        compiler_params=pltpu.CompilerParams(
            dimension_semantics=("arbitrary",),
            vmem_limit_bytes=_vmem_limit(
                _nbytes((tm, d), F32), _nbytes((tm, n_qkv), BF16), 2 * _nbytes((tm, nqb), BF16),
                _nbytes((tm, nvb), BF16), _nbytes((tm, d), BF16), 4 * _nbytes((tm, LANES), F32),
                scratch=_nbytes((d, n_cols), BF16) + 2 * _nbytes((lora, nqb), BF16)
                + _nbytes((2, LANES, d), F32),
                temps=_nbytes((tm, d), F32) + 4 * _nbytes((tm, nqb), F32)),
        ),
        name="inproj",
    )(x2, g_mix, w_in_t, wq_pad, w_kv_b, q_norm, k_norm, lat_gains, cos_a, sin_a, cos_b, sin_b)


_L_SAFE = 2.0 ** -64


def _attn_kernel(q_ref, k_ref, v_ref, qg_ref, kg_ref, o_ref, kaug_ref, vt_ref, kmax_ref,
                 qa_ref, s_ref, acc_ref, *, groups, dk, dv, kc, sq, aug_lane, unit_rms_scale):
    seq = k_ref.shape[0]
    tq = q_ref.shape[0]
    dka = kaug_ref.shape[1]
    append_tile = aug_lane == dk

    @pl.when(pl.program_id(2) == 0)
    def _():
        kf = k_ref[...].astype(F32)
        if unit_rms_scale is None:
            kt = kf.T
            kmax2 = jnp.max(jnp.sum(kt * kt, axis=0, keepdims=True), axis=1, keepdims=True)
            kmax_ref[...] = jnp.broadcast_to(kmax2, kmax_ref.shape)
        if append_tile:
            kaug_ref[:, :dk] = k_ref[...]
            lane = lax.broadcasted_iota(jnp.int32, (seq, dka - dk), 1)
            kaug_ref[:, dk:] = jnp.where(lane == 0, 1.0, 0.0).astype(BF16)
        else:
            lane = lax.broadcasted_iota(jnp.int32, (seq, dk), 1)
            kaug_ref[...] = jnp.where(lane == aug_lane, 1.0, kf).astype(BF16)
        vt_ref[:dv, :] = v_ref[...].astype(F32).T.astype(BF16)
        vt_ref[dv:, :] = jnp.ones((vt_ref.shape[0] - dv, seq), BF16)

    streams = [(r, g) for r in range(tq // sq) for g in range(groups)]

    n_str = len(streams)

    def stage_queries(shift_row_fn):
        for i, (r, g) in enumerate(streams):
            qt = q_ref[r * sq:(r + 1) * sq, g * dk:(g + 1) * dk].astype(F32).T
            neg = -shift_row_fn(i, qt)
            if append_tile:
                row = lax.broadcasted_iota(jnp.int32, (dka - dk, sq), 0)
                qa = jnp.concatenate([qt, jnp.where(row == 0, neg, 0.0)], axis=0)
            else:
                row = lax.broadcasted_iota(jnp.int32, (dk, sq), 0)
                qa = jnp.where(row == aug_lane, neg, qt)
            qa_ref[i] = qa.astype(BF16)

    def run():
        n_chunks = seq // kc
        acc_ref[...] = jnp.zeros(acc_ref.shape, F32)

        def scores(i, c):
            start = c * kc if isinstance(c, int) else pl.multiple_of(c * kc, kc)
            s_ref[i % 2] = jnp.dot(kaug_ref[pl.ds(start, kc), :], qa_ref[i],
                                   preferred_element_type=F32)

        def accumulate(i, c):
            start = c * kc if isinstance(c, int) else pl.multiple_of(c * kc, kc)
            p = jnp.exp2(s_ref[i % 2]).astype(BF16)
            acc_ref[i] += jnp.dot(vt_ref[:, pl.ds(start, kc)], p, preferred_element_type=F32)

        def chunk(c, last):
            for i in range(n_str):
                if i + 1 < n_str:
                    scores(i + 1, c)
                elif not last:
                    scores(0, c + 1)
                accumulate(i, c)

        def body(c, carry):
            chunk(c, last=False)
            return carry

        scores(0, 0)
        lax.fori_loop(0, n_chunks - 1, body, 0, unroll=2)
        chunk(n_chunks - 1, last=True)
        return tuple((acc_ref[i, :dv, :], acc_ref[i, dv:dv + 1, :]) for i in range(n_str))

    def write_out(results):
        for (r, g), (acc, l) in zip(streams, results):
            o_ref[r * sq:(r + 1) * sq, g * dv:(g + 1) * dv] = (acc * (1.0 / l)).T.astype(BF16)

    if unit_rms_scale is None:
        kmax2 = kmax_ref[0:1, 0:1]
        stage_queries(
            lambda i, qt: jnp.sqrt(jnp.sum(qt * qt, axis=0, keepdims=True) * kmax2))
    else:
        bound = (jnp.max(jnp.abs(qg_ref[...]), axis=1, keepdims=True)
                 * jnp.max(jnp.abs(kg_ref[...]), axis=1, keepdims=True) * (dk * unit_rms_scale))
        stage_queries(lambda i, qt: jnp.broadcast_to(bound, (1, sq)))
    fast = run()
    write_out(fast)

    l_min = None
    for _, l in fast:
        cur = jnp.min(l)
        l_min = cur if l_min is None else jnp.minimum(l_min, cur)

    @pl.when(jnp.logical_not(l_min >= _L_SAFE))
    def _():
        stage_queries(lambda i, qt: jnp.zeros((1, sq), F32))

        def max_body(c, ms):
            start = pl.multiple_of(c * kc, kc)
            ks = kaug_ref[pl.ds(start, kc), :]
            return tuple(
                jnp.maximum(m, jnp.max(
                    jnp.dot(ks, qa_ref[i], preferred_element_type=F32), axis=0, keepdims=True))
                for i, m in enumerate(ms))

        ms = lax.fori_loop(0, seq // kc, max_body,
                           tuple(jnp.full((1, sq), -jnp.inf, F32) for _ in streams))
        stage_queries(lambda i, qt: ms[i])
        write_out(run())


def _attention(q_arr, k_arr, v_arr, *, batch, seq, n_kv, groups, dk, dv,
               q_col0, k_col0, v_col0, tq, kc, sq, aug_lane, q_gain=None, k_gain=None,
               unit_rms_scale=None):
    t = batch * seq
    if unit_rms_scale is None:
        q_gain = k_gain = jnp.ones((1, dk), F32)
    nq = seq // tq
    qb0 = q_col0 // (groups * dk)
    kb0 = k_col0 // dk
    vb0 = v_col0 // dv
    dka = dk + LANES if aug_lane == dk else dk
    n_str = (tq // sq) * groups
    assert aug_lane <= dk and tq % sq == 0 and seq % kc == 0 and n_str % 2 == 0
    dvx = dv + BF16_ROWS
    scratch_bytes = (_nbytes((seq, dka), BF16) + _nbytes((dvx, seq), BF16)
                     + _nbytes((n_str, dka, sq), BF16) + _nbytes((2, kc, sq), F32)
                     + _nbytes((n_str, dvx, sq), F32))
    return pl.pallas_call(
        functools.partial(_attn_kernel, groups=groups, dk=dk, dv=dv, kc=kc, sq=sq,
                          aug_lane=aug_lane, unit_rms_scale=unit_rms_scale),
        grid=(batch, n_kv, nq),
        in_specs=[
            pl.BlockSpec((tq, groups * dk), lambda b, h, i: (b * nq + i, qb0 + h)),
            pl.BlockSpec((seq, dk), lambda b, h, i: (b, kb0 + h)),
            pl.BlockSpec((seq, dv), lambda b, h, i: (b, vb0 + h)),
            pl.BlockSpec((1, dk), lambda b, h, i: (0, 0)),
            pl.BlockSpec((1, dk), lambda b, h, i: (0, 0)),
        ],
        out_specs=pl.BlockSpec((tq, groups * dv), lambda b, h, i: (b * nq + i, h)),
        out_shape=jax.ShapeDtypeStruct((t, n_kv * groups * dv), BF16),
        scratch_shapes=[pltpu.VMEM((seq, dka), BF16), pltpu.VMEM((dvx, seq), BF16),
                        pltpu.VMEM((8, LANES), F32), pltpu.VMEM((n_str, dka, sq), BF16),
                        pltpu.VMEM((2, kc, sq), F32), pltpu.VMEM((n_str, dvx, sq), F32)],
        compiler_params=pltpu.CompilerParams(
            dimension_semantics=("parallel", "parallel", "arbitrary"),
            vmem_limit_bytes=_vmem_limit(
                _nbytes((tq, groups * dk), BF16), _nbytes((seq, dk), BF16),
                _nbytes((seq, dv), BF16), _nbytes((tq, groups * dv), BF16),
                scratch=scratch_bytes,
                temps=2 * _nbytes((seq, dk), F32) + 2 * _nbytes((seq, dv), F32)
                + 4 * _nbytes((kc, sq), F32)),
        ),
        name="attention",
    )(q_arr, k_arr, v_arr, q_gain, k_gain)


def _merge_kernel(u_ref, oa_ref, ob_ref, wa_hbm, wb_hbm, wt_hbm, o_ref,
                  wabf_ref, wbbf_ref, wgbf_ref, stage_ref, sem_ref, *, tn, gate_row0):
    @pl.when(pl.program_id(0) == 0)
    def _():
        _stage_bf16(wa_hbm, wabf_ref, stage_ref, sem_ref)
        _stage_bf16(wb_hbm, wbbf_ref, stage_ref, sem_ref)
        _stage_bf16(wt_hbm, wgbf_ref, stage_ref, sem_ref, row0=gate_row0,
                    n_rows=wgbf_ref.shape[1], transpose=True)

    u = u_ref[...]
    oa = oa_ref[...]
    ob = ob_ref[...]
    d = o_ref.shape[1]
    for jj in range(d // tn):
        cols = slice(jj * tn, (jj + 1) * tn)
        gcols = slice(d + jj * tn, d + (jj + 1) * tn)
        ya = jnp.dot(oa, wabf_ref[:, cols], preferred_element_type=F32)
        ga = jnp.dot(u, wgbf_ref[:, cols], preferred_element_type=F32)
        yb = jnp.dot(ob, wbbf_ref[:, cols], preferred_element_type=F32)
        gb = jnp.dot(u, wgbf_ref[:, gcols], preferred_element_type=F32)
        o_ref[:, cols] = (jax.nn.sigmoid(ga) * ya + jax.nn.sigmoid(gb) * yb).astype(BF16)


def _merge(u, oa, ob, w_a_out, w_b_out, w_in_t, *, tm, tn, gate_row0, stage_rows):
    t, d = u.shape
    ka = oa.shape[1]
    kb = ob.shape[1]
    assert ka % stage_rows == 0 and kb % stage_rows == 0 and (2 * d) % stage_rows == 0
    assert w_a_out.shape[1] == d and w_in_t.shape[1] == d
    return pl.pallas_call(
        functools.partial(_merge_kernel, tn=tn, gate_row0=gate_row0),
        grid=(t // tm,),
        in_specs=[
            pl.BlockSpec((tm, d), lambda i: (i, 0)),
            pl.BlockSpec((tm, ka), lambda i: (i, 0)),
            pl.BlockSpec((tm, kb), lambda i: (i, 0)),
            pl.BlockSpec(memory_space=pl.ANY),
            pl.BlockSpec(memory_space=pl.ANY),
            pl.BlockSpec(memory_space=pl.ANY),
        ],
        out_specs=pl.BlockSpec((tm, d), lambda i: (i, 0)),
        out_shape=jax.ShapeDtypeStruct((t, d), BF16),
        scratch_shapes=[pltpu.VMEM((ka, d), BF16), pltpu.VMEM((kb, d), BF16),
                        pltpu.VMEM((d, 2 * d), BF16), pltpu.VMEM((2, stage_rows, d), F32),
                        pltpu.SemaphoreType.DMA((2,))],
        compiler_params=pltpu.CompilerParams(
            dimension_semantics=("arbitrary",),
            vmem_limit_bytes=_vmem_limit(
                _nbytes((tm, d), BF16), _nbytes((tm, ka), BF16), _nbytes((tm, kb), BF16),
                _nbytes((tm, d), BF16),
                scratch=_nbytes((ka + kb + 2 * d, d), BF16) + _nbytes((2, stage_rows, d), F32),
                temps=8 * _nbytes((tm, tn), F32)),
        ),
        name="merge",
    )(u, oa, ob, w_a_out, w_b_out, w_in_t)


def _resid_rows_kernel(lhs_ref, w_hbm, resid_ref, g_ref, h_ref, un_ref,
                       wbf_ref, stage_ref, sem_ref, *, tn):
    @pl.when(pl.program_id(0) == 0)
    def _():
        _stage_bf16(w_hbm, wbf_ref, stage_ref, sem_ref)

    lhs = lhs_ref[...]
    d = h_ref.shape[1]
    ss = jnp.zeros((h_ref.shape[0], 1), F32)
    for jj in range(d // tn):
        cols = slice(jj * tn, (jj + 1) * tn)
        val = resid_ref[:, cols] + jnp.dot(lhs, wbf_ref[:, cols], preferred_element_type=F32)
        h_ref[:, cols] = val
        ss = ss + jnp.sum(val * val, axis=-1, keepdims=True)
    inv = lax.rsqrt(ss * (1.0 / d) + RMS_EPS)
    un_ref[...] = (h_ref[...] * inv * g_ref[...]).astype(BF16)


def _resid_rows(lhs, w, resid, gain, *, tm, tn, stage_rows):
    t, k = lhs.shape
    d = w.shape[1]
    assert k % stage_rows == 0 and d % tn == 0
    return pl.pallas_call(
        functools.partial(_resid_rows_kernel, tn=tn),
        grid=(t // tm,),
        in_specs=[
            pl.BlockSpec((tm, k), lambda i: (i, 0)),
            pl.BlockSpec(memory_space=pl.ANY),
            pl.BlockSpec((tm, d), lambda i: (i, 0)),
            pl.BlockSpec((1, d), lambda i: (0, 0)),
        ],
        out_specs=[
            pl.BlockSpec((tm, d), lambda i: (i, 0)),
            pl.BlockSpec((tm, d), lambda i: (i, 0)),
        ],
        out_shape=[
            jax.ShapeDtypeStruct((t, d), F32),
            jax.ShapeDtypeStruct((t, d), BF16),
        ],
        scratch_shapes=[pltpu.VMEM((k, d), BF16), pltpu.VMEM((2, stage_rows, d), F32),
                        pltpu.SemaphoreType.DMA((2,))],
        compiler_params=pltpu.CompilerParams(
            dimension_semantics=("arbitrary",),
            vmem_limit_bytes=_vmem_limit(
                _nbytes((tm, k), BF16), 2 * _nbytes((tm, d), F32), _nbytes((tm, d), BF16),
                scratch=_nbytes((k, d), BF16) + _nbytes((2, stage_rows, d), F32),
                temps=4 * _nbytes((tm, tn), F32) + _nbytes((tm, d), F32)),
        ),
        name="resid_rows",
    )(lhs, w, resid, gain)


def _ffn_up_kernel(un_ref, wg_ref, wu_ref, o_ref):
    un = un_ref[...]
    g = jnp.dot(un, wg_ref[...].astype(BF16), preferred_element_type=F32)
    u = jnp.dot(un, wu_ref[...].astype(BF16), preferred_element_type=F32)
    o_ref[...] = (jax.nn.silu(g) * u).astype(BF16)


def _ffn_up(un, w_gate_up, *, tm, tn):
    t, d = un.shape
    d_ff = w_gate_up.shape[1] // 2
    nj = d_ff // tn
    return pl.pallas_call(
        _ffn_up_kernel,
        grid=(t // tm, nj),
        in_specs=[
            pl.BlockSpec((tm, d), lambda i, j: (i, 0)),
            pl.BlockSpec((d, tn), lambda i, j: (0, j)),
            pl.BlockSpec((d, tn), lambda i, j: (0, nj + j)),
        ],
        out_specs=pl.BlockSpec((tm, tn), lambda i, j: (i, j)),
        out_shape=jax.ShapeDtypeStruct((t, d_ff), BF16),
        compiler_params=pltpu.CompilerParams(
            dimension_semantics=("parallel", "arbitrary"),
            vmem_limit_bytes=_vmem_limit(
                _nbytes((tm, d), BF16), 2 * _nbytes((d, tn), F32), _nbytes((tm, tn), BF16),
                temps=2 * _nbytes((d, tn), BF16) + 4 * _nbytes((tm, tn), F32)),
        ),
        name="ffn_up",
    )(un, w_gate_up, w_gate_up)


def _ple_kernel(un_ref, w_hbm, h_ref, p_ref, wp_hbm, gpost_ref, gfin_ref,
                o_ref, wbf_ref, wpbf_ref, stage_ref, sem_ref, *, tn, final_norm):
    @pl.when(pl.program_id(0) == 0)
    def _():
        _stage_bf16(w_hbm, wbf_ref, stage_ref, sem_ref)
        _stage_bf16(wp_hbm, wpbf_ref, stage_ref, sem_ref)

    un = un_ref[...]
    d = o_ref.shape[1]
    e = _rms(jnp.dot(p_ref[...].astype(BF16), wpbf_ref[...], preferred_element_type=F32),
             gpost_ref[...])
    ss = jnp.zeros((o_ref.shape[0], 1), F32)
    for jj in range(d // tn):
        cols = slice(jj * tn, (jj + 1) * tn)
        gate = jax.nn.sigmoid(jnp.dot(un, wbf_ref[:, cols], preferred_element_type=F32))
        val = h_ref[:, cols] + gate * e[:, cols]
        o_ref[:, cols] = val
        ss = ss + jnp.sum(val * val, axis=-1, keepdims=True)
    if final_norm:
        inv = lax.rsqrt(ss * (1.0 / d) + RMS_EPS)
        o_ref[...] = o_ref[...] * inv * gfin_ref[...]


def _ple(un, w_gate, h, p2, w_proj, g_post, g_final, *, tm, tn, final_norm, stage_rows):
    t, d = un.shape
    pd = p2.shape[1]
    assert d % stage_rows == 0 and pd % stage_rows == 0
    return pl.pallas_call(
        functools.partial(_ple_kernel, tn=tn, final_norm=final_norm),
        grid=(t // tm,),
        in_specs=[
            pl.BlockSpec((tm, d), lambda i: (i, 0)),
            pl.BlockSpec(memory_space=pl.ANY),
            pl.BlockSpec((tm, d), lambda i: (i, 0)),
            pl.BlockSpec((tm, pd), lambda i: (i, 0)),
            pl.BlockSpec(memory_space=pl.ANY),
            pl.BlockSpec((1, d), lambda i: (0, 0)),
            pl.BlockSpec((1, d), lambda i: (0, 0)),
        ],
        out_specs=pl.BlockSpec((tm, d), lambda i: (i, 0)),
        out_shape=jax.ShapeDtypeStruct((t, d), F32),
        scratch_shapes=[pltpu.VMEM((d, d), BF16), pltpu.VMEM((pd, d), BF16),
                        pltpu.VMEM((2, stage_rows, d), F32), pltpu.SemaphoreType.DMA((2,))],
        compiler_params=pltpu.CompilerParams(
            dimension_semantics=("arbitrary",),
            vmem_limit_bytes=_vmem_limit(
                _nbytes((tm, d), BF16), 2 * _nbytes((tm, d), F32), _nbytes((tm, pd), F32),
                scratch=_nbytes((d + pd, d), BF16) + _nbytes((2, stage_rows, d), F32),
                temps=2 * _nbytes((tm, d), F32) + 4 * _nbytes((tm, tn), F32)),
        ),
        name="ple_final",
    )(un, w_gate, h, p2, w_proj, g_post, g_final)


def kernel(x, p, g_mix, w_in, q_norm_a, k_norm_a, q_a_norm, w_q_b, kv_a_norm, w_kv_b,
           w_a_out, w_b_out, w_o, g_ffn, w_gate_up, w_down, g_ple, w_ple_gate,
           w_ple_proj, g_ple_post, g_final):
    batch, seq, d = x.shape
    depth = w_in.shape[0]
    t = batch * seq
    q_lora = q_a_norm.shape[-1]
    kv_lora = kv_a_norm.shape[-1]
    assert q_lora == kv_lora
    n_qa = N_HEADS_A * HEAD_DIM_A
    n_kva = N_KV_A * HEAD_DIM_A
    lat0 = n_qa + 2 * n_kva
    kr0 = lat0 + q_lora + kv_lora
    gate0 = kr0 + ROPE_DIM_B

    cos_a, sin_a = _rope_tables(seq, HEAD_DIM_A)
    cos_b, sin_b = _rope_tables(seq, ROPE_DIM_B)

    h = x.reshape(t, d)
    for i in range(depth):
        wi_t = jnp.swapaxes(w_in[i], 0, 1)
        wq_pad = jnp.pad(
            w_q_b[i].reshape(q_lora, N_HEADS_B, QK_DIM_B),
            ((0, 0), (0, 0), (0, QK_PAD_B - QK_DIM_B))).reshape(q_lora, N_HEADS_B * QK_PAD_B)
        lat_gains = jnp.stack([q_a_norm[i], kv_a_norm[i]])[:, None, :]
        qkv_a, q_b, k_b, v_b, u = _inproj(
            h, g_mix[i][None], wi_t, wq_pad, w_kv_b[i], q_norm_a[i][None], k_norm_a[i][None],
            lat_gains, cos_a, sin_a, cos_b, sin_b, seq=seq, tm=512, kr_col0=kr0)

        oa = _attention(qkv_a, qkv_a, qkv_a, batch=batch, seq=seq, n_kv=N_KV_A,
                        groups=N_HEADS_A // N_KV_A, dk=HEAD_DIM_A, dv=HEAD_DIM_A,
                        q_col0=0, k_col0=n_qa, v_col0=n_qa + n_kva, tq=1024, kc=512, sq=512,
                        aug_lane=HEAD_DIM_A, q_gain=q_norm_a[i][None], k_gain=k_norm_a[i][None],
                        unit_rms_scale=(HEAD_DIM_A ** -0.5) * LOG2E)
        ob = _attention(q_b, k_b, v_b, batch=batch, seq=seq, n_kv=N_HEADS_B,
                        groups=1, dk=QK_PAD_B, dv=V_DIM_B,
                        q_col0=0, k_col0=0, v_col0=0, tq=4096, kc=512, sq=512,
                        aug_lane=QK_DIM_B)

        merged = _merge(u, oa, ob, w_a_out[i], w_b_out[i], wi_t, tm=512, tn=512,
                        gate_row0=gate0, stage_rows=256)
        h1, un1 = _resid_rows(merged, w_o[i], h, g_ffn[i][None], tm=512, tn=512, stage_rows=256)
        act = _ffn_up(un1, w_gate_up[i], tm=1024, tn=512)
        h2, un2 = _resid_rows(act, w_down[i], h1, g_ple[i][None], tm=256, tn=512, stage_rows=256)
        h = _ple(un2, w_ple_gate[i], h2, p[i].reshape(t, -1), w_ple_proj[i],
                 g_ple_post[i][None], g_final[None], tm=512, tn=512,
                 final_norm=(i == depth - 1), stage_rows=256)
    return h.reshape(batch, seq, d)
```

```python
import functools
import math

import numpy as np
import jax
import jax.numpy as jnp
from jax import lax
from jax.experimental import pallas as pl
from jax.experimental.pallas import tpu as pltpu

F32 = jnp.float32
BF16 = jnp.bfloat16

GRID_W = 64
ROPE_THETA = 10000.0
RMS_EPS = 1e-6

N_HEADS_A = 8
N_KV_A = 2
HEAD_DIM_A = 128
N_HEADS_B = 8
NOPE_DIM_B = 128
ROPE_DIM_B = 64
V_DIM_B = 128
QK_DIM_B = NOPE_DIM_B + ROPE_DIM_B
QK_PAD_B = 256

LANES = 128
BF16_ROWS = 16
V7X_VMEM_BYTES = 64 * 1024 * 1024
VMEM_SLACK_BYTES = 4 << 20
VMEM_RESERVED_BYTES = 6 << 20
LOG2E = math.log2(math.e)

ROW_TILE = 512
ROW_TILE_DOWN = 256
ROW_TILE_FFN_UP = 1024
COL_TILE = 512
STAGE_ROWS = 256
ATTN_Q_STREAM = 512
ATTN_KV_CHUNK = 512
ATTN_STREAMS = 8


def _vmem_limit(*block_bytes, scratch=0, temps=0):
    need = 2 * sum(block_bytes) + scratch + temps + VMEM_SLACK_BYTES
    return int(min(need, V7X_VMEM_BYTES - VMEM_RESERVED_BYTES))


def _nbytes(shape, dtype):
    return math.prod(shape) * jnp.dtype(dtype).itemsize


def _rms(x32, gain_row):
    ms = jnp.mean(x32 * x32, axis=-1, keepdims=True)
    return x32 * lax.rsqrt(ms + RMS_EPS) * gain_row


def _swap_chunks(y, chunk):
    lane = lax.broadcasted_iota(jnp.int32, y.shape, 1)
    first = (lane % (2 * chunk)) < chunk
    return jnp.where(first, pltpu.roll(y, LANES - chunk, 1), pltpu.roll(y, chunk, 1))


def _rope(y, cos, sin_signed, chunk):
    return y * cos + _swap_chunks(y, chunk) * sin_signed


def _rope_tables(seq, rot_dim):
    rows = seq // GRID_W
    row = np.repeat(np.arange(rows, dtype=np.float64), GRID_W)
    col = np.tile(np.arange(GRID_W, dtype=np.float64), rows)
    half = rot_dim // 2
    inv = 1.0 / np.power(ROPE_THETA, np.arange(0, half, 2, dtype=np.float64) / half)
    ar = row[:, None] * inv[None, :]
    ac = col[:, None] * inv[None, :]
    cos = np.concatenate([np.cos(ar), np.cos(ar), np.cos(ac), np.cos(ac)], axis=-1)
    sin = np.concatenate([-np.sin(ar), np.sin(ar), -np.sin(ac), np.sin(ac)], axis=-1)
    pad = LANES - rot_dim
    if pad:
        cos = np.concatenate([cos, np.ones((seq, pad))], axis=-1)
        sin = np.concatenate([sin, np.zeros((seq, pad))], axis=-1)
    return jnp.asarray(cos, F32), jnp.asarray(sin, F32)


def _stage_bf16(w_hbm, dst_ref, stage_ref, sem_ref, *, row0=0, n_rows=None, transpose=False,
                keep_rows_last=None):
    rows = stage_ref.shape[1]
    n_rows = w_hbm.shape[0] if n_rows is None else n_rows
    n_chunks = n_rows // rows

    def chunk_copy(r, slot):
        return pltpu.make_async_copy(w_hbm.at[pl.ds(row0 + r * rows, rows), :],
                                     stage_ref.at[slot], sem_ref.at[slot])

    chunk_copy(0, 0).start()
    for r in range(n_chunks):
        slot = r % 2
        if r + 1 < n_chunks:
            chunk_copy(r + 1, 1 - slot).start()
        chunk_copy(r, slot).wait()
        blk = stage_ref[slot]
        if keep_rows_last is not None and r == n_chunks - 1:
            row = lax.broadcasted_iota(jnp.int32, blk.shape, 0)
            blk = jnp.where(row < keep_rows_last, blk, 0.0)
        if transpose:
            dst_ref[:, r * rows:(r + 1) * rows] = blk.T.astype(BF16)
        else:
            dst_ref[r * rows:(r + 1) * rows, :] = blk.astype(BF16)


def _inproj_kernel(x_ref, g_ref, wt_hbm, wq_hbm, wkv_hbm, qn_ref, kn_ref, lg_ref,
                   cosa_ref, sina_ref, cosb_ref, sinb_ref,
                   qkv_ref, qb_ref, kb_ref, vb_ref, u_ref,
                   wbf_ref, wqbf_ref, wkvbf_ref, stage_ref, sem_ref,
                   *, q_scale, qb_scale, n_q, n_kv, lora, kr_rows):
    hd = HEAD_DIM_A

    @pl.when(pl.program_id(0) == 0)
    def _():
        _stage_bf16(wt_hbm, wbf_ref, stage_ref, sem_ref, n_rows=wbf_ref.shape[1],
                    transpose=True, keep_rows_last=kr_rows)
        _stage_bf16(wq_hbm, wqbf_ref, stage_ref, sem_ref)
        _stage_bf16(wkv_hbm, wkvbf_ref, stage_ref, sem_ref)

    u = _rms(x_ref[...], g_ref[...]).astype(BF16)
    u_ref[...] = u
    cos = cosa_ref[...]
    sin = sina_ref[...]

    def proj(c0, width):
        return jnp.dot(u, wbf_ref[:, c0:c0 + width], preferred_element_type=F32)

    heads_per_dot = COL_TILE // hd
    for h0 in range(0, n_q, heads_per_dot):
        z = proj(h0 * hd, heads_per_dot * hd)
        for h in range(heads_per_dot):
            y = _rms(z[:, h * hd:(h + 1) * hd], qn_ref[...])
            qkv_ref[:, (h0 + h) * hd:(h0 + h + 1) * hd] = (
                _rope(y, cos, sin, hd // 4) * q_scale).astype(BF16)

    z = proj(n_q * hd, 2 * n_kv * hd)
    for h in range(n_kv):
        y = _rms(z[:, h * hd:(h + 1) * hd], kn_ref[...])
        qkv_ref[:, (n_q + h) * hd:(n_q + h + 1) * hd] = _rope(y, cos, sin, hd // 4).astype(BF16)
    qkv_ref[:, (n_q + n_kv) * hd:] = z[:, n_kv * hd:].astype(BF16)

    lat0 = (n_q + 2 * n_kv) * hd
    cq = _rms(proj(lat0, lora), lg_ref[0]).astype(BF16)
    ckv = _rms(proj(lat0 + lora, lora), lg_ref[1]).astype(BF16)
    cosb = cosb_ref[...]
    sinb = sinb_ref[...]
    kpe = _rope(proj(lat0 + 2 * lora, LANES), cosb, sinb, ROPE_DIM_B // 4)
    lane = lax.broadcasted_iota(jnp.int32, kpe.shape, 1)
    kpe = jnp.where(lane == ROPE_DIM_B, 1.0, kpe).astype(BF16)
    zq = jnp.dot(cq, wqbf_ref[...], preferred_element_type=F32)
    zkv = jnp.dot(ckv, wkvbf_ref[...], preferred_element_type=F32)
    for h in range(N_HEADS_B):
        c0 = h * QK_PAD_B
        qb_ref[:, c0:c0 + LANES] = (zq[:, c0:c0 + LANES] * qb_scale).astype(BF16)
        pe = _rope(zq[:, c0 + LANES:c0 + 2 * LANES], cosb, sinb, ROPE_DIM_B // 4)
        qb_ref[:, c0 + LANES:c0 + 2 * LANES] = (pe * qb_scale).astype(BF16)
        kb_ref[:, c0:c0 + LANES] = zkv[:, c0:c0 + LANES].astype(BF16)
        kb_ref[:, c0 + LANES:c0 + 2 * LANES] = kpe
        vb_ref[:, h * V_DIM_B:(h + 1) * V_DIM_B] = zkv[:, c0 + LANES:c0 + 2 * LANES].astype(BF16)


def _inproj(x2, g_mix, w_in_t, wq_pad, w_kv_b, q_norm, k_norm, lat_gains,
            cos_a, sin_a, cos_b, sin_b, *, seq, tm, kr_col0):
    t, d = x2.shape
    lora = lat_gains.shape[-1]
    n_qkv = (N_HEADS_A + 2 * N_KV_A) * HEAD_DIM_A
    n_cols = n_qkv + 2 * lora + LANES
    nqb = wq_pad.shape[1]
    nvb = N_HEADS_B * V_DIM_B
    assert kr_col0 == n_qkv + 2 * lora and n_cols % LANES == 0
    assert (N_HEADS_A * HEAD_DIM_A) % COL_TILE == 0 and w_in_t.shape[0] >= n_cols
    assert t % tm == 0 and seq % tm == 0 and lora % LANES == 0
    assert wq_pad.shape == (lora, N_HEADS_B * QK_PAD_B) and w_kv_b.shape == (lora, nqb)
    assert nqb == d
    sblocks = seq // tm
    q_scale = (HEAD_DIM_A ** -0.5) * LOG2E
    qb_scale = (QK_DIM_B ** -0.5) * LOG2E
    return pl.pallas_call(
        functools.partial(_inproj_kernel, q_scale=q_scale, qb_scale=qb_scale, n_q=N_HEADS_A,
                          n_kv=N_KV_A, lora=lora, kr_rows=ROPE_DIM_B),
        grid=(t // tm,),
        in_specs=[
            pl.BlockSpec((tm, d), lambda i: (i, 0)),
            pl.BlockSpec((1, d), lambda i: (0, 0)),
            pl.BlockSpec(memory_space=pl.ANY),
            pl.BlockSpec(memory_space=pl.ANY),
            pl.BlockSpec(memory_space=pl.ANY),
            pl.BlockSpec((1, HEAD_DIM_A), lambda i: (0, 0)),
            pl.BlockSpec((1, HEAD_DIM_A), lambda i: (0, 0)),
            pl.BlockSpec((2, 1, lora), lambda i: (0, 0, 0)),
            pl.BlockSpec((tm, LANES), lambda i: (i % sblocks, 0)),
            pl.BlockSpec((tm, LANES), lambda i: (i % sblocks, 0)),
            pl.BlockSpec((tm, LANES), lambda i: (i % sblocks, 0)),
            pl.BlockSpec((tm, LANES), lambda i: (i % sblocks, 0)),
        ],
        out_specs=[
            pl.BlockSpec((tm, n_qkv), lambda i: (i, 0)),
            pl.BlockSpec((tm, nqb), lambda i: (i, 0)),
            pl.BlockSpec((tm, nqb), lambda i: (i, 0)),
            pl.BlockSpec((tm, nvb), lambda i: (i, 0)),
            pl.BlockSpec((tm, d), lambda i: (i, 0)),
        ],
        out_shape=[
            jax.ShapeDtypeStruct((t, n_qkv), BF16),
            jax.ShapeDtypeStruct((t, nqb), BF16),
            jax.ShapeDtypeStruct((t, nqb), BF16),
            jax.ShapeDtypeStruct((t, nvb), BF16),
            jax.ShapeDtypeStruct((t, d), BF16),
        ],
        scratch_shapes=[pltpu.VMEM((d, n_cols), BF16), pltpu.VMEM((lora, nqb), BF16),
                        pltpu.VMEM((lora, nqb), BF16), pltpu.VMEM((2, LANES, d), F32),
                        pltpu.SemaphoreType.DMA((2,))],
        compiler_params=pltpu.CompilerParams(
            dimension_semantics=("arbitrary",),
            vmem_limit_bytes=_vmem_limit(
                _nbytes((tm, d), F32), _nbytes((tm, n_qkv), BF16), 2 * _nbytes((tm, nqb), BF16),
                _nbytes((tm, nvb), BF16), _nbytes((tm, d), BF16), 4 * _nbytes((tm, LANES), F32),
                scratch=_nbytes((d, n_cols), BF16) + 2 * _nbytes((lora, nqb), BF16)
                + _nbytes((2, LANES, d), F32),
                temps=_nbytes((tm, d), F32) + 4 * _nbytes((tm, nqb), F32)),
        ),
        name="inproj",
    )(x2, g_mix, w_in_t, wq_pad, w_kv_b, q_norm, k_norm, lat_gains, cos_a, sin_a, cos_b, sin_b)


_L_SAFE = 2.0 ** -64


def _attn_kernel(q_ref, k_ref, v_ref, qg_ref, kg_ref, o_ref, vt_ref, kmax_ref,
                 qa_ref, s_ref, acc_ref, *maybe_kaug, groups, dk, dv, kc, sq, aug_lane,
                 unit_rms_scale):
    seq = k_ref.shape[0]
    tq = q_ref.shape[0]
    append_tile = aug_lane == dk
    kaug_ref = maybe_kaug[0] if append_tile else k_ref
    dka = kaug_ref.shape[1]

    @pl.when(pl.program_id(2) == 0)
    def _():
        if unit_rms_scale is None:
            kt = k_ref[...].astype(F32).T
            kmax2 = jnp.max(jnp.sum(kt * kt, axis=0, keepdims=True), axis=1, keepdims=True)
            kmax_ref[...] = jnp.broadcast_to(kmax2, kmax_ref.shape)
        if append_tile:
            kaug_ref[:, :dk] = k_ref[...]
            lane = lax.broadcasted_iota(jnp.int32, (seq, dka - dk), 1)
            kaug_ref[:, dk:] = jnp.where(lane == 0, 1.0, 0.0).astype(BF16)
        vt_ref[:dv, :] = v_ref[...].astype(F32).T.astype(BF16)
        vt_ref[dv:, :] = jnp.ones((vt_ref.shape[0] - dv, seq), BF16)

    streams = [(r, g) for r in range(tq // sq) for g in range(groups)]

    n_str = len(streams)

    def stage_queries(shift_row_fn):
        for i, (r, g) in enumerate(streams):
            qt = q_ref[r * sq:(r + 1) * sq, g * dk:(g + 1) * dk].astype(F32).T
            neg = -shift_row_fn(i, qt)
            if append_tile:
                row = lax.broadcasted_iota(jnp.int32, (dka - dk, sq), 0)
                qa = jnp.concatenate([qt, jnp.where(row == 0, neg, 0.0)], axis=0)
            else:
                row = lax.broadcasted_iota(jnp.int32, (dk, sq), 0)
                qa = jnp.where(row == aug_lane, neg, qt)
            qa_ref[i] = qa.astype(BF16)

    def run():
        n_chunks = seq // kc
        acc_ref[...] = jnp.zeros(acc_ref.shape, F32)

        def scores(i, c):
            start = c * kc if isinstance(c, int) else pl.multiple_of(c * kc, kc)
            s_ref[i % 2] = jnp.dot(kaug_ref[pl.ds(start, kc), :], qa_ref[i],
                                   preferred_element_type=F32)

        def accumulate(i, c):
            start = c * kc if isinstance(c, int) else pl.multiple_of(c * kc, kc)
            p = jnp.exp2(s_ref[i % 2]).astype(BF16)
            acc_ref[i] += jnp.dot(vt_ref[:, pl.ds(start, kc)], p, preferred_element_type=F32)

        def chunk(c, last):
            for i in range(n_str):
                if i + 1 < n_str:
                    scores(i + 1, c)
                elif not last:
                    scores(0, c + 1)
                accumulate(i, c)

        def body(c, carry):
            chunk(c, last=False)
            return carry

        scores(0, 0)
        lax.fori_loop(0, n_chunks - 1, body, 0, unroll=2)
        chunk(n_chunks - 1, last=True)
        return tuple((acc_ref[i, :dv, :], acc_ref[i, dv:dv + 1, :]) for i in range(n_str))

    def write_out(results):
        for (r, g), (acc, l) in zip(streams, results):
            o_ref[r * sq:(r + 1) * sq, g * dv:(g + 1) * dv] = (acc * (1.0 / l)).T.astype(BF16)

    if unit_rms_scale is None:
        kmax2 = kmax_ref[0:1, 0:1]
        stage_queries(
            lambda i, qt: jnp.sqrt(jnp.sum(qt * qt, axis=0, keepdims=True) * kmax2))
    else:
        bound = (jnp.max(jnp.abs(qg_ref[...]), axis=1, keepdims=True)
                 * jnp.max(jnp.abs(kg_ref[...]), axis=1, keepdims=True) * (dk * unit_rms_scale))
        stage_queries(lambda i, qt: jnp.broadcast_to(bound, (1, sq)))
    fast = run()
    write_out(fast)

    l_min = None
    for _, l in fast:
        cur = jnp.min(l)
        l_min = cur if l_min is None else jnp.minimum(l_min, cur)

    @pl.when(jnp.logical_not(l_min >= _L_SAFE))
    def _():
        stage_queries(lambda i, qt: jnp.zeros((1, sq), F32))

        def max_body(c, ms):
            start = pl.multiple_of(c * kc, kc)
            ks = kaug_ref[pl.ds(start, kc), :]
            return tuple(
                jnp.maximum(m, jnp.max(
                    jnp.dot(ks, qa_ref[i], preferred_element_type=F32), axis=0, keepdims=True))
                for i, m in enumerate(ms))

        ms = lax.fori_loop(0, seq // kc, max_body,
                           tuple(jnp.full((1, sq), -jnp.inf, F32) for _ in streams))
        stage_queries(lambda i, qt: ms[i])
        write_out(run())


def _attention(q_arr, k_arr, v_arr, *, batch, seq, n_kv, groups, dk, dv,
               q_col0, k_col0, v_col0, tq, kc, sq, aug_lane, q_gain=None, k_gain=None,
               unit_rms_scale=None):
    t = batch * seq
    if unit_rms_scale is None:
        q_gain = k_gain = jnp.ones((1, dk), F32)
    nq = seq // tq
    qb0 = q_col0 // (groups * dk)
    kb0 = k_col0 // dk
    vb0 = v_col0 // dv
    dka = dk + LANES if aug_lane == dk else dk
    n_str = (tq // sq) * groups
    assert aug_lane <= dk and tq % sq == 0 and seq % kc == 0 and n_str % 2 == 0
    dvx = dv + BF16_ROWS
    scratch_shapes = [pltpu.VMEM((dvx, seq), BF16), pltpu.VMEM((8, LANES), F32),
                      pltpu.VMEM((n_str, dka, sq), BF16), pltpu.VMEM((2, kc, sq), F32),
                      pltpu.VMEM((n_str, dvx, sq), F32)]
    scratch_bytes = (_nbytes((dvx, seq), BF16) + _nbytes((n_str, dka, sq), BF16)
                     + _nbytes((2, kc, sq), F32) + _nbytes((n_str, dvx, sq), F32))
    if aug_lane == dk:
        scratch_shapes.append(pltpu.VMEM((seq, dka), BF16))
        scratch_bytes += _nbytes((seq, dka), BF16)
    return pl.pallas_call(
        functools.partial(_attn_kernel, groups=groups, dk=dk, dv=dv, kc=kc, sq=sq,
                          aug_lane=aug_lane, unit_rms_scale=unit_rms_scale),
        grid=(batch, n_kv, nq),
        in_specs=[
            pl.BlockSpec((tq, groups * dk), lambda b, h, i: (b * nq + i, qb0 + h)),
            pl.BlockSpec((seq, dk), lambda b, h, i: (b, kb0 + h)),
            pl.BlockSpec((seq, dv), lambda b, h, i: (b, vb0 + h)),
            pl.BlockSpec((1, dk), lambda b, h, i: (0, 0)),
            pl.BlockSpec((1, dk), lambda b, h, i: (0, 0)),
        ],
        out_specs=pl.BlockSpec((tq, groups * dv), lambda b, h, i: (b * nq + i, h)),
        out_shape=jax.ShapeDtypeStruct((t, n_kv * groups * dv), BF16),
        scratch_shapes=scratch_shapes,
        compiler_params=pltpu.CompilerParams(
            dimension_semantics=("parallel", "parallel", "arbitrary"),
            vmem_limit_bytes=_vmem_limit(
                _nbytes((tq, groups * dk), BF16), _nbytes((seq, dk), BF16),
                _nbytes((seq, dv), BF16), _nbytes((tq, groups * dv), BF16),
                scratch=scratch_bytes,
                temps=2 * _nbytes((seq, dk), F32) + 2 * _nbytes((seq, dv), F32)
                + 4 * _nbytes((kc, sq), F32)),
        ),
        name="attention",
    )(q_arr, k_arr, v_arr, q_gain, k_gain)


def _merge_kernel(u_ref, oa_ref, ob_ref, wa_hbm, wb_hbm, wt_hbm, o_ref,
                  wabf_ref, wbbf_ref, wgbf_ref, stage_ref, sem_ref, *, tn, gate_row0):
    @pl.when(pl.program_id(0) == 0)
    def _():
        _stage_bf16(wa_hbm, wabf_ref, stage_ref, sem_ref)
        _stage_bf16(wb_hbm, wbbf_ref, stage_ref, sem_ref)
        _stage_bf16(wt_hbm, wgbf_ref, stage_ref, sem_ref, row0=gate_row0,
                    n_rows=wgbf_ref.shape[1], transpose=True)

    u = u_ref[...]
    oa = oa_ref[...]
    ob = ob_ref[...]
    d = o_ref.shape[1]
    for jj in range(d // tn):
        cols = slice(jj * tn, (jj + 1) * tn)
        gcols = slice(d + jj * tn, d + (jj + 1) * tn)
        ya = jnp.dot(oa, wabf_ref[:, cols], preferred_element_type=F32)
        ga = jnp.dot(u, wgbf_ref[:, cols], preferred_element_type=F32)
        yb = jnp.dot(ob, wbbf_ref[:, cols], preferred_element_type=F32)
        gb = jnp.dot(u, wgbf_ref[:, gcols], preferred_element_type=F32)
        o_ref[:, cols] = (jax.nn.sigmoid(ga) * ya + jax.nn.sigmoid(gb) * yb).astype(BF16)


def _merge(u, oa, ob, w_a_out, w_b_out, w_in_t, *, tm, tn, gate_row0, stage_rows):
    t, d = u.shape
    ka = oa.shape[1]
    kb = ob.shape[1]
    assert ka % stage_rows == 0 and kb % stage_rows == 0 and (2 * d) % stage_rows == 0
    assert w_a_out.shape[1] == d and w_in_t.shape[1] == d
    return pl.pallas_call(
        functools.partial(_merge_kernel, tn=tn, gate_row0=gate_row0),
        grid=(t // tm,),
        in_specs=[
            pl.BlockSpec((tm, d), lambda i: (i, 0)),
            pl.BlockSpec((tm, ka), lambda i: (i, 0)),
            pl.BlockSpec((tm, kb), lambda i: (i, 0)),
            pl.BlockSpec(memory_space=pl.ANY),
            pl.BlockSpec(memory_space=pl.ANY),
            pl.BlockSpec(memory_space=pl.ANY),
        ],
        out_specs=pl.BlockSpec((tm, d), lambda i: (i, 0)),
        out_shape=jax.ShapeDtypeStruct((t, d), BF16),
        scratch_shapes=[pltpu.VMEM((ka, d), BF16), pltpu.VMEM((kb, d), BF16),
                        pltpu.VMEM((d, 2 * d), BF16), pltpu.VMEM((2, stage_rows, d), F32),
                        pltpu.SemaphoreType.DMA((2,))],
        compiler_params=pltpu.CompilerParams(
            dimension_semantics=("arbitrary",),
            vmem_limit_bytes=_vmem_limit(
                _nbytes((tm, d), BF16), _nbytes((tm, ka), BF16), _nbytes((tm, kb), BF16),
                _nbytes((tm, d), BF16),
                scratch=_nbytes((ka + kb + 2 * d, d), BF16) + _nbytes((2, stage_rows, d), F32),
                temps=8 * _nbytes((tm, tn), F32)),
        ),
        name="merge",
    )(u, oa, ob, w_a_out, w_b_out, w_in_t)


def _resid_rows_kernel(lhs_ref, w_hbm, resid_ref, g_ref, h_ref, un_ref,
                       wbf_ref, stage_ref, sem_ref, *, tn):
    @pl.when(pl.program_id(0) == 0)
    def _():
        _stage_bf16(w_hbm, wbf_ref, stage_ref, sem_ref)

    lhs = lhs_ref[...]
    d = h_ref.shape[1]
    ss = jnp.zeros((h_ref.shape[0], 1), F32)
    for jj in range(d // tn):
        cols = slice(jj * tn, (jj + 1) * tn)
        val = resid_ref[:, cols] + jnp.dot(lhs, wbf_ref[:, cols], preferred_element_type=F32)
        h_ref[:, cols] = val
        ss = ss + jnp.sum(val * val, axis=-1, keepdims=True)
    inv = lax.rsqrt(ss * (1.0 / d) + RMS_EPS)
    un_ref[...] = (h_ref[...] * inv * g_ref[...]).astype(BF16)


def _resid_rows(lhs, w, resid, gain, *, tm, tn, stage_rows):
    t, k = lhs.shape
    d = w.shape[1]
    assert k % stage_rows == 0 and d % tn == 0
    return pl.pallas_call(
        functools.partial(_resid_rows_kernel, tn=tn),
        grid=(t // tm,),
        in_specs=[
            pl.BlockSpec((tm, k), lambda i: (i, 0)),
            pl.BlockSpec(memory_space=pl.ANY),
            pl.BlockSpec((tm, d), lambda i: (i, 0)),
            pl.BlockSpec((1, d), lambda i: (0, 0)),
        ],
        out_specs=[
            pl.BlockSpec((tm, d), lambda i: (i, 0)),
            pl.BlockSpec((tm, d), lambda i: (i, 0)),
        ],
        out_shape=[
            jax.ShapeDtypeStruct((t, d), F32),
            jax.ShapeDtypeStruct((t, d), BF16),
        ],
        scratch_shapes=[pltpu.VMEM((k, d), BF16), pltpu.VMEM((2, stage_rows, d), F32),
                        pltpu.SemaphoreType.DMA((2,))],
        compiler_params=pltpu.CompilerParams(
            dimension_semantics=("arbitrary",),
            vmem_limit_bytes=_vmem_limit(
                _nbytes((tm, k), BF16), 2 * _nbytes((tm, d), F32), _nbytes((tm, d), BF16),
                scratch=_nbytes((k, d), BF16) + _nbytes((2, stage_rows, d), F32),
                temps=4 * _nbytes((tm, tn), F32) + _nbytes((tm, d), F32)),
        ),
        name="resid_rows",
    )(lhs, w, resid, gain)


def _ffn_up_kernel(un_ref, wg_ref, wu_ref, o_ref):
    un = un_ref[...]
    g = jnp.dot(un, wg_ref[...].astype(BF16), preferred_element_type=F32)
    u = jnp.dot(un, wu_ref[...].astype(BF16), preferred_element_type=F32)
    o_ref[...] = (jax.nn.silu(g) * u).astype(BF16)


def _ffn_up(un, w_gate_up, *, tm, tn):
    t, d = un.shape
    d_ff = w_gate_up.shape[1] // 2
    nj = d_ff // tn
    return pl.pallas_call(
        _ffn_up_kernel,
        grid=(t // tm, nj),
        in_specs=[
            pl.BlockSpec((tm, d), lambda i, j: (i, 0)),
            pl.BlockSpec((d, tn), lambda i, j: (0, j)),
            pl.BlockSpec((d, tn), lambda i, j: (0, nj + j)),
        ],
        out_specs=pl.BlockSpec((tm, tn), lambda i, j: (i, j)),
        out_shape=jax.ShapeDtypeStruct((t, d_ff), BF16),
        compiler_params=pltpu.CompilerParams(
            dimension_semantics=("parallel", "arbitrary"),
            vmem_limit_bytes=_vmem_limit(
                _nbytes((tm, d), BF16), 2 * _nbytes((d, tn), F32), _nbytes((tm, tn), BF16),
                temps=2 * _nbytes((d, tn), BF16) + 4 * _nbytes((tm, tn), F32)),
        ),
        name="ffn_up",
    )(un, w_gate_up, w_gate_up)


def _ple_kernel(un_ref, w_hbm, h_ref, p_ref, wp_hbm, gpost_ref, gfin_ref,
                o_ref, wbf_ref, wpbf_ref, stage_ref, sem_ref, *, tn, final_norm):
    @pl.when(pl.program_id(0) == 0)
    def _():
        _stage_bf16(w_hbm, wbf_ref, stage_ref, sem_ref)
        _stage_bf16(wp_hbm, wpbf_ref, stage_ref, sem_ref)

    un = un_ref[...]
    d = o_ref.shape[1]
    e = _rms(jnp.dot(p_ref[...].astype(BF16), wpbf_ref[...], preferred_element_type=F32),
             gpost_ref[...])
    ss = jnp.zeros((o_ref.shape[0], 1), F32)
    for jj in range(d // tn):
        cols = slice(jj * tn, (jj + 1) * tn)
        gate = jax.nn.sigmoid(jnp.dot(un, wbf_ref[:, cols], preferred_element_type=F32))
        val = h_ref[:, cols] + gate * e[:, cols]
        o_ref[:, cols] = val
        ss = ss + jnp.sum(val * val, axis=-1, keepdims=True)
    if final_norm:
        inv = lax.rsqrt(ss * (1.0 / d) + RMS_EPS)
        o_ref[...] = o_ref[...] * inv * gfin_ref[...]


def _ple(un, w_gate, h, p2, w_proj, g_post, g_final, *, tm, tn, final_norm, stage_rows):
    t, d = un.shape
    pd = p2.shape[1]
    assert d % stage_rows == 0 and pd % stage_rows == 0
    return pl.pallas_call(
        functools.partial(_ple_kernel, tn=tn, final_norm=final_norm),
        grid=(t // tm,),
        in_specs=[
            pl.BlockSpec((tm, d), lambda i: (i, 0)),
            pl.BlockSpec(memory_space=pl.ANY),
            pl.BlockSpec((tm, d), lambda i: (i, 0)),
            pl.BlockSpec((tm, pd), lambda i: (i, 0)),
            pl.BlockSpec(memory_space=pl.ANY),
            pl.BlockSpec((1, d), lambda i: (0, 0)),
            pl.BlockSpec((1, d), lambda i: (0, 0)),
        ],
        out_specs=pl.BlockSpec((tm, d), lambda i: (i, 0)),
        out_shape=jax.ShapeDtypeStruct((t, d), F32),
        scratch_shapes=[pltpu.VMEM((d, d), BF16), pltpu.VMEM((pd, d), BF16),
                        pltpu.VMEM((2, stage_rows, d), F32), pltpu.SemaphoreType.DMA((2,))],
        compiler_params=pltpu.CompilerParams(
            dimension_semantics=("arbitrary",),
            vmem_limit_bytes=_vmem_limit(
                _nbytes((tm, d), BF16), 2 * _nbytes((tm, d), F32), _nbytes((tm, pd), F32),
                scratch=_nbytes((d + pd, d), BF16) + _nbytes((2, stage_rows, d), F32),
                temps=2 * _nbytes((tm, d), F32) + 4 * _nbytes((tm, tn), F32)),
        ),
        name="ple_final",
    )(un, w_gate, h, p2, w_proj, g_post, g_final)


def kernel(x, p, g_mix, w_in, q_norm_a, k_norm_a, q_a_norm, w_q_b, kv_a_norm, w_kv_b,
           w_a_out, w_b_out, w_o, g_ffn, w_gate_up, w_down, g_ple, w_ple_gate,
           w_ple_proj, g_ple_post, g_final):
    batch, seq, d = x.shape
    depth = w_in.shape[0]
    t = batch * seq
    q_lora = q_a_norm.shape[-1]
    kv_lora = kv_a_norm.shape[-1]
    assert q_lora == kv_lora
    n_qa = N_HEADS_A * HEAD_DIM_A
    n_kva = N_KV_A * HEAD_DIM_A
    lat0 = n_qa + 2 * n_kva
    kr0 = lat0 + q_lora + kv_lora
    gate0 = kr0 + ROPE_DIM_B

    cos_a, sin_a = _rope_tables(seq, HEAD_DIM_A)
    cos_b, sin_b = _rope_tables(seq, ROPE_DIM_B)

    h = x.reshape(t, d)
    for i in range(depth):
        wi_t = jnp.swapaxes(w_in[i], 0, 1)
        wq_pad = jnp.pad(
            w_q_b[i].reshape(q_lora, N_HEADS_B, QK_DIM_B),
            ((0, 0), (0, 0), (0, QK_PAD_B - QK_DIM_B))).reshape(q_lora, N_HEADS_B * QK_PAD_B)
        lat_gains = jnp.stack([q_a_norm[i], kv_a_norm[i]])[:, None, :]
        qkv_a, q_b, k_b, v_b, u = _inproj(
            h, g_mix[i][None], wi_t, wq_pad, w_kv_b[i], q_norm_a[i][None], k_norm_a[i][None],
            lat_gains, cos_a, sin_a, cos_b, sin_b, seq=seq, tm=ROW_TILE, kr_col0=kr0)

        groups_a = N_HEADS_A // N_KV_A
        oa = _attention(qkv_a, qkv_a, qkv_a, batch=batch, seq=seq, n_kv=N_KV_A,
                        groups=groups_a, dk=HEAD_DIM_A, dv=HEAD_DIM_A,
                        q_col0=0, k_col0=n_qa, v_col0=n_qa + n_kva,
                        tq=ATTN_STREAMS * ATTN_Q_STREAM // groups_a, kc=ATTN_KV_CHUNK,
                        sq=ATTN_Q_STREAM, aug_lane=HEAD_DIM_A, q_gain=q_norm_a[i][None],
                        k_gain=k_norm_a[i][None], unit_rms_scale=(HEAD_DIM_A ** -0.5) * LOG2E)
        ob = _attention(q_b, k_b, v_b, batch=batch, seq=seq, n_kv=N_HEADS_B,
                        groups=1, dk=QK_PAD_B, dv=V_DIM_B, q_col0=0, k_col0=0, v_col0=0,
                        tq=ATTN_STREAMS * ATTN_Q_STREAM, kc=ATTN_KV_CHUNK, sq=ATTN_Q_STREAM,
                        aug_lane=QK_DIM_B)

        merged = _merge(u, oa, ob, w_a_out[i], w_b_out[i], wi_t, tm=ROW_TILE, tn=COL_TILE,
                        gate_row0=gate0, stage_rows=STAGE_ROWS)
        h1, un1 = _resid_rows(merged, w_o[i], h, g_ffn[i][None], tm=ROW_TILE, tn=COL_TILE,
                              stage_rows=STAGE_ROWS)
        act = _ffn_up(un1, w_gate_up[i], tm=ROW_TILE_FFN_UP, tn=COL_TILE)
        h2, un2 = _resid_rows(act, w_down[i], h1, g_ple[i][None], tm=ROW_TILE_DOWN, tn=COL_TILE,
                              stage_rows=STAGE_ROWS)
        h = _ple(un2, w_ple_gate[i], h2, p[i].reshape(t, -1), w_ple_proj[i],
                 g_ple_post[i][None], g_final[None], tm=ROW_TILE, tn=COL_TILE,
                 final_norm=(i == depth - 1), stage_rows=STAGE_ROWS)
    return h.reshape(batch, seq, d)
```

```python
import functools
import math

import numpy as np
import jax
import jax.numpy as jnp
from jax import lax
from jax.experimental import pallas as pl
from jax.experimental.pallas import tpu as pltpu

F32 = jnp.float32
BF16 = jnp.bfloat16

GRID_W = 64
ROPE_THETA = 10000.0
RMS_EPS = 1e-6

N_HEADS_A = 8
N_KV_A = 2
HEAD_DIM_A = 128
N_HEADS_B = 8
NOPE_DIM_B = 128
ROPE_DIM_B = 64
V_DIM_B = 128
QK_DIM_B = NOPE_DIM_B + ROPE_DIM_B
QK_PAD_B = 256

LANES = 128
BF16_ROWS = 16
V7X_VMEM_BYTES = 64 * 1024 * 1024
VMEM_SLACK_BYTES = 4 << 20
VMEM_RESERVED_BYTES = 6 << 20
LOG2E = math.log2(math.e)

ROW_TILE = 512
ROW_TILE_DOWN = 256
ROW_TILE_FFN_UP = 1024
COL_TILE = 512
STAGE_ROWS = 256
ATTN_Q_STREAM = 512
ATTN_KV_CHUNK = 512
ATTN_STREAMS = 8


def _vmem_limit(*block_bytes, scratch=0, temps=0):
    need = 2 * sum(block_bytes) + scratch + temps + VMEM_SLACK_BYTES
    return int(min(need, V7X_VMEM_BYTES - VMEM_RESERVED_BYTES))


def _nbytes(shape, dtype):
    return math.prod(shape) * jnp.dtype(dtype).itemsize


def _rms(x32, gain_row):
    ms = jnp.mean(x32 * x32, axis=-1, keepdims=True)
    return x32 * lax.rsqrt(ms + RMS_EPS) * gain_row


def _swap_chunks(y, chunk):
    lane = lax.broadcasted_iota(jnp.int32, y.shape, 1)
    first = (lane % (2 * chunk)) < chunk
    return jnp.where(first, pltpu.roll(y, LANES - chunk, 1), pltpu.roll(y, chunk, 1))


def _rope(y, cos, sin_signed, chunk):
    return y * cos + _swap_chunks(y, chunk) * sin_signed


def _rope_tables(seq, rot_dim):
    rows = seq // GRID_W
    row = np.repeat(np.arange(rows, dtype=np.float64), GRID_W)
    col = np.tile(np.arange(GRID_W, dtype=np.float64), rows)
    half = rot_dim // 2
    inv = 1.0 / np.power(ROPE_THETA, np.arange(0, half, 2, dtype=np.float64) / half)
    ar = row[:, None] * inv[None, :]
    ac = col[:, None] * inv[None, :]
    cos = np.concatenate([np.cos(ar), np.cos(ar), np.cos(ac), np.cos(ac)], axis=-1)
    sin = np.concatenate([-np.sin(ar), np.sin(ar), -np.sin(ac), np.sin(ac)], axis=-1)
    pad = LANES - rot_dim
    if pad:
        cos = np.concatenate([cos, np.ones((seq, pad))], axis=-1)
        sin = np.concatenate([sin, np.zeros((seq, pad))], axis=-1)
    return jnp.asarray(cos, F32), jnp.asarray(sin, F32)


def _stage_bf16(w_hbm, dst_ref, stage_ref, sem_ref, *, row0=0, n_rows=None, transpose=False,
                keep_rows_last=None):
    rows = stage_ref.shape[1]
    n_rows = w_hbm.shape[0] if n_rows is None else n_rows
    n_chunks = n_rows // rows

    def chunk_copy(r, slot):
        return pltpu.make_async_copy(w_hbm.at[pl.ds(row0 + r * rows, rows), :],
                                     stage_ref.at[slot], sem_ref.at[slot])

    chunk_copy(0, 0).start()
    for r in range(n_chunks):
        slot = r % 2
        if r + 1 < n_chunks:
            chunk_copy(r + 1, 1 - slot).start()
        chunk_copy(r, slot).wait()
        blk = stage_ref[slot]
        if keep_rows_last is not None and r == n_chunks - 1:
            row = lax.broadcasted_iota(jnp.int32, blk.shape, 0)
            blk = jnp.where(row < keep_rows_last, blk, 0.0)
        if transpose:
            dst_ref[:, r * rows:(r + 1) * rows] = blk.T.astype(BF16)
        else:
            dst_ref[r * rows:(r + 1) * rows, :] = blk.astype(BF16)


def _inproj_kernel(x_ref, g_ref, wt_hbm, wq_hbm, wkv_hbm, qn_ref, kn_ref, lg_ref,
                   cosa_ref, sina_ref, cosb_ref, sinb_ref,
                   qta_ref, ka_ref, vta_ref, qtb_ref, kb_ref, vtb_ref, u_ref,
                   wbf_ref, wqbf_ref, wkvbf_ref, stage_ref, sem_ref,
                   *, q_scale, qb_scale, n_q, n_kv, lora, kr_rows):
    hd = HEAD_DIM_A

    @pl.when(pl.program_id(0) == 0)
    def _():
        _stage_bf16(wt_hbm, wbf_ref, stage_ref, sem_ref, n_rows=wbf_ref.shape[1],
                    transpose=True, keep_rows_last=kr_rows)
        _stage_bf16(wq_hbm, wqbf_ref, stage_ref, sem_ref)
        _stage_bf16(wkv_hbm, wkvbf_ref, stage_ref, sem_ref)

    u = _rms(x_ref[...], g_ref[...]).astype(BF16)
    u_ref[...] = u
    cos = cosa_ref[...]
    sin = sina_ref[...]

    def proj(c0, width):
        return jnp.dot(u, wbf_ref[:, c0:c0 + width], preferred_element_type=F32)

    heads_per_dot = COL_TILE // hd
    for h0 in range(0, n_q, heads_per_dot):
        z = proj(h0 * hd, heads_per_dot * hd)
        for h in range(heads_per_dot):
            y = _rms(z[:, h * hd:(h + 1) * hd], qn_ref[...])
            qta_ref[(h0 + h) * hd:(h0 + h + 1) * hd, :] = (
                _rope(y, cos, sin, hd // 4) * q_scale).T.astype(BF16)

    z = proj(n_q * hd, 2 * n_kv * hd)
    for h in range(n_kv):
        y = _rms(z[:, h * hd:(h + 1) * hd], kn_ref[...])
        ka_ref[:, h * hd:(h + 1) * hd] = _rope(y, cos, sin, hd // 4).astype(BF16)
        vta_ref[h * hd:(h + 1) * hd, :] = z[:, (n_kv + h) * hd:(n_kv + h + 1) * hd].T.astype(BF16)

    lat0 = (n_q + 2 * n_kv) * hd
    cq = _rms(proj(lat0, lora), lg_ref[0]).astype(BF16)
    ckv = _rms(proj(lat0 + lora, lora), lg_ref[1]).astype(BF16)
    cosb = cosb_ref[...]
    sinb = sinb_ref[...]
    kpe = _rope(proj(lat0 + 2 * lora, LANES), cosb, sinb, ROPE_DIM_B // 4)
    lane = lax.broadcasted_iota(jnp.int32, kpe.shape, 1)
    kpe = jnp.where(lane == ROPE_DIM_B, 1.0, kpe).astype(BF16)
    zq = jnp.dot(cq, wqbf_ref[...], preferred_element_type=F32)
    zkv = jnp.dot(ckv, wkvbf_ref[...], preferred_element_type=F32)
    for h in range(N_HEADS_B):
        c0 = h * QK_PAD_B
        qtb_ref[c0:c0 + LANES, :] = (zq[:, c0:c0 + LANES] * qb_scale).T.astype(BF16)
        pe = _rope(zq[:, c0 + LANES:c0 + 2 * LANES], cosb, sinb, ROPE_DIM_B // 4)
        qtb_ref[c0 + LANES:c0 + 2 * LANES, :] = (pe * qb_scale).T.astype(BF16)
        kb_ref[:, c0:c0 + LANES] = zkv[:, c0:c0 + LANES].astype(BF16)
        kb_ref[:, c0 + LANES:c0 + 2 * LANES] = kpe
        vtb_ref[h * V_DIM_B:(h + 1) * V_DIM_B, :] = zkv[:, c0 + LANES:c0 + 2 * LANES].T.astype(BF16)


def _inproj(x2, g_mix, w_in_t, wq_pad, w_kv_b, q_norm, k_norm, lat_gains,
            cos_a, sin_a, cos_b, sin_b, *, seq, tm, kr_col0):
    t, d = x2.shape
    lora = lat_gains.shape[-1]
    n_qa = N_HEADS_A * HEAD_DIM_A
    n_kva = N_KV_A * HEAD_DIM_A
    n_qkv = n_qa + 2 * n_kva
    n_cols = n_qkv + 2 * lora + LANES
    nqb = wq_pad.shape[1]
    nvb = N_HEADS_B * V_DIM_B
    assert kr_col0 == n_qkv + 2 * lora and n_cols % LANES == 0
    assert (N_HEADS_A * HEAD_DIM_A) % COL_TILE == 0 and w_in_t.shape[0] >= n_cols
    assert t % tm == 0 and seq % tm == 0 and lora % LANES == 0
    assert wq_pad.shape == (lora, N_HEADS_B * QK_PAD_B) and w_kv_b.shape == (lora, nqb)
    assert nqb == d
    sblocks = seq // tm
    q_scale = (HEAD_DIM_A ** -0.5) * LOG2E
    qb_scale = (QK_DIM_B ** -0.5) * LOG2E
    return pl.pallas_call(
        functools.partial(_inproj_kernel, q_scale=q_scale, qb_scale=qb_scale, n_q=N_HEADS_A,
                          n_kv=N_KV_A, lora=lora, kr_rows=ROPE_DIM_B),
        grid=(t // tm,),
        in_specs=[
            pl.BlockSpec((tm, d), lambda i: (i, 0)),
            pl.BlockSpec((1, d), lambda i: (0, 0)),
            pl.BlockSpec(memory_space=pl.ANY),
            pl.BlockSpec(memory_space=pl.ANY),
            pl.BlockSpec(memory_space=pl.ANY),
            pl.BlockSpec((1, HEAD_DIM_A), lambda i: (0, 0)),
            pl.BlockSpec((1, HEAD_DIM_A), lambda i: (0, 0)),
            pl.BlockSpec((2, 1, lora), lambda i: (0, 0, 0)),
            pl.BlockSpec((tm, LANES), lambda i: (i % sblocks, 0)),
            pl.BlockSpec((tm, LANES), lambda i: (i % sblocks, 0)),
            pl.BlockSpec((tm, LANES), lambda i: (i % sblocks, 0)),
            pl.BlockSpec((tm, LANES), lambda i: (i % sblocks, 0)),
        ],
        out_specs=[
            pl.BlockSpec((n_qa, tm), lambda i: (0, i)),
            pl.BlockSpec((tm, n_kva), lambda i: (i, 0)),
            pl.BlockSpec((n_kva, tm), lambda i: (0, i)),
            pl.BlockSpec((nqb, tm), lambda i: (0, i)),
            pl.BlockSpec((tm, nqb), lambda i: (i, 0)),
            pl.BlockSpec((nvb, tm), lambda i: (0, i)),
            pl.BlockSpec((tm, d), lambda i: (i, 0)),
        ],
        out_shape=[
            jax.ShapeDtypeStruct((n_qa, t), BF16),
            jax.ShapeDtypeStruct((t, n_kva), BF16),
            jax.ShapeDtypeStruct((n_kva, t), BF16),
            jax.ShapeDtypeStruct((nqb, t), BF16),
            jax.ShapeDtypeStruct((t, nqb), BF16),
            jax.ShapeDtypeStruct((nvb, t), BF16),
            jax.ShapeDtypeStruct((t, d), BF16),
        ],
        scratch_shapes=[pltpu.VMEM((d, n_cols), BF16), pltpu.VMEM((lora, nqb), BF16),
                        pltpu.VMEM((lora, nqb), BF16), pltpu.VMEM((2, LANES, d), F32),
                        pltpu.SemaphoreType.DMA((2,))],
        compiler_params=pltpu.CompilerParams(
            dimension_semantics=("arbitrary",),
            vmem_limit_bytes=_vmem_limit(
                _nbytes((tm, d), F32), _nbytes((tm, n_qkv), BF16), 2 * _nbytes((tm, nqb), BF16),
                _nbytes((tm, nvb), BF16), _nbytes((tm, d), BF16), 4 * _nbytes((tm, LANES), F32),
                scratch=_nbytes((d, n_cols), BF16) + 2 * _nbytes((lora, nqb), BF16)
                + _nbytes((2, LANES, d), F32),
                temps=_nbytes((tm, d), F32) + 4 * _nbytes((tm, nqb), F32)),
        ),
        name="inproj",
    )(x2, g_mix, w_in_t, wq_pad, w_kv_b, q_norm, k_norm, lat_gains, cos_a, sin_a, cos_b, sin_b)


_L_SAFE = 2.0 ** -64


def _attn_kernel(q_ref, k_ref, v_ref, qg_ref, kg_ref, o_ref, vt_ref, kmax_ref,
                 qa_ref, s_ref, acc_ref, *maybe_kaug, groups, dk, dv, kc, sq, aug_lane,
                 unit_rms_scale):
    seq = k_ref.shape[0]
    tq = q_ref.shape[1]
    append_tile = aug_lane == dk
    kaug_ref = maybe_kaug[0] if append_tile else k_ref
    dka = kaug_ref.shape[1]

    @pl.when(pl.program_id(2) == 0)
    def _():
        if unit_rms_scale is None:
            kt = k_ref[...].astype(F32).T
            kmax2 = jnp.max(jnp.sum(kt * kt, axis=0, keepdims=True), axis=1, keepdims=True)
            kmax_ref[...] = jnp.broadcast_to(kmax2, kmax_ref.shape)
        if append_tile:
            kaug_ref[:, :dk] = k_ref[...]
            lane = lax.broadcasted_iota(jnp.int32, (seq, dka - dk), 1)
            kaug_ref[:, dk:] = jnp.where(lane == 0, 1.0, 0.0).astype(BF16)
        vt_ref[:dv, :] = v_ref[...]
        vt_ref[dv:, :] = jnp.ones((vt_ref.shape[0] - dv, seq), BF16)

    streams = [(r, g) for r in range(tq // sq) for g in range(groups)]

    n_str = len(streams)

    def stage_queries(shift_row_fn):
        for i, (r, g) in enumerate(streams):
            qt = q_ref[g * dk:(g + 1) * dk, r * sq:(r + 1) * sq].astype(F32)
            neg = -shift_row_fn(i, qt)
            if append_tile:
                row = lax.broadcasted_iota(jnp.int32, (dka - dk, sq), 0)
                qa = jnp.concatenate([qt, jnp.where(row == 0, neg, 0.0)], axis=0)
            else:
                row = lax.broadcasted_iota(jnp.int32, (dk, sq), 0)
                qa = jnp.where(row == aug_lane, neg, qt)
            qa_ref[i] = qa.astype(BF16)

    def run():
        n_chunks = seq // kc
        acc_ref[...] = jnp.zeros(acc_ref.shape, F32)

        def scores(i, c):
            start = c * kc if isinstance(c, int) else pl.multiple_of(c * kc, kc)
            s_ref[i % 2] = jnp.dot(kaug_ref[pl.ds(start, kc), :], qa_ref[i],
                                   preferred_element_type=F32)

        def accumulate(i, c):
            start = c * kc if isinstance(c, int) else pl.multiple_of(c * kc, kc)
            p = jnp.exp2(s_ref[i % 2]).astype(BF16)
            acc_ref[i] += jnp.dot(vt_ref[:, pl.ds(start, kc)], p, preferred_element_type=F32)

        def chunk(c, last):
            for i in range(n_str):
                if i + 1 < n_str:
                    scores(i + 1, c)
                elif not last:
                    scores(0, c + 1)
                accumulate(i, c)

        def body(c, carry):
            chunk(c, last=False)
            return carry

        scores(0, 0)
        lax.fori_loop(0, n_chunks - 1, body, 0, unroll=2)
        chunk(n_chunks - 1, last=True)
        return tuple((acc_ref[i, :dv, :], acc_ref[i, dv:dv + 1, :]) for i in range(n_str))

    def write_out(results):
        for (r, g), (acc, l) in zip(streams, results):
            o_ref[g * dv:(g + 1) * dv, r * sq:(r + 1) * sq] = (acc * (1.0 / l)).astype(BF16)

    if unit_rms_scale is None:
        kmax2 = kmax_ref[0:1, 0:1]
        stage_queries(
            lambda i, qt: jnp.sqrt(jnp.sum(qt * qt, axis=0, keepdims=True) * kmax2))
    else:
        bound = (jnp.max(jnp.abs(qg_ref[...]), axis=1, keepdims=True)
                 * jnp.max(jnp.abs(kg_ref[...]), axis=1, keepdims=True) * (dk * unit_rms_scale))
        stage_queries(lambda i, qt: jnp.broadcast_to(bound, (1, sq)))
    fast = run()
    write_out(fast)

    l_min = None
    for _, l in fast:
        cur = jnp.min(l)
        l_min = cur if l_min is None else jnp.minimum(l_min, cur)

    @pl.when(jnp.logical_not(l_min >= _L_SAFE))
    def _():
        stage_queries(lambda i, qt: jnp.zeros((1, sq), F32))

        def max_body(c, ms):
            start = pl.multiple_of(c * kc, kc)
            ks = kaug_ref[pl.ds(start, kc), :]
            return tuple(
                jnp.maximum(m, jnp.max(
                    jnp.dot(ks, qa_ref[i], preferred_element_type=F32), axis=0, keepdims=True))
                for i, m in enumerate(ms))

        ms = lax.fori_loop(0, seq // kc, max_body,
                           tuple(jnp.full((1, sq), -jnp.inf, F32) for _ in streams))
        stage_queries(lambda i, qt: ms[i])
        write_out(run())


def _attention(qt_arr, k_arr, vt_arr, *, batch, seq, n_kv, groups, dk, dv,
               tq, kc, sq, aug_lane, q_gain=None, k_gain=None, unit_rms_scale=None):
    t = batch * seq
    if unit_rms_scale is None:
        q_gain = k_gain = jnp.ones((1, dk), F32)
    nq = seq // tq
    dka = dk + LANES if aug_lane == dk else dk
    n_str = (tq // sq) * groups
    assert aug_lane <= dk and tq % sq == 0 and seq % kc == 0 and n_str % 2 == 0
    dvx = dv + BF16_ROWS
    scratch_shapes = [pltpu.VMEM((dvx, seq), BF16), pltpu.VMEM((8, LANES), F32),
                      pltpu.VMEM((n_str, dka, sq), BF16), pltpu.VMEM((2, kc, sq), F32),
                      pltpu.VMEM((n_str, dvx, sq), F32)]
    scratch_bytes = (_nbytes((dvx, seq), BF16) + _nbytes((n_str, dka, sq), BF16)
                     + _nbytes((2, kc, sq), F32) + _nbytes((n_str, dvx, sq), F32))
    if aug_lane == dk:
        scratch_shapes.append(pltpu.VMEM((seq, dka), BF16))
        scratch_bytes += _nbytes((seq, dka), BF16)
    return pl.pallas_call(
        functools.partial(_attn_kernel, groups=groups, dk=dk, dv=dv, kc=kc, sq=sq,
                          aug_lane=aug_lane, unit_rms_scale=unit_rms_scale),
        grid=(batch, n_kv, nq),
        in_specs=[
            pl.BlockSpec((groups * dk, tq), lambda b, h, i: (h, b * nq + i)),
            pl.BlockSpec((seq, dk), lambda b, h, i: (b, h)),
            pl.BlockSpec((dv, seq), lambda b, h, i: (h, b)),
            pl.BlockSpec((1, dk), lambda b, h, i: (0, 0)),
            pl.BlockSpec((1, dk), lambda b, h, i: (0, 0)),
        ],
        out_specs=pl.BlockSpec((groups * dv, tq), lambda b, h, i: (h, b * nq + i)),
        out_shape=jax.ShapeDtypeStruct((n_kv * groups * dv, t), BF16),
        scratch_shapes=scratch_shapes,
        compiler_params=pltpu.CompilerParams(
            dimension_semantics=("parallel", "parallel", "arbitrary"),
            vmem_limit_bytes=_vmem_limit(
                _nbytes((tq, groups * dk), BF16), _nbytes((seq, dk), BF16),
                _nbytes((seq, dv), BF16), _nbytes((tq, groups * dv), BF16),
                scratch=scratch_bytes,
                temps=2 * _nbytes((seq, dk), F32) + 4 * _nbytes((kc, sq), F32)),
        ),
        name="attention",
    )(qt_arr, k_arr, vt_arr, q_gain, k_gain)


def _merge_kernel(u_ref, oa_ref, ob_ref, wa_hbm, wb_hbm, wt_hbm, o_ref,
                  wabf_ref, wbbf_ref, wgbf_ref, stage_ref, sem_ref, *, tn, gate_row0):
    @pl.when(pl.program_id(0) == 0)
    def _():
        _stage_bf16(wa_hbm, wabf_ref, stage_ref, sem_ref)
        _stage_bf16(wb_hbm, wbbf_ref, stage_ref, sem_ref)
        _stage_bf16(wt_hbm, wgbf_ref, stage_ref, sem_ref, row0=gate_row0,
                    n_rows=wgbf_ref.shape[1], transpose=True)

    u = u_ref[...]
    d = o_ref.shape[1]
    tn_dims = (((0,), (0,)), ((), ()))
    for jj in range(d // tn):
        cols = slice(jj * tn, (jj + 1) * tn)
        gcols = slice(d + jj * tn, d + (jj + 1) * tn)
        ga = jnp.dot(u, wgbf_ref[:, cols], preferred_element_type=F32)
        gb = jnp.dot(u, wgbf_ref[:, gcols], preferred_element_type=F32)
        ya = lax.dot_general(oa_ref[...], wabf_ref[:, cols], tn_dims, preferred_element_type=F32)
        yb = lax.dot_general(ob_ref[...], wbbf_ref[:, cols], tn_dims, preferred_element_type=F32)
        o_ref[:, cols] = (jax.nn.sigmoid(ga) * ya + jax.nn.sigmoid(gb) * yb).astype(BF16)


def _merge(u, oa_t, ob_t, w_a_out, w_b_out, w_in_t, *, tm, tn, gate_row0, stage_rows):
    t, d = u.shape
    ka = oa_t.shape[0]
    kb = ob_t.shape[0]
    assert ka % stage_rows == 0 and kb % stage_rows == 0 and (2 * d) % stage_rows == 0
    assert w_a_out.shape[1] == d and w_in_t.shape[1] == d
    return pl.pallas_call(
        functools.partial(_merge_kernel, tn=tn, gate_row0=gate_row0),
        grid=(t // tm,),
        in_specs=[
            pl.BlockSpec((tm, d), lambda i: (i, 0)),
            pl.BlockSpec((ka, tm), lambda i: (0, i)),
            pl.BlockSpec((kb, tm), lambda i: (0, i)),
            pl.BlockSpec(memory_space=pl.ANY),
            pl.BlockSpec(memory_space=pl.ANY),
            pl.BlockSpec(memory_space=pl.ANY),
        ],
        out_specs=pl.BlockSpec((tm, d), lambda i: (i, 0)),
        out_shape=jax.ShapeDtypeStruct((t, d), BF16),
        scratch_shapes=[pltpu.VMEM((ka, d), BF16), pltpu.VMEM((kb, d), BF16),
                        pltpu.VMEM((d, 2 * d), BF16), pltpu.VMEM((2, stage_rows, d), F32),
                        pltpu.SemaphoreType.DMA((2,))],
        compiler_params=pltpu.CompilerParams(
            dimension_semantics=("arbitrary",),
            vmem_limit_bytes=_vmem_limit(
                _nbytes((tm, d), BF16), _nbytes((tm, ka), BF16), _nbytes((tm, kb), BF16),
                _nbytes((tm, d), BF16),
                scratch=_nbytes((ka + kb + 2 * d, d), BF16) + _nbytes((2, stage_rows, d), F32),
                temps=8 * _nbytes((tm, tn), F32)),
        ),
        name="merge",
    )(u, oa_t, ob_t, w_a_out, w_b_out, w_in_t)


def _resid_rows_kernel(lhs_ref, w_hbm, resid_ref, g_ref, h_ref, un_ref,
                       wbf_ref, stage_ref, sem_ref, *, tn):
    @pl.when(pl.program_id(0) == 0)
    def _():
        _stage_bf16(w_hbm, wbf_ref, stage_ref, sem_ref)

    lhs = lhs_ref[...]
    d = h_ref.shape[1]
    ss = jnp.zeros((h_ref.shape[0], 1), F32)
    for jj in range(d // tn):
        cols = slice(jj * tn, (jj + 1) * tn)
        val = resid_ref[:, cols] + jnp.dot(lhs, wbf_ref[:, cols], preferred_element_type=F32)
        h_ref[:, cols] = val
        ss = ss + jnp.sum(val * val, axis=-1, keepdims=True)
    inv = lax.rsqrt(ss * (1.0 / d) + RMS_EPS)
    un_ref[...] = (h_ref[...] * inv * g_ref[...]).astype(BF16)


def _resid_rows(lhs, w, resid, gain, *, tm, tn, stage_rows):
    t, k = lhs.shape
    d = w.shape[1]
    assert k % stage_rows == 0 and d % tn == 0
    return pl.pallas_call(
        functools.partial(_resid_rows_kernel, tn=tn),
        grid=(t // tm,),
        in_specs=[
            pl.BlockSpec((tm, k), lambda i: (i, 0)),
            pl.BlockSpec(memory_space=pl.ANY),
            pl.BlockSpec((tm, d), lambda i: (i, 0)),
            pl.BlockSpec((1, d), lambda i: (0, 0)),
        ],
        out_specs=[
            pl.BlockSpec((tm, d), lambda i: (i, 0)),
            pl.BlockSpec((tm, d), lambda i: (i, 0)),
        ],
        out_shape=[
            jax.ShapeDtypeStruct((t, d), F32),
            jax.ShapeDtypeStruct((t, d), BF16),
        ],
        scratch_shapes=[pltpu.VMEM((k, d), BF16), pltpu.VMEM((2, stage_rows, d), F32),
                        pltpu.SemaphoreType.DMA((2,))],
        compiler_params=pltpu.CompilerParams(
            dimension_semantics=("arbitrary",),
            vmem_limit_bytes=_vmem_limit(
                _nbytes((tm, k), BF16), 2 * _nbytes((tm, d), F32), _nbytes((tm, d), BF16),
                scratch=_nbytes((k, d), BF16) + _nbytes((2, stage_rows, d), F32),
                temps=4 * _nbytes((tm, tn), F32) + _nbytes((tm, d), F32)),
        ),
        name="resid_rows",
    )(lhs, w, resid, gain)


def _ffn_up_kernel(un_ref, wg_ref, wu_ref, o_ref):
    un = un_ref[...]
    g = jnp.dot(un, wg_ref[...].astype(BF16), preferred_element_type=F32)
    u = jnp.dot(un, wu_ref[...].astype(BF16), preferred_element_type=F32)
    o_ref[...] = (jax.nn.silu(g) * u).astype(BF16)


def _ffn_up(un, w_gate_up, *, tm, tn):
    t, d = un.shape
    d_ff = w_gate_up.shape[1] // 2
    nj = d_ff // tn
    return pl.pallas_call(
        _ffn_up_kernel,
        grid=(t // tm, nj),
        in_specs=[
            pl.BlockSpec((tm, d), lambda i, j: (i, 0)),
            pl.BlockSpec((d, tn), lambda i, j: (0, j)),
            pl.BlockSpec((d, tn), lambda i, j: (0, nj + j)),
        ],
        out_specs=pl.BlockSpec((tm, tn), lambda i, j: (i, j)),
        out_shape=jax.ShapeDtypeStruct((t, d_ff), BF16),
        compiler_params=pltpu.CompilerParams(
            dimension_semantics=("parallel", "arbitrary"),
            vmem_limit_bytes=_vmem_limit(
                _nbytes((tm, d), BF16), 2 * _nbytes((d, tn), F32), _nbytes((tm, tn), BF16),
                temps=2 * _nbytes((d, tn), BF16) + 4 * _nbytes((tm, tn), F32)),
        ),
        name="ffn_up",
    )(un, w_gate_up, w_gate_up)


def _ple_kernel(un_ref, w_hbm, h_ref, p_ref, wp_hbm, gpost_ref, gfin_ref,
                o_ref, wbf_ref, wpbf_ref, stage_ref, sem_ref, *, tn, final_norm):
    @pl.when(pl.program_id(0) == 0)
    def _():
        _stage_bf16(w_hbm, wbf_ref, stage_ref, sem_ref)
        _stage_bf16(wp_hbm, wpbf_ref, stage_ref, sem_ref)

    un = un_ref[...]
    d = o_ref.shape[1]
    e = _rms(jnp.dot(p_ref[...].astype(BF16), wpbf_ref[...], preferred_element_type=F32),
             gpost_ref[...])
    ss = jnp.zeros((o_ref.shape[0], 1), F32)
    for jj in range(d // tn):
        cols = slice(jj * tn, (jj + 1) * tn)
        gate = jax.nn.sigmoid(jnp.dot(un, wbf_ref[:, cols], preferred_element_type=F32))
        val = h_ref[:, cols] + gate * e[:, cols]
        o_ref[:, cols] = val
        ss = ss + jnp.sum(val * val, axis=-1, keepdims=True)
    if final_norm:
        inv = lax.rsqrt(ss * (1.0 / d) + RMS_EPS)
        o_ref[...] = o_ref[...] * inv * gfin_ref[...]


def _ple(un, w_gate, h, p2, w_proj, g_post, g_final, *, tm, tn, final_norm, stage_rows):
    t, d = un.shape
    pd = p2.shape[1]
    assert d % stage_rows == 0 and pd % stage_rows == 0
    return pl.pallas_call(
        functools.partial(_ple_kernel, tn=tn, final_norm=final_norm),
        grid=(t // tm,),
        in_specs=[
            pl.BlockSpec((tm, d), lambda i: (i, 0)),
            pl.BlockSpec(memory_space=pl.ANY),
            pl.BlockSpec((tm, d), lambda i: (i, 0)),
            pl.BlockSpec((tm, pd), lambda i: (i, 0)),
            pl.BlockSpec(memory_space=pl.ANY),
            pl.BlockSpec((1, d), lambda i: (0, 0)),
            pl.BlockSpec((1, d), lambda i: (0, 0)),
        ],
        out_specs=pl.BlockSpec((tm, d), lambda i: (i, 0)),
        out_shape=jax.ShapeDtypeStruct((t, d), F32),
        scratch_shapes=[pltpu.VMEM((d, d), BF16), pltpu.VMEM((pd, d), BF16),
                        pltpu.VMEM((2, stage_rows, d), F32), pltpu.SemaphoreType.DMA((2,))],
        compiler_params=pltpu.CompilerParams(
            dimension_semantics=("arbitrary",),
            vmem_limit_bytes=_vmem_limit(
                _nbytes((tm, d), BF16), 2 * _nbytes((tm, d), F32), _nbytes((tm, pd), F32),
                scratch=_nbytes((d + pd, d), BF16) + _nbytes((2, stage_rows, d), F32),
                temps=2 * _nbytes((tm, d), F32) + 4 * _nbytes((tm, tn), F32)),
        ),
        name="ple_final",
    )(un, w_gate, h, p2, w_proj, g_post, g_final)


def kernel(x, p, g_mix, w_in, q_norm_a, k_norm_a, q_a_norm, w_q_b, kv_a_norm, w_kv_b,
           w_a_out, w_b_out, w_o, g_ffn, w_gate_up, w_down, g_ple, w_ple_gate,
           w_ple_proj, g_ple_post, g_final):
    batch, seq, d = x.shape
    depth = w_in.shape[0]
    t = batch * seq
    q_lora = q_a_norm.shape[-1]
    kv_lora = kv_a_norm.shape[-1]
    assert q_lora == kv_lora
    n_qa = N_HEADS_A * HEAD_DIM_A
    n_kva = N_KV_A * HEAD_DIM_A
    lat0 = n_qa + 2 * n_kva
    kr0 = lat0 + q_lora + kv_lora
    gate0 = kr0 + ROPE_DIM_B

    cos_a, sin_a = _rope_tables(seq, HEAD_DIM_A)
    cos_b, sin_b = _rope_tables(seq, ROPE_DIM_B)

    h = x.reshape(t, d)
    for i in range(depth):
        wi_t = jnp.swapaxes(w_in[i], 0, 1)
        wq_pad = jnp.pad(
            w_q_b[i].reshape(q_lora, N_HEADS_B, QK_DIM_B),
            ((0, 0), (0, 0), (0, QK_PAD_B - QK_DIM_B))).reshape(q_lora, N_HEADS_B * QK_PAD_B)
        lat_gains = jnp.stack([q_a_norm[i], kv_a_norm[i]])[:, None, :]
        qt_a, k_a, vt_a, qt_b, k_b, vt_b, u = _inproj(
            h, g_mix[i][None], wi_t, wq_pad, w_kv_b[i], q_norm_a[i][None], k_norm_a[i][None],
            lat_gains, cos_a, sin_a, cos_b, sin_b, seq=seq, tm=ROW_TILE, kr_col0=kr0)

        groups_a = N_HEADS_A // N_KV_A
        oa_t = _attention(qt_a, k_a, vt_a, batch=batch, seq=seq, n_kv=N_KV_A,
                          groups=groups_a, dk=HEAD_DIM_A, dv=HEAD_DIM_A,
                          tq=ATTN_STREAMS * ATTN_Q_STREAM // groups_a, kc=ATTN_KV_CHUNK,
                          sq=ATTN_Q_STREAM, aug_lane=HEAD_DIM_A, q_gain=q_norm_a[i][None],
                          k_gain=k_norm_a[i][None], unit_rms_scale=(HEAD_DIM_A ** -0.5) * LOG2E)
        ob_t = _attention(qt_b, k_b, vt_b, batch=batch, seq=seq, n_kv=N_HEADS_B,
                          groups=1, dk=QK_PAD_B, dv=V_DIM_B,
                          tq=ATTN_STREAMS * ATTN_Q_STREAM, kc=ATTN_KV_CHUNK, sq=ATTN_Q_STREAM,
                          aug_lane=QK_DIM_B)

        merged = _merge(u, oa_t, ob_t, w_a_out[i], w_b_out[i], wi_t, tm=ROW_TILE, tn=COL_TILE,
                        gate_row0=gate0, stage_rows=STAGE_ROWS)
        h1, un1 = _resid_rows(merged, w_o[i], h, g_ffn[i][None], tm=ROW_TILE, tn=COL_TILE,
                              stage_rows=STAGE_ROWS)
        act = _ffn_up(un1, w_gate_up[i], tm=ROW_TILE_FFN_UP, tn=COL_TILE)
        h2, un2 = _resid_rows(act, w_down[i], h1, g_ple[i][None], tm=ROW_TILE_DOWN, tn=COL_TILE,
                              stage_rows=STAGE_ROWS)
        h = _ple(un2, w_ple_gate[i], h2, p[i].reshape(t, -1), w_ple_proj[i],
                 g_ple_post[i][None], g_final[None], tm=ROW_TILE, tn=COL_TILE,
                 final_norm=(i == depth - 1), stage_rows=STAGE_ROWS)
    return h.reshape(batch, seq, d)
```

```python
import functools
import math

import numpy as np
import jax
import jax.numpy as jnp
from jax import lax
from jax.experimental import pallas as pl
from jax.experimental.pallas import tpu as pltpu

F32 = jnp.float32
BF16 = jnp.bfloat16

GRID_W = 64
ROPE_THETA = 10000.0
RMS_EPS = 1e-6

N_HEADS_A = 8
N_KV_A = 2
HEAD_DIM_A = 128
N_HEADS_B = 8
NOPE_DIM_B = 128
ROPE_DIM_B = 64
V_DIM_B = 128
QK_DIM_B = NOPE_DIM_B + ROPE_DIM_B
QK_PAD_B = 256

LANES = 128
BF16_ROWS = 16
V7X_VMEM_BYTES = 64 * 1024 * 1024
VMEM_SLACK_BYTES = 4 << 20
VMEM_RESERVED_BYTES = 6 << 20
LOG2E = math.log2(math.e)

ROW_TILE = 512
ROW_TILE_DOWN = 256
ROW_TILE_FFN_UP = 1024
COL_TILE = 512
STAGE_ROWS = 256
ATTN_Q_STREAM = 512
ATTN_KV_CHUNK = 512
ATTN_STREAMS = 8


def _vmem_limit(*block_bytes, scratch=0, temps=0):
    need = 2 * sum(block_bytes) + scratch + temps + VMEM_SLACK_BYTES
    return int(min(need, V7X_VMEM_BYTES - VMEM_RESERVED_BYTES))


def _nbytes(shape, dtype):
    return math.prod(shape) * jnp.dtype(dtype).itemsize


def _rms(x32, gain_row):
    ms = jnp.mean(x32 * x32, axis=-1, keepdims=True)
    return x32 * lax.rsqrt(ms + RMS_EPS) * gain_row


def _swap_chunks(y, chunk):
    lane = lax.broadcasted_iota(jnp.int32, y.shape, 1)
    first = (lane % (2 * chunk)) < chunk
    return jnp.where(first, pltpu.roll(y, LANES - chunk, 1), pltpu.roll(y, chunk, 1))


def _rope(y, cos, sin_signed, chunk):
    return y * cos + _swap_chunks(y, chunk) * sin_signed


def _rope_tables(seq, rot_dim):
    rows = seq // GRID_W
    row = np.repeat(np.arange(rows, dtype=np.float64), GRID_W)
    col = np.tile(np.arange(GRID_W, dtype=np.float64), rows)
    half = rot_dim // 2
    inv = 1.0 / np.power(ROPE_THETA, np.arange(0, half, 2, dtype=np.float64) / half)
    ar = row[:, None] * inv[None, :]
    ac = col[:, None] * inv[None, :]
    cos = np.concatenate([np.cos(ar), np.cos(ar), np.cos(ac), np.cos(ac)], axis=-1)
    sin = np.concatenate([-np.sin(ar), np.sin(ar), -np.sin(ac), np.sin(ac)], axis=-1)
    pad = LANES - rot_dim
    if pad:
        cos = np.concatenate([cos, np.ones((seq, pad))], axis=-1)
        sin = np.concatenate([sin, np.zeros((seq, pad))], axis=-1)
    return jnp.asarray(cos, F32), jnp.asarray(sin, F32)


def _stage_bf16(w_hbm, dst_ref, stage_ref, sem_ref, *, row0=0, n_rows=None, transpose=False,
                keep_rows_last=None):
    rows = stage_ref.shape[1]
    n_rows = w_hbm.shape[0] if n_rows is None else n_rows
    n_chunks = n_rows // rows

    def chunk_copy(r, slot):
        return pltpu.make_async_copy(w_hbm.at[pl.ds(row0 + r * rows, rows), :],
                                     stage_ref.at[slot], sem_ref.at[slot])

    chunk_copy(0, 0).start()
    for r in range(n_chunks):
        slot = r % 2
        if r + 1 < n_chunks:
            chunk_copy(r + 1, 1 - slot).start()
        chunk_copy(r, slot).wait()
        blk = stage_ref[slot]
        if keep_rows_last is not None and r == n_chunks - 1:
            row = lax.broadcasted_iota(jnp.int32, blk.shape, 0)
            blk = jnp.where(row < keep_rows_last, blk, 0.0)
        if transpose:
            dst_ref[:, r * rows:(r + 1) * rows] = blk.T.astype(BF16)
        else:
            dst_ref[r * rows:(r + 1) * rows, :] = blk.astype(BF16)


def _inproj_kernel(x_ref, g_ref, wt_hbm, wq_hbm, wkv_hbm, qn_ref, kn_ref, lg_ref,
                   cosa_ref, sina_ref, cosb_ref, sinb_ref,
                   qta_ref, ka_ref, vta_ref, qtb_ref, kb_ref, vtb_ref, u_ref,
                   wbf_ref, wqbf_ref, wkvbf_ref, stage_ref, sem_ref,
                   *, q_scale, qb_scale, n_q, n_kv, lora, kr_rows):
    hd = HEAD_DIM_A

    @pl.when(pl.program_id(0) == 0)
    def _():
        _stage_bf16(wt_hbm, wbf_ref, stage_ref, sem_ref, n_rows=wbf_ref.shape[1],
                    transpose=True, keep_rows_last=kr_rows)
        _stage_bf16(wq_hbm, wqbf_ref, stage_ref, sem_ref)
        _stage_bf16(wkv_hbm, wkvbf_ref, stage_ref, sem_ref)

    u = _rms(x_ref[...], g_ref[...]).astype(BF16)
    u_ref[...] = u
    cos = cosa_ref[...]
    sin = sina_ref[...]

    def proj(c0, width):
        return jnp.dot(u, wbf_ref[:, c0:c0 + width], preferred_element_type=F32)

    heads_per_dot = COL_TILE // hd
    for h0 in range(0, n_q, heads_per_dot):
        z = proj(h0 * hd, heads_per_dot * hd)
        for h in range(heads_per_dot):
            y = _rms(z[:, h * hd:(h + 1) * hd], qn_ref[...])
            qta_ref[(h0 + h) * hd:(h0 + h + 1) * hd, :] = (
                _rope(y, cos, sin, hd // 4) * q_scale).astype(BF16).T

    z = proj(n_q * hd, 2 * n_kv * hd)
    for h in range(n_kv):
        y = _rms(z[:, h * hd:(h + 1) * hd], kn_ref[...])
        ka_ref[:, h * hd:(h + 1) * hd] = _rope(y, cos, sin, hd // 4).astype(BF16)
        vta_ref[h * hd:(h + 1) * hd, :] = z[:, (n_kv + h) * hd:(n_kv + h + 1) * hd].astype(BF16).T

    lat0 = (n_q + 2 * n_kv) * hd
    cq = _rms(proj(lat0, lora), lg_ref[0]).astype(BF16)
    ckv = _rms(proj(lat0 + lora, lora), lg_ref[1]).astype(BF16)
    cosb = cosb_ref[...]
    sinb = sinb_ref[...]
    kpe = _rope(proj(lat0 + 2 * lora, LANES), cosb, sinb, ROPE_DIM_B // 4)
    lane = lax.broadcasted_iota(jnp.int32, kpe.shape, 1)
    kpe = jnp.where(lane == ROPE_DIM_B, 1.0, kpe).astype(BF16)
    zq = jnp.dot(cq, wqbf_ref[...], preferred_element_type=F32)
    zkv = jnp.dot(ckv, wkvbf_ref[...], preferred_element_type=F32)
    for h in range(N_HEADS_B):
        c0 = h * QK_PAD_B
        qtb_ref[c0:c0 + LANES, :] = (zq[:, c0:c0 + LANES] * qb_scale).astype(BF16).T
        pe = _rope(zq[:, c0 + LANES:c0 + 2 * LANES], cosb, sinb, ROPE_DIM_B // 4)
        qtb_ref[c0 + LANES:c0 + 2 * LANES, :] = (pe * qb_scale).astype(BF16).T
        kb_ref[:, c0:c0 + LANES] = zkv[:, c0:c0 + LANES].astype(BF16)
        kb_ref[:, c0 + LANES:c0 + 2 * LANES] = kpe
        vtb_ref[h * V_DIM_B:(h + 1) * V_DIM_B, :] = zkv[:, c0 + LANES:c0 + 2 * LANES].astype(BF16).T


def _inproj(x2, g_mix, w_in_t, wq_pad, w_kv_b, q_norm, k_norm, lat_gains,
            cos_a, sin_a, cos_b, sin_b, *, seq, tm, kr_col0):
    t, d = x2.shape
    lora = lat_gains.shape[-1]
    n_qa = N_HEADS_A * HEAD_DIM_A
    n_kva = N_KV_A * HEAD_DIM_A
    n_qkv = n_qa + 2 * n_kva
    n_cols = n_qkv + 2 * lora + LANES
    nqb = wq_pad.shape[1]
    nvb = N_HEADS_B * V_DIM_B
    assert kr_col0 == n_qkv + 2 * lora and n_cols % LANES == 0
    assert (N_HEADS_A * HEAD_DIM_A) % COL_TILE == 0 and w_in_t.shape[0] >= n_cols
    assert t % tm == 0 and seq % tm == 0 and lora % LANES == 0
    assert wq_pad.shape == (lora, N_HEADS_B * QK_PAD_B) and w_kv_b.shape == (lora, nqb)
    assert nqb == d
    sblocks = seq // tm
    q_scale = (HEAD_DIM_A ** -0.5) * LOG2E
    qb_scale = (QK_DIM_B ** -0.5) * LOG2E
    return pl.pallas_call(
        functools.partial(_inproj_kernel, q_scale=q_scale, qb_scale=qb_scale, n_q=N_HEADS_A,
                          n_kv=N_KV_A, lora=lora, kr_rows=ROPE_DIM_B),
        grid=(t // tm,),
        in_specs=[
            pl.BlockSpec((tm, d), lambda i: (i, 0)),
            pl.BlockSpec((1, d), lambda i: (0, 0)),
            pl.BlockSpec(memory_space=pl.ANY),
            pl.BlockSpec(memory_space=pl.ANY),
            pl.BlockSpec(memory_space=pl.ANY),
            pl.BlockSpec((1, HEAD_DIM_A), lambda i: (0, 0)),
            pl.BlockSpec((1, HEAD_DIM_A), lambda i: (0, 0)),
            pl.BlockSpec((2, 1, lora), lambda i: (0, 0, 0)),
            pl.BlockSpec((tm, LANES), lambda i: (i % sblocks, 0)),
            pl.BlockSpec((tm, LANES), lambda i: (i % sblocks, 0)),
            pl.BlockSpec((tm, LANES), lambda i: (i % sblocks, 0)),
            pl.BlockSpec((tm, LANES), lambda i: (i % sblocks, 0)),
        ],
        out_specs=[
            pl.BlockSpec((n_qa, tm), lambda i: (0, i)),
            pl.BlockSpec((tm, n_kva), lambda i: (i, 0)),
            pl.BlockSpec((n_kva, tm), lambda i: (0, i)),
            pl.BlockSpec((nqb, tm), lambda i: (0, i)),
            pl.BlockSpec((tm, nqb), lambda i: (i, 0)),
            pl.BlockSpec((nvb, tm), lambda i: (0, i)),
            pl.BlockSpec((tm, d), lambda i: (i, 0)),
        ],
        out_shape=[
            jax.ShapeDtypeStruct((n_qa, t), BF16),
            jax.ShapeDtypeStruct((t, n_kva), BF16),
            jax.ShapeDtypeStruct((n_kva, t), BF16),
            jax.ShapeDtypeStruct((nqb, t), BF16),
            jax.ShapeDtypeStruct((t, nqb), BF16),
            jax.ShapeDtypeStruct((nvb, t), BF16),
            jax.ShapeDtypeStruct((t, d), BF16),
        ],
        scratch_shapes=[pltpu.VMEM((d, n_cols), BF16), pltpu.VMEM((lora, nqb), BF16),
                        pltpu.VMEM((lora, nqb), BF16), pltpu.VMEM((2, LANES, d), F32),
                        pltpu.SemaphoreType.DMA((2,))],
        compiler_params=pltpu.CompilerParams(
            dimension_semantics=("arbitrary",),
            vmem_limit_bytes=_vmem_limit(
                _nbytes((tm, d), F32), _nbytes((tm, n_qkv), BF16), 2 * _nbytes((tm, nqb), BF16),
                _nbytes((tm, nvb), BF16), _nbytes((tm, d), BF16), 4 * _nbytes((tm, LANES), F32),
                scratch=_nbytes((d, n_cols), BF16) + 2 * _nbytes((lora, nqb), BF16)
                + _nbytes((2, LANES, d), F32),
                temps=_nbytes((tm, d), F32) + 4 * _nbytes((tm, nqb), F32)),
        ),
        name="inproj",
    )(x2, g_mix, w_in_t, wq_pad, w_kv_b, q_norm, k_norm, lat_gains, cos_a, sin_a, cos_b, sin_b)


_L_SAFE = 2.0 ** -64


def _attn_kernel(q_ref, k_ref, v_ref, qg_ref, kg_ref, o_ref, vt_ref, kmax_ref,
                 qa_ref, s_ref, acc_ref, *maybe_kaug, groups, dk, dv, kc, sq, aug_lane,
                 unit_rms_scale):
    seq = k_ref.shape[0]
    tq = q_ref.shape[1]
    append_tile = aug_lane == dk
    kaug_ref = maybe_kaug[0] if append_tile else k_ref
    dka = kaug_ref.shape[1]

    @pl.when(pl.program_id(2) == 0)
    def _():
        if unit_rms_scale is None:
            kt = k_ref[...].astype(F32).T
            kmax2 = jnp.max(jnp.sum(kt * kt, axis=0, keepdims=True), axis=1, keepdims=True)
            kmax_ref[...] = jnp.broadcast_to(kmax2, kmax_ref.shape)
        if append_tile:
            kaug_ref[:, :dk] = k_ref[...]
            lane = lax.broadcasted_iota(jnp.int32, (seq, dka - dk), 1)
            kaug_ref[:, dk:] = jnp.where(lane == 0, 1.0, 0.0).astype(BF16)
        vt_ref[:dv, :] = v_ref[...]
        vt_ref[dv:, :] = jnp.ones((vt_ref.shape[0] - dv, seq), BF16)

    streams = [(r, g) for r in range(tq // sq) for g in range(groups)]

    n_str = len(streams)

    def stage_queries(shift_row_fn):
        for i, (r, g) in enumerate(streams):
            qt = q_ref[g * dk:(g + 1) * dk, r * sq:(r + 1) * sq].astype(F32)
            neg = -shift_row_fn(i, qt)
            if append_tile:
                row = lax.broadcasted_iota(jnp.int32, (dka - dk, sq), 0)
                qa = jnp.concatenate([qt, jnp.where(row == 0, neg, 0.0)], axis=0)
            else:
                row = lax.broadcasted_iota(jnp.int32, (dk, sq), 0)
                qa = jnp.where(row == aug_lane, neg, qt)
            qa_ref[i] = qa.astype(BF16)

    def run():
        n_chunks = seq // kc
        acc_ref[...] = jnp.zeros(acc_ref.shape, F32)

        def scores(i, c):
            start = c * kc if isinstance(c, int) else pl.multiple_of(c * kc, kc)
            s_ref[i % 2] = jnp.dot(kaug_ref[pl.ds(start, kc), :], qa_ref[i],
                                   preferred_element_type=F32)

        def accumulate(i, c):
            start = c * kc if isinstance(c, int) else pl.multiple_of(c * kc, kc)
            p = jnp.exp2(s_ref[i % 2]).astype(BF16)
            acc_ref[i] += jnp.dot(vt_ref[:, pl.ds(start, kc)], p, preferred_element_type=F32)

        def chunk(c, last):
            for i in range(n_str):
                if i + 1 < n_str:
                    scores(i + 1, c)
                elif not last:
                    scores(0, c + 1)
                accumulate(i, c)

        def body(c, carry):
            chunk(c, last=False)
            return carry

        scores(0, 0)
        lax.fori_loop(0, n_chunks - 1, body, 0, unroll=2)
        chunk(n_chunks - 1, last=True)
        return tuple((acc_ref[i, :dv, :], acc_ref[i, dv:dv + 1, :]) for i in range(n_str))

    def write_out(results):
        for (r, g), (acc, l) in zip(streams, results):
            o_ref[g * dv:(g + 1) * dv, r * sq:(r + 1) * sq] = (acc * (1.0 / l)).astype(BF16)

    if unit_rms_scale is None:
        kmax2 = kmax_ref[0:1, 0:1]
        stage_queries(
            lambda i, qt: jnp.sqrt(jnp.sum(qt * qt, axis=0, keepdims=True) * kmax2))
    else:
        bound = (jnp.max(jnp.abs(qg_ref[...]), axis=1, keepdims=True)
                 * jnp.max(jnp.abs(kg_ref[...]), axis=1, keepdims=True) * (dk * unit_rms_scale))
        stage_queries(lambda i, qt: jnp.broadcast_to(bound, (1, sq)))
    fast = run()
    write_out(fast)

    l_min = None
    for _, l in fast:
        cur = jnp.min(l)
        l_min = cur if l_min is None else jnp.minimum(l_min, cur)

    @pl.when(jnp.logical_not(l_min >= _L_SAFE))
    def _():
        stage_queries(lambda i, qt: jnp.zeros((1, sq), F32))

        def max_body(c, ms):
            start = pl.multiple_of(c * kc, kc)
            ks = kaug_ref[pl.ds(start, kc), :]
            return tuple(
                jnp.maximum(m, jnp.max(
                    jnp.dot(ks, qa_ref[i], preferred_element_type=F32), axis=0, keepdims=True))
                for i, m in enumerate(ms))

        ms = lax.fori_loop(0, seq // kc, max_body,
                           tuple(jnp.full((1, sq), -jnp.inf, F32) for _ in streams))
        stage_queries(lambda i, qt: ms[i])
        write_out(run())


def _attention(qt_arr, k_arr, vt_arr, *, batch, seq, n_kv, groups, dk, dv,
               tq, kc, sq, aug_lane, q_gain=None, k_gain=None, unit_rms_scale=None):
    t = batch * seq
    if unit_rms_scale is None:
        q_gain = k_gain = jnp.ones((1, dk), F32)
    nq = seq // tq
    dka = dk + LANES if aug_lane == dk else dk
    n_str = (tq // sq) * groups
    assert aug_lane <= dk and tq % sq == 0 and seq % kc == 0 and n_str % 2 == 0
    dvx = dv + BF16_ROWS
    scratch_shapes = [pltpu.VMEM((dvx, seq), BF16), pltpu.VMEM((8, LANES), F32),
                      pltpu.VMEM((n_str, dka, sq), BF16), pltpu.VMEM((2, kc, sq), F32),
                      pltpu.VMEM((n_str, dvx, sq), F32)]
    scratch_bytes = (_nbytes((dvx, seq), BF16) + _nbytes((n_str, dka, sq), BF16)
                     + _nbytes((2, kc, sq), F32) + _nbytes((n_str, dvx, sq), F32))
    if aug_lane == dk:
        scratch_shapes.append(pltpu.VMEM((seq, dka), BF16))
        scratch_bytes += _nbytes((seq, dka), BF16)
    return pl.pallas_call(
        functools.partial(_attn_kernel, groups=groups, dk=dk, dv=dv, kc=kc, sq=sq,
                          aug_lane=aug_lane, unit_rms_scale=unit_rms_scale),
        grid=(batch, n_kv, nq),
        in_specs=[
            pl.BlockSpec((groups * dk, tq), lambda b, h, i: (h, b * nq + i)),
            pl.BlockSpec((seq, dk), lambda b, h, i: (b, h)),
            pl.BlockSpec((dv, seq), lambda b, h, i: (h, b)),
            pl.BlockSpec((1, dk), lambda b, h, i: (0, 0)),
            pl.BlockSpec((1, dk), lambda b, h, i: (0, 0)),
        ],
        out_specs=pl.BlockSpec((groups * dv, tq), lambda b, h, i: (h, b * nq + i)),
        out_shape=jax.ShapeDtypeStruct((n_kv * groups * dv, t), BF16),
        scratch_shapes=scratch_shapes,
        compiler_params=pltpu.CompilerParams(
            dimension_semantics=("parallel", "parallel", "arbitrary"),
            vmem_limit_bytes=_vmem_limit(
                _nbytes((tq, groups * dk), BF16), _nbytes((seq, dk), BF16),
                _nbytes((seq, dv), BF16), _nbytes((tq, groups * dv), BF16),
                scratch=scratch_bytes,
                temps=2 * _nbytes((seq, dk), F32) + 4 * _nbytes((kc, sq), F32)),
        ),
        name="attention",
    )(qt_arr, k_arr, vt_arr, q_gain, k_gain)


def _merge_kernel(u_ref, oa_ref, ob_ref, wa_hbm, wb_hbm, wt_hbm, o_ref,
                  wabf_ref, wbbf_ref, wgbf_ref, stage_ref, sem_ref, *, tn, gate_row0):
    @pl.when(pl.program_id(0) == 0)
    def _():
        _stage_bf16(wa_hbm, wabf_ref, stage_ref, sem_ref)
        _stage_bf16(wb_hbm, wbbf_ref, stage_ref, sem_ref)
        _stage_bf16(wt_hbm, wgbf_ref, stage_ref, sem_ref, row0=gate_row0,
                    n_rows=wgbf_ref.shape[1], transpose=True)

    u = u_ref[...]
    d = o_ref.shape[1]
    tn_dims = (((0,), (0,)), ((), ()))
    for jj in range(d // tn):
        cols = slice(jj * tn, (jj + 1) * tn)
        gcols = slice(d + jj * tn, d + (jj + 1) * tn)
        ga = jnp.dot(u, wgbf_ref[:, cols], preferred_element_type=F32)
        gb = jnp.dot(u, wgbf_ref[:, gcols], preferred_element_type=F32)
        ya = lax.dot_general(oa_ref[...], wabf_ref[:, cols], tn_dims, preferred_element_type=F32)
        yb = lax.dot_general(ob_ref[...], wbbf_ref[:, cols], tn_dims, preferred_element_type=F32)
        o_ref[:, cols] = (jax.nn.sigmoid(ga) * ya + jax.nn.sigmoid(gb) * yb).astype(BF16)


def _merge(u, oa_t, ob_t, w_a_out, w_b_out, w_in_t, *, tm, tn, gate_row0, stage_rows):
    t, d = u.shape
    ka = oa_t.shape[0]
    kb = ob_t.shape[0]
    assert ka % stage_rows == 0 and kb % stage_rows == 0 and (2 * d) % stage_rows == 0
    assert w_a_out.shape[1] == d and w_in_t.shape[1] == d
    return pl.pallas_call(
        functools.partial(_merge_kernel, tn=tn, gate_row0=gate_row0),
        grid=(t // tm,),
        in_specs=[
            pl.BlockSpec((tm, d), lambda i: (i, 0)),
            pl.BlockSpec((ka, tm), lambda i: (0, i)),
            pl.BlockSpec((kb, tm), lambda i: (0, i)),
            pl.BlockSpec(memory_space=pl.ANY),
            pl.BlockSpec(memory_space=pl.ANY),
            pl.BlockSpec(memory_space=pl.ANY),
        ],
        out_specs=pl.BlockSpec((tm, d), lambda i: (i, 0)),
        out_shape=jax.ShapeDtypeStruct((t, d), BF16),
        scratch_shapes=[pltpu.VMEM((ka, d), BF16), pltpu.VMEM((kb, d), BF16),
                        pltpu.VMEM((d, 2 * d), BF16), pltpu.VMEM((2, stage_rows, d), F32),
                        pltpu.SemaphoreType.DMA((2,))],
        compiler_params=pltpu.CompilerParams(
            dimension_semantics=("arbitrary",),
            vmem_limit_bytes=_vmem_limit(
                _nbytes((tm, d), BF16), _nbytes((tm, ka), BF16), _nbytes((tm, kb), BF16),
                _nbytes((tm, d), BF16),
                scratch=_nbytes((ka + kb + 2 * d, d), BF16) + _nbytes((2, stage_rows, d), F32),
                temps=8 * _nbytes((tm, tn), F32)),
        ),
        name="merge",
    )(u, oa_t, ob_t, w_a_out, w_b_out, w_in_t)


def _resid_rows_kernel(lhs_ref, w_hbm, resid_ref, g_ref, h_ref, un_ref,
                       wbf_ref, stage_ref, sem_ref, *, tn):
    @pl.when(pl.program_id(0) == 0)
    def _():
        _stage_bf16(w_hbm, wbf_ref, stage_ref, sem_ref)

    lhs = lhs_ref[...]
    d = h_ref.shape[1]
    ss = jnp.zeros((h_ref.shape[0], 1), F32)
    for jj in range(d // tn):
        cols = slice(jj * tn, (jj + 1) * tn)
        val = resid_ref[:, cols] + jnp.dot(lhs, wbf_ref[:, cols], preferred_element_type=F32)
        h_ref[:, cols] = val
        ss = ss + jnp.sum(val * val, axis=-1, keepdims=True)
    inv = lax.rsqrt(ss * (1.0 / d) + RMS_EPS)
    un_ref[...] = (h_ref[...] * inv * g_ref[...]).astype(BF16)


def _resid_rows(lhs, w, resid, gain, *, tm, tn, stage_rows):
    t, k = lhs.shape
    d = w.shape[1]
    assert k % stage_rows == 0 and d % tn == 0
    return pl.pallas_call(
        functools.partial(_resid_rows_kernel, tn=tn),
        grid=(t // tm,),
        in_specs=[
            pl.BlockSpec((tm, k), lambda i: (i, 0)),
            pl.BlockSpec(memory_space=pl.ANY),
            pl.BlockSpec((tm, d), lambda i: (i, 0)),
            pl.BlockSpec((1, d), lambda i: (0, 0)),
        ],
        out_specs=[
            pl.BlockSpec((tm, d), lambda i: (i, 0)),
            pl.BlockSpec((tm, d), lambda i: (i, 0)),
        ],
        out_shape=[
            jax.ShapeDtypeStruct((t, d), F32),
            jax.ShapeDtypeStruct((t, d), BF16),
        ],
        scratch_shapes=[pltpu.VMEM((k, d), BF16), pltpu.VMEM((2, stage_rows, d), F32),
                        pltpu.SemaphoreType.DMA((2,))],
        compiler_params=pltpu.CompilerParams(
            dimension_semantics=("arbitrary",),
            vmem_limit_bytes=_vmem_limit(
                _nbytes((tm, k), BF16), 2 * _nbytes((tm, d), F32), _nbytes((tm, d), BF16),
                scratch=_nbytes((k, d), BF16) + _nbytes((2, stage_rows, d), F32),
                temps=4 * _nbytes((tm, tn), F32) + _nbytes((tm, d), F32)),
        ),
        name="resid_rows",
    )(lhs, w, resid, gain)


def _ffn_up_kernel(un_ref, wg_ref, wu_ref, o_ref):
    un = un_ref[...]
    g = jnp.dot(un, wg_ref[...].astype(BF16), preferred_element_type=F32)
    u = jnp.dot(un, wu_ref[...].astype(BF16), preferred_element_type=F32)
    o_ref[...] = (jax.nn.silu(g) * u).astype(BF16)


def _ffn_up(un, w_gate_up, *, tm, tn):
    t, d = un.shape
    d_ff = w_gate_up.shape[1] // 2
    nj = d_ff // tn
    return pl.pallas_call(
        _ffn_up_kernel,
        grid=(t // tm, nj),
        in_specs=[
            pl.BlockSpec((tm, d), lambda i, j: (i, 0)),
            pl.BlockSpec((d, tn), lambda i, j: (0, j)),
            pl.BlockSpec((d, tn), lambda i, j: (0, nj + j)),
        ],
        out_specs=pl.BlockSpec((tm, tn), lambda i, j: (i, j)),
        out_shape=jax.ShapeDtypeStruct((t, d_ff), BF16),
        compiler_params=pltpu.CompilerParams(
            dimension_semantics=("parallel", "arbitrary"),
            vmem_limit_bytes=_vmem_limit(
                _nbytes((tm, d), BF16), 2 * _nbytes((d, tn), F32), _nbytes((tm, tn), BF16),
                temps=2 * _nbytes((d, tn), BF16) + 4 * _nbytes((tm, tn), F32)),
        ),
        name="ffn_up",
    )(un, w_gate_up, w_gate_up)


def _ple_kernel(un_ref, w_hbm, h_ref, p_ref, wp_hbm, gpost_ref, gfin_ref,
                o_ref, wbf_ref, wpbf_ref, stage_ref, sem_ref, *, tn, final_norm):
    @pl.when(pl.program_id(0) == 0)
    def _():
        _stage_bf16(w_hbm, wbf_ref, stage_ref, sem_ref)
        _stage_bf16(wp_hbm, wpbf_ref, stage_ref, sem_ref)

    un = un_ref[...]
    d = o_ref.shape[1]
    e = _rms(jnp.dot(p_ref[...].astype(BF16), wpbf_ref[...], preferred_element_type=F32),
             gpost_ref[...])
    ss = jnp.zeros((o_ref.shape[0], 1), F32)
    for jj in range(d // tn):
        cols = slice(jj * tn, (jj + 1) * tn)
        gate = jax.nn.sigmoid(jnp.dot(un, wbf_ref[:, cols], preferred_element_type=F32))
        val = h_ref[:, cols] + gate * e[:, cols]
        o_ref[:, cols] = val
        ss = ss + jnp.sum(val * val, axis=-1, keepdims=True)
    if final_norm:
        inv = lax.rsqrt(ss * (1.0 / d) + RMS_EPS)
        o_ref[...] = o_ref[...] * inv * gfin_ref[...]


def _ple(un, w_gate, h, p2, w_proj, g_post, g_final, *, tm, tn, final_norm, stage_rows):
    t, d = un.shape
    pd = p2.shape[1]
    assert d % stage_rows == 0 and pd % stage_rows == 0
    return pl.pallas_call(
        functools.partial(_ple_kernel, tn=tn, final_norm=final_norm),
        grid=(t // tm,),
        in_specs=[
            pl.BlockSpec((tm, d), lambda i: (i, 0)),
            pl.BlockSpec(memory_space=pl.ANY),
            pl.BlockSpec((tm, d), lambda i: (i, 0)),
            pl.BlockSpec((tm, pd), lambda i: (i, 0)),
            pl.BlockSpec(memory_space=pl.ANY),
            pl.BlockSpec((1, d), lambda i: (0, 0)),
            pl.BlockSpec((1, d), lambda i: (0, 0)),
        ],
        out_specs=pl.BlockSpec((tm, d), lambda i: (i, 0)),
        out_shape=jax.ShapeDtypeStruct((t, d), F32),
        scratch_shapes=[pltpu.VMEM((d, d), BF16), pltpu.VMEM((pd, d), BF16),
                        pltpu.VMEM((2, stage_rows, d), F32), pltpu.SemaphoreType.DMA((2,))],
        compiler_params=pltpu.CompilerParams(
            dimension_semantics=("arbitrary",),
            vmem_limit_bytes=_vmem_limit(
                _nbytes((tm, d), BF16), 2 * _nbytes((tm, d), F32), _nbytes((tm, pd), F32),
                scratch=_nbytes((d + pd, d), BF16) + _nbytes((2, stage_rows, d), F32),
                temps=2 * _nbytes((tm, d), F32) + 4 * _nbytes((tm, tn), F32)),
        ),
        name="ple_final",
    )(un, w_gate, h, p2, w_proj, g_post, g_final)


def kernel(x, p, g_mix, w_in, q_norm_a, k_norm_a, q_a_norm, w_q_b, kv_a_norm, w_kv_b,
           w_a_out, w_b_out, w_o, g_ffn, w_gate_up, w_down, g_ple, w_ple_gate,
           w_ple_proj, g_ple_post, g_final):
    batch, seq, d = x.shape
    depth = w_in.shape[0]
    t = batch * seq
    q_lora = q_a_norm.shape[-1]
    kv_lora = kv_a_norm.shape[-1]
    assert q_lora == kv_lora
    n_qa = N_HEADS_A * HEAD_DIM_A
    n_kva = N_KV_A * HEAD_DIM_A
    lat0 = n_qa + 2 * n_kva
    kr0 = lat0 + q_lora + kv_lora
    gate0 = kr0 + ROPE_DIM_B

    cos_a, sin_a = _rope_tables(seq, HEAD_DIM_A)
    cos_b, sin_b = _rope_tables(seq, ROPE_DIM_B)

    h = x.reshape(t, d)
    for i in range(depth):
        wi_t = jnp.swapaxes(w_in[i], 0, 1)
        wq_pad = jnp.pad(
            w_q_b[i].reshape(q_lora, N_HEADS_B, QK_DIM_B),
            ((0, 0), (0, 0), (0, QK_PAD_B - QK_DIM_B))).reshape(q_lora, N_HEADS_B * QK_PAD_B)
        lat_gains = jnp.stack([q_a_norm[i], kv_a_norm[i]])[:, None, :]
        qt_a, k_a, vt_a, qt_b, k_b, vt_b, u = _inproj(
            h, g_mix[i][None], wi_t, wq_pad, w_kv_b[i], q_norm_a[i][None], k_norm_a[i][None],
            lat_gains, cos_a, sin_a, cos_b, sin_b, seq=seq, tm=ROW_TILE, kr_col0=kr0)

        groups_a = N_HEADS_A // N_KV_A
        oa_t = _attention(qt_a, k_a, vt_a, batch=batch, seq=seq, n_kv=N_KV_A,
                          groups=groups_a, dk=HEAD_DIM_A, dv=HEAD_DIM_A,
                          tq=ATTN_STREAMS * ATTN_Q_STREAM // groups_a, kc=ATTN_KV_CHUNK,
                          sq=ATTN_Q_STREAM, aug_lane=HEAD_DIM_A, q_gain=q_norm_a[i][None],
                          k_gain=k_norm_a[i][None], unit_rms_scale=(HEAD_DIM_A ** -0.5) * LOG2E)
        ob_t = _attention(qt_b, k_b, vt_b, batch=batch, seq=seq, n_kv=N_HEADS_B,
                          groups=1, dk=QK_PAD_B, dv=V_DIM_B,
                          tq=ATTN_STREAMS * ATTN_Q_STREAM, kc=ATTN_KV_CHUNK, sq=ATTN_Q_STREAM,
                          aug_lane=QK_DIM_B)

        merged = _merge(u, oa_t, ob_t, w_a_out[i], w_b_out[i], wi_t, tm=ROW_TILE, tn=COL_TILE,
                        gate_row0=gate0, stage_rows=STAGE_ROWS)
        h1, un1 = _resid_rows(merged, w_o[i], h, g_ffn[i][None], tm=ROW_TILE, tn=COL_TILE,
                              stage_rows=STAGE_ROWS)
        act = _ffn_up(un1, w_gate_up[i], tm=ROW_TILE_FFN_UP, tn=COL_TILE)
        h2, un2 = _resid_rows(act, w_down[i], h1, g_ple[i][None], tm=ROW_TILE_DOWN, tn=COL_TILE,
                              stage_rows=STAGE_ROWS)
        h = _ple(un2, w_ple_gate[i], h2, p[i].reshape(t, -1), w_ple_proj[i],
                 g_ple_post[i][None], g_final[None], tm=ROW_TILE, tn=COL_TILE,
                 final_norm=(i == depth - 1), stage_rows=STAGE_ROWS)
    return h.reshape(batch, seq, d)
```

```python
import functools
import math

import numpy as np
import jax
import jax.numpy as jnp
from jax import lax
from jax.experimental import pallas as pl
from jax.experimental.pallas import tpu as pltpu

F32 = jnp.float32
BF16 = jnp.bfloat16

GRID_W = 64
ROPE_THETA = 10000.0
RMS_EPS = 1e-6

N_HEADS_A = 8
N_KV_A = 2
HEAD_DIM_A = 128
N_HEADS_B = 8
NOPE_DIM_B = 128
ROPE_DIM_B = 64
V_DIM_B = 128
QK_DIM_B = NOPE_DIM_B + ROPE_DIM_B
QK_PAD_B = 256

LANES = 128
BF16_ROWS = 16
V7X_VMEM_BYTES = 64 * 1024 * 1024
VMEM_SLACK_BYTES = 4 << 20
VMEM_RESERVED_BYTES = 6 << 20
LOG2E = math.log2(math.e)

ROW_TILE = 512
ROW_TILE_DOWN = 256
ROW_TILE_FFN_UP = 1024
COL_TILE = 512
STAGE_ROWS = 256
ATTN_Q_STREAM = 512
ATTN_KV_CHUNK = 512
ATTN_STREAMS = 8


def _vmem_limit(*block_bytes, scratch=0, temps=0):
    need = 2 * sum(block_bytes) + scratch + temps + VMEM_SLACK_BYTES
    return int(min(need, V7X_VMEM_BYTES - VMEM_RESERVED_BYTES))


def _nbytes(shape, dtype):
    return math.prod(shape) * jnp.dtype(dtype).itemsize


def _rms(x32, gain_row):
    ms = jnp.mean(x32 * x32, axis=-1, keepdims=True)
    return x32 * lax.rsqrt(ms + RMS_EPS) * gain_row


def _swap_chunks(y, chunk):
    lane = lax.broadcasted_iota(jnp.int32, y.shape, 1)
    first = (lane % (2 * chunk)) < chunk
    return jnp.where(first, pltpu.roll(y, LANES - chunk, 1), pltpu.roll(y, chunk, 1))


def _rope(y, cos, sin_signed, chunk):
    return y * cos + _swap_chunks(y, chunk) * sin_signed


def _rope_tables(seq, rot_dim):
    rows = seq // GRID_W
    row = np.repeat(np.arange(rows, dtype=np.float64), GRID_W)
    col = np.tile(np.arange(GRID_W, dtype=np.float64), rows)
    half = rot_dim // 2
    inv = 1.0 / np.power(ROPE_THETA, np.arange(0, half, 2, dtype=np.float64) / half)
    ar = row[:, None] * inv[None, :]
    ac = col[:, None] * inv[None, :]
    cos = np.concatenate([np.cos(ar), np.cos(ar), np.cos(ac), np.cos(ac)], axis=-1)
    sin = np.concatenate([-np.sin(ar), np.sin(ar), -np.sin(ac), np.sin(ac)], axis=-1)
    pad = LANES - rot_dim
    if pad:
        cos = np.concatenate([cos, np.ones((seq, pad))], axis=-1)
        sin = np.concatenate([sin, np.zeros((seq, pad))], axis=-1)
    return jnp.asarray(cos, F32), jnp.asarray(sin, F32)


def _stage_bf16(w_hbm, dst_ref, stage_ref, sem_ref, *, row0=0, n_rows=None, transpose=False,
                keep_rows_last=None):
    rows = stage_ref.shape[1]
    n_rows = w_hbm.shape[0] if n_rows is None else n_rows
    n_chunks = n_rows // rows

    def chunk_copy(r, slot):
        return pltpu.make_async_copy(w_hbm.at[pl.ds(row0 + r * rows, rows), :],
                                     stage_ref.at[slot], sem_ref.at[slot])

    chunk_copy(0, 0).start()
    for r in range(n_chunks):
        slot = r % 2
        if r + 1 < n_chunks:
            chunk_copy(r + 1, 1 - slot).start()
        chunk_copy(r, slot).wait()
        blk = stage_ref[slot]
        if keep_rows_last is not None and r == n_chunks - 1:
            row = lax.broadcasted_iota(jnp.int32, blk.shape, 0)
            blk = jnp.where(row < keep_rows_last, blk, 0.0)
        if transpose:
            dst_ref[:, r * rows:(r + 1) * rows] = blk.astype(BF16).T
        else:
            dst_ref[r * rows:(r + 1) * rows, :] = blk.astype(BF16)


def _inproj_kernel(x_ref, g_ref, wt_hbm, wq_hbm, wkv_hbm, qn_ref, kn_ref, lg_ref,
                   cosa_ref, sina_ref, cosb_ref, sinb_ref,
                   qta_ref, ka_ref, vta_ref, qtb_ref, kb_ref, vtb_ref, u_ref,
                   wbf_ref, wqbf_ref, wkvbf_ref, stage_ref, sem_ref,
                   *, q_scale, qb_scale, n_q, n_kv, lora, kr_rows):
    hd = HEAD_DIM_A

    @pl.when(pl.program_id(0) == 0)
    def _():
        _stage_bf16(wt_hbm, wbf_ref, stage_ref, sem_ref, n_rows=wbf_ref.shape[1],
                    transpose=True, keep_rows_last=kr_rows)
        _stage_bf16(wq_hbm, wqbf_ref, stage_ref, sem_ref)
        _stage_bf16(wkv_hbm, wkvbf_ref, stage_ref, sem_ref)

    u = _rms(x_ref[...], g_ref[...]).astype(BF16)
    u_ref[...] = u
    cos = cosa_ref[...]
    sin = sina_ref[...]

    def proj(c0, width):
        return jnp.dot(u, wbf_ref[:, c0:c0 + width], preferred_element_type=F32)

    heads_per_dot = COL_TILE // hd
    for h0 in range(0, n_q, heads_per_dot):
        z = proj(h0 * hd, heads_per_dot * hd)
        for h in range(heads_per_dot):
            y = _rms(z[:, h * hd:(h + 1) * hd], qn_ref[...])
            qta_ref[(h0 + h) * hd:(h0 + h + 1) * hd, :] = (
                _rope(y, cos, sin, hd // 4) * q_scale).astype(BF16).T

    z = proj(n_q * hd, 2 * n_kv * hd)
    for h in range(n_kv):
        y = _rms(z[:, h * hd:(h + 1) * hd], kn_ref[...])
        ka_ref[:, h * hd:(h + 1) * hd] = _rope(y, cos, sin, hd // 4).astype(BF16)
        vta_ref[h * hd:(h + 1) * hd, :] = z[:, (n_kv + h) * hd:(n_kv + h + 1) * hd].astype(BF16).T

    lat0 = (n_q + 2 * n_kv) * hd
    cq = _rms(proj(lat0, lora), lg_ref[0]).astype(BF16)
    ckv = _rms(proj(lat0 + lora, lora), lg_ref[1]).astype(BF16)
    cosb = cosb_ref[...]
    sinb = sinb_ref[...]
    kpe = _rope(proj(lat0 + 2 * lora, LANES), cosb, sinb, ROPE_DIM_B // 4)
    lane = lax.broadcasted_iota(jnp.int32, kpe.shape, 1)
    kpe = jnp.where(lane == ROPE_DIM_B, 1.0, kpe).astype(BF16)
    zq = jnp.dot(cq, wqbf_ref[...], preferred_element_type=F32)
    zkv = jnp.dot(ckv, wkvbf_ref[...], preferred_element_type=F32)
    for h in range(N_HEADS_B):
        c0 = h * QK_PAD_B
        qtb_ref[c0:c0 + LANES, :] = (zq[:, c0:c0 + LANES] * qb_scale).astype(BF16).T
        pe = _rope(zq[:, c0 + LANES:c0 + 2 * LANES], cosb, sinb, ROPE_DIM_B // 4)
        qtb_ref[c0 + LANES:c0 + 2 * LANES, :] = (pe * qb_scale).astype(BF16).T
        kb_ref[:, c0:c0 + LANES] = zkv[:, c0:c0 + LANES].astype(BF16)
        kb_ref[:, c0 + LANES:c0 + 2 * LANES] = kpe
        vtb_ref[h * V_DIM_B:(h + 1) * V_DIM_B, :] = zkv[:, c0 + LANES:c0 + 2 * LANES].astype(BF16).T


def _inproj(x2, g_mix, w_in_t, wq_pad, w_kv_b, q_norm, k_norm, lat_gains,
            cos_a, sin_a, cos_b, sin_b, *, seq, tm, kr_col0):
    t, d = x2.shape
    lora = lat_gains.shape[-1]
    n_qa = N_HEADS_A * HEAD_DIM_A
    n_kva = N_KV_A * HEAD_DIM_A
    n_qkv = n_qa + 2 * n_kva
    n_cols = n_qkv + 2 * lora + LANES
    nqb = wq_pad.shape[1]
    nvb = N_HEADS_B * V_DIM_B
    assert kr_col0 == n_qkv + 2 * lora and n_cols % LANES == 0
    assert (N_HEADS_A * HEAD_DIM_A) % COL_TILE == 0 and w_in_t.shape[0] >= n_cols
    assert t % tm == 0 and seq % tm == 0 and lora % LANES == 0
    assert wq_pad.shape == (lora, N_HEADS_B * QK_PAD_B) and w_kv_b.shape == (lora, nqb)
    assert nqb == d
    sblocks = seq // tm
    q_scale = (HEAD_DIM_A ** -0.5) * LOG2E
    qb_scale = (QK_DIM_B ** -0.5) * LOG2E
    return pl.pallas_call(
        functools.partial(_inproj_kernel, q_scale=q_scale, qb_scale=qb_scale, n_q=N_HEADS_A,
                          n_kv=N_KV_A, lora=lora, kr_rows=ROPE_DIM_B),
        grid=(t // tm,),
        in_specs=[
            pl.BlockSpec((tm, d), lambda i: (i, 0)),
            pl.BlockSpec((1, d), lambda i: (0, 0)),
            pl.BlockSpec(memory_space=pl.ANY),
            pl.BlockSpec(memory_space=pl.ANY),
            pl.BlockSpec(memory_space=pl.ANY),
            pl.BlockSpec((1, HEAD_DIM_A), lambda i: (0, 0)),
            pl.BlockSpec((1, HEAD_DIM_A), lambda i: (0, 0)),
            pl.BlockSpec((2, 1, lora), lambda i: (0, 0, 0)),
            pl.BlockSpec((tm, LANES), lambda i: (i % sblocks, 0)),
            pl.BlockSpec((tm, LANES), lambda i: (i % sblocks, 0)),
            pl.BlockSpec((tm, LANES), lambda i: (i % sblocks, 0)),
            pl.BlockSpec((tm, LANES), lambda i: (i % sblocks, 0)),
        ],
        out_specs=[
            pl.BlockSpec((n_qa, tm), lambda i: (0, i)),
            pl.BlockSpec((tm, n_kva), lambda i: (i, 0)),
            pl.BlockSpec((n_kva, tm), lambda i: (0, i)),
            pl.BlockSpec((nqb, tm), lambda i: (0, i)),
            pl.BlockSpec((tm, nqb), lambda i: (i, 0)),
            pl.BlockSpec((nvb, tm), lambda i: (0, i)),
            pl.BlockSpec((tm, d), lambda i: (i, 0)),
        ],
        out_shape=[
            jax.ShapeDtypeStruct((n_qa, t), BF16),
            jax.ShapeDtypeStruct((t, n_kva), BF16),
            jax.ShapeDtypeStruct((n_kva, t), BF16),
            jax.ShapeDtypeStruct((nqb, t), BF16),
            jax.ShapeDtypeStruct((t, nqb), BF16),
            jax.ShapeDtypeStruct((nvb, t), BF16),
            jax.ShapeDtypeStruct((t, d), BF16),
        ],
        scratch_shapes=[pltpu.VMEM((d, n_cols), BF16), pltpu.VMEM((lora, nqb), BF16),
                        pltpu.VMEM((lora, nqb), BF16), pltpu.VMEM((2, LANES, d), F32),
                        pltpu.SemaphoreType.DMA((2,))],
        compiler_params=pltpu.CompilerParams(
            dimension_semantics=("arbitrary",),
            vmem_limit_bytes=_vmem_limit(
                _nbytes((tm, d), F32), _nbytes((tm, n_qkv), BF16), 2 * _nbytes((tm, nqb), BF16),
                _nbytes((tm, nvb), BF16), _nbytes((tm, d), BF16), 4 * _nbytes((tm, LANES), F32),
                scratch=_nbytes((d, n_cols), BF16) + 2 * _nbytes((lora, nqb), BF16)
                + _nbytes((2, LANES, d), F32),
                temps=_nbytes((tm, d), F32) + 4 * _nbytes((tm, nqb), F32)),
        ),
        name="inproj",
    )(x2, g_mix, w_in_t, wq_pad, w_kv_b, q_norm, k_norm, lat_gains, cos_a, sin_a, cos_b, sin_b)


_L_SAFE = 2.0 ** -64


def _attn_kernel(q_ref, k_ref, v_ref, qg_ref, kg_ref, o_ref, vt_ref, kmax_ref,
                 qa_ref, s_ref, acc_ref, *maybe_kaug, groups, dk, dv, kc, sq, aug_lane,
                 unit_rms_scale):
    seq = k_ref.shape[0]
    tq = q_ref.shape[1]
    append_tile = aug_lane == dk
    kaug_ref = maybe_kaug[0] if append_tile else k_ref
    dka = kaug_ref.shape[1]

    @pl.when(pl.program_id(2) == 0)
    def _():
        if unit_rms_scale is None:
            kt = k_ref[...].T.astype(F32)
            kmax2 = jnp.max(jnp.sum(kt * kt, axis=0, keepdims=True), axis=1, keepdims=True)
            kmax_ref[...] = jnp.broadcast_to(kmax2, kmax_ref.shape)
        if append_tile:
            kaug_ref[:, :dk] = k_ref[...]
            lane = lax.broadcasted_iota(jnp.int32, (seq, dka - dk), 1)
            kaug_ref[:, dk:] = jnp.where(lane == 0, 1.0, 0.0).astype(BF16)
        vt_ref[:dv, :] = v_ref[...]
        vt_ref[dv:, :] = jnp.ones((vt_ref.shape[0] - dv, seq), BF16)

    streams = [(r, g) for r in range(tq // sq) for g in range(groups)]

    n_str = len(streams)

    def stage_queries(shift_row_fn):
        for i, (r, g) in enumerate(streams):
            qt = q_ref[g * dk:(g + 1) * dk, r * sq:(r + 1) * sq].astype(F32)
            neg = -shift_row_fn(i, qt)
            if append_tile:
                row = lax.broadcasted_iota(jnp.int32, (dka - dk, sq), 0)
                qa = jnp.concatenate([qt, jnp.where(row == 0, neg, 0.0)], axis=0)
            else:
                row = lax.broadcasted_iota(jnp.int32, (dk, sq), 0)
                qa = jnp.where(row == aug_lane, neg, qt)
            qa_ref[i] = qa.astype(BF16)

    def run():
        n_chunks = seq // kc
        acc_ref[...] = jnp.zeros(acc_ref.shape, F32)

        def scores(i, c):
            start = c * kc if isinstance(c, int) else pl.multiple_of(c * kc, kc)
            s_ref[i % 2] = jnp.dot(kaug_ref[pl.ds(start, kc), :], qa_ref[i],
                                   preferred_element_type=F32)

        def accumulate(i, c):
            start = c * kc if isinstance(c, int) else pl.multiple_of(c * kc, kc)
            p = jnp.exp2(s_ref[i % 2]).astype(BF16)
            acc_ref[i] += jnp.dot(vt_ref[:, pl.ds(start, kc)], p, preferred_element_type=F32)

        def chunk(c, last):
            for i in range(n_str):
                if i + 1 < n_str:
                    scores(i + 1, c)
                elif not last:
                    scores(0, c + 1)
                accumulate(i, c)

        def body(c, carry):
            chunk(c, last=False)
            return carry

        scores(0, 0)
        lax.fori_loop(0, n_chunks - 1, body, 0, unroll=2)
        chunk(n_chunks - 1, last=True)
        return tuple((acc_ref[i, :dv, :], acc_ref[i, dv:dv + 1, :]) for i in range(n_str))

    def write_out(results):
        for (r, g), (acc, l) in zip(streams, results):
            o_ref[g * dv:(g + 1) * dv, r * sq:(r + 1) * sq] = (acc * (1.0 / l)).astype(BF16)

    if unit_rms_scale is None:
        kmax2 = kmax_ref[0:1, 0:1]
        stage_queries(
            lambda i, qt: jnp.sqrt(jnp.sum(qt * qt, axis=0, keepdims=True) * kmax2))
    else:
        bound = (jnp.max(jnp.abs(qg_ref[...]), axis=1, keepdims=True)
                 * jnp.max(jnp.abs(kg_ref[...]), axis=1, keepdims=True) * (dk * unit_rms_scale))
        stage_queries(lambda i, qt: jnp.broadcast_to(bound, (1, sq)))
    fast = run()
    write_out(fast)

    l_min = None
    for _, l in fast:
        cur = jnp.min(l)
        l_min = cur if l_min is None else jnp.minimum(l_min, cur)

    @pl.when(jnp.logical_not(l_min >= _L_SAFE))
    def _():
        stage_queries(lambda i, qt: jnp.zeros((1, sq), F32))

        def max_body(c, ms):
            start = pl.multiple_of(c * kc, kc)
            ks = kaug_ref[pl.ds(start, kc), :]
            return tuple(
                jnp.maximum(m, jnp.max(
                    jnp.dot(ks, qa_ref[i], preferred_element_type=F32), axis=0, keepdims=True))
                for i, m in enumerate(ms))

        ms = lax.fori_loop(0, seq // kc, max_body,
                           tuple(jnp.full((1, sq), -jnp.inf, F32) for _ in streams))
        stage_queries(lambda i, qt: ms[i])
        write_out(run())


def _attention(qt_arr, k_arr, vt_arr, *, batch, seq, n_kv, groups, dk, dv,
               tq, kc, sq, aug_lane, q_gain=None, k_gain=None, unit_rms_scale=None):
    t = batch * seq
    if unit_rms_scale is None:
        q_gain = k_gain = jnp.ones((1, dk), F32)
    nq = seq // tq
    dka = dk + LANES if aug_lane == dk else dk
    n_str = (tq // sq) * groups
    assert aug_lane <= dk and tq % sq == 0 and seq % kc == 0 and n_str % 2 == 0
    dvx = dv + BF16_ROWS
    scratch_shapes = [pltpu.VMEM((dvx, seq), BF16), pltpu.VMEM((8, LANES), F32),
                      pltpu.VMEM((n_str, dka, sq), BF16), pltpu.VMEM((2, kc, sq), F32),
                      pltpu.VMEM((n_str, dvx, sq), F32)]
    scratch_bytes = (_nbytes((dvx, seq), BF16) + _nbytes((n_str, dka, sq), BF16)
                     + _nbytes((2, kc, sq), F32) + _nbytes((n_str, dvx, sq), F32))
    if aug_lane == dk:
        scratch_shapes.append(pltpu.VMEM((seq, dka), BF16))
        scratch_bytes += _nbytes((seq, dka), BF16)
    return pl.pallas_call(
        functools.partial(_attn_kernel, groups=groups, dk=dk, dv=dv, kc=kc, sq=sq,
                          aug_lane=aug_lane, unit_rms_scale=unit_rms_scale),
        grid=(batch, n_kv, nq),
        in_specs=[
            pl.BlockSpec((groups * dk, tq), lambda b, h, i: (h, b * nq + i)),
            pl.BlockSpec((seq, dk), lambda b, h, i: (b, h)),
            pl.BlockSpec((dv, seq), lambda b, h, i: (h, b)),
            pl.BlockSpec((1, dk), lambda b, h, i: (0, 0)),
            pl.BlockSpec((1, dk), lambda b, h, i: (0, 0)),
        ],
        out_specs=pl.BlockSpec((groups * dv, tq), lambda b, h, i: (h, b * nq + i)),
        out_shape=jax.ShapeDtypeStruct((n_kv * groups * dv, t), BF16),
        scratch_shapes=scratch_shapes,
        compiler_params=pltpu.CompilerParams(
            dimension_semantics=("parallel", "parallel", "arbitrary"),
            vmem_limit_bytes=_vmem_limit(
                _nbytes((tq, groups * dk), BF16), _nbytes((seq, dk), BF16),
                _nbytes((seq, dv), BF16), _nbytes((tq, groups * dv), BF16),
                scratch=scratch_bytes,
                temps=2 * _nbytes((seq, dk), F32) + 4 * _nbytes((kc, sq), F32)),
        ),
        name="attention",
    )(qt_arr, k_arr, vt_arr, q_gain, k_gain)


def _merge_kernel(u_ref, oa_ref, ob_ref, wa_hbm, wb_hbm, wt_hbm, o_ref,
                  wabf_ref, wbbf_ref, wgbf_ref, stage_ref, sem_ref, *, tn, gate_row0):
    @pl.when(pl.program_id(0) == 0)
    def _():
        _stage_bf16(wa_hbm, wabf_ref, stage_ref, sem_ref)
        _stage_bf16(wb_hbm, wbbf_ref, stage_ref, sem_ref)
        _stage_bf16(wt_hbm, wgbf_ref, stage_ref, sem_ref, row0=gate_row0,
                    n_rows=wgbf_ref.shape[1], transpose=True)

    u = u_ref[...]
    d = o_ref.shape[1]
    tn_dims = (((0,), (0,)), ((), ()))
    for jj in range(d // tn):
        cols = slice(jj * tn, (jj + 1) * tn)
        gcols = slice(d + jj * tn, d + (jj + 1) * tn)
        ga = jnp.dot(u, wgbf_ref[:, cols], preferred_element_type=F32)
        gb = jnp.dot(u, wgbf_ref[:, gcols], preferred_element_type=F32)
        ya = lax.dot_general(oa_ref[...], wabf_ref[:, cols], tn_dims, preferred_element_type=F32)
        yb = lax.dot_general(ob_ref[...], wbbf_ref[:, cols], tn_dims, preferred_element_type=F32)
        o_ref[:, cols] = (jax.nn.sigmoid(ga) * ya + jax.nn.sigmoid(gb) * yb).astype(BF16)


def _merge(u, oa_t, ob_t, w_a_out, w_b_out, w_in_t, *, tm, tn, gate_row0, stage_rows):
    t, d = u.shape
    ka = oa_t.shape[0]
    kb = ob_t.shape[0]
    assert ka % stage_rows == 0 and kb % stage_rows == 0 and (2 * d) % stage_rows == 0
    assert w_a_out.shape[1] == d and w_in_t.shape[1] == d
    return pl.pallas_call(
        functools.partial(_merge_kernel, tn=tn, gate_row0=gate_row0),
        grid=(t // tm,),
        in_specs=[
            pl.BlockSpec((tm, d), lambda i: (i, 0)),
            pl.BlockSpec((ka, tm), lambda i: (0, i)),
            pl.BlockSpec((kb, tm), lambda i: (0, i)),
            pl.BlockSpec(memory_space=pl.ANY),
            pl.BlockSpec(memory_space=pl.ANY),
            pl.BlockSpec(memory_space=pl.ANY),
        ],
        out_specs=pl.BlockSpec((tm, d), lambda i: (i, 0)),
        out_shape=jax.ShapeDtypeStruct((t, d), BF16),
        scratch_shapes=[pltpu.VMEM((ka, d), BF16), pltpu.VMEM((kb, d), BF16),
                        pltpu.VMEM((d, 2 * d), BF16), pltpu.VMEM((2, stage_rows, d), F32),
                        pltpu.SemaphoreType.DMA((2,))],
        compiler_params=pltpu.CompilerParams(
            dimension_semantics=("arbitrary",),
            vmem_limit_bytes=_vmem_limit(
                _nbytes((tm, d), BF16), _nbytes((tm, ka), BF16), _nbytes((tm, kb), BF16),
                _nbytes((tm, d), BF16),
                scratch=_nbytes((ka + kb + 2 * d, d), BF16) + _nbytes((2, stage_rows, d), F32),
                temps=8 * _nbytes((tm, tn), F32)),
        ),
        name="merge",
    )(u, oa_t, ob_t, w_a_out, w_b_out, w_in_t)


def _resid_rows_kernel(lhs_ref, w_hbm, resid_ref, g_ref, h_ref, un_ref,
                       wbf_ref, stage_ref, sem_ref, *, tn):
    @pl.when(pl.program_id(0) == 0)
    def _():
        _stage_bf16(w_hbm, wbf_ref, stage_ref, sem_ref)

    lhs = lhs_ref[...]
    d = h_ref.shape[1]
    ss = jnp.zeros((h_ref.shape[0], 1), F32)
    for jj in range(d // tn):
        cols = slice(jj * tn, (jj + 1) * tn)
        val = resid_ref[:, cols] + jnp.dot(lhs, wbf_ref[:, cols], preferred_element_type=F32)
        h_ref[:, cols] = val
        ss = ss + jnp.sum(val * val, axis=-1, keepdims=True)
    inv = lax.rsqrt(ss * (1.0 / d) + RMS_EPS)
    un_ref[...] = (h_ref[...] * inv * g_ref[...]).astype(BF16)


def _resid_rows(lhs, w, resid, gain, *, tm, tn, stage_rows):
    t, k = lhs.shape
    d = w.shape[1]
    assert k % stage_rows == 0 and d % tn == 0
    return pl.pallas_call(
        functools.partial(_resid_rows_kernel, tn=tn),
        grid=(t // tm,),
        in_specs=[
            pl.BlockSpec((tm, k), lambda i: (i, 0)),
            pl.BlockSpec(memory_space=pl.ANY),
            pl.BlockSpec((tm, d), lambda i: (i, 0)),
            pl.BlockSpec((1, d), lambda i: (0, 0)),
        ],
        out_specs=[
            pl.BlockSpec((tm, d), lambda i: (i, 0)),
            pl.BlockSpec((tm, d), lambda i: (i, 0)),
        ],
        out_shape=[
            jax.ShapeDtypeStruct((t, d), F32),
            jax.ShapeDtypeStruct((t, d), BF16),
        ],
        scratch_shapes=[pltpu.VMEM((k, d), BF16), pltpu.VMEM((2, stage_rows, d), F32),
                        pltpu.SemaphoreType.DMA((2,))],
        compiler_params=pltpu.CompilerParams(
            dimension_semantics=("arbitrary",),
            vmem_limit_bytes=_vmem_limit(
                _nbytes((tm, k), BF16), 2 * _nbytes((tm, d), F32), _nbytes((tm, d), BF16),
                scratch=_nbytes((k, d), BF16) + _nbytes((2, stage_rows, d), F32),
                temps=4 * _nbytes((tm, tn), F32) + _nbytes((tm, d), F32)),
        ),
        name="resid_rows",
    )(lhs, w, resid, gain)


def _ffn_up_kernel(un_ref, wg_ref, wu_ref, o_ref):
    un = un_ref[...]
    g = jnp.dot(un, wg_ref[...].astype(BF16), preferred_element_type=F32)
    u = jnp.dot(un, wu_ref[...].astype(BF16), preferred_element_type=F32)
    o_ref[...] = (jax.nn.silu(g) * u).astype(BF16)


def _ffn_up(un, w_gate_up, *, tm, tn):
    t, d = un.shape
    d_ff = w_gate_up.shape[1] // 2
    nj = d_ff // tn
    return pl.pallas_call(
        _ffn_up_kernel,
        grid=(t // tm, nj),
        in_specs=[
            pl.BlockSpec((tm, d), lambda i, j: (i, 0)),
            pl.BlockSpec((d, tn), lambda i, j: (0, j)),
            pl.BlockSpec((d, tn), lambda i, j: (0, nj + j)),
        ],
        out_specs=pl.BlockSpec((tm, tn), lambda i, j: (i, j)),
        out_shape=jax.ShapeDtypeStruct((t, d_ff), BF16),
        compiler_params=pltpu.CompilerParams(
            dimension_semantics=("parallel", "arbitrary"),
            vmem_limit_bytes=_vmem_limit(
                _nbytes((tm, d), BF16), 2 * _nbytes((d, tn), F32), _nbytes((tm, tn), BF16),
                temps=2 * _nbytes((d, tn), BF16) + 4 * _nbytes((tm, tn), F32)),
        ),
        name="ffn_up",
    )(un, w_gate_up, w_gate_up)


def _ple_kernel(un_ref, w_hbm, h_ref, p_ref, wp_hbm, gpost_ref, gfin_ref,
                o_ref, wbf_ref, wpbf_ref, stage_ref, sem_ref, *, tn, final_norm):
    @pl.when(pl.program_id(0) == 0)
    def _():
        _stage_bf16(w_hbm, wbf_ref, stage_ref, sem_ref)
        _stage_bf16(wp_hbm, wpbf_ref, stage_ref, sem_ref)

    un = un_ref[...]
    d = o_ref.shape[1]
    e = _rms(jnp.dot(p_ref[...].astype(BF16), wpbf_ref[...], preferred_element_type=F32),
             gpost_ref[...])
    ss = jnp.zeros((o_ref.shape[0], 1), F32)
    for jj in range(d // tn):
        cols = slice(jj * tn, (jj + 1) * tn)
        gate = jax.nn.sigmoid(jnp.dot(un, wbf_ref[:, cols], preferred_element_type=F32))
        val = h_ref[:, cols] + gate * e[:, cols]
        o_ref[:, cols] = val
        ss = ss + jnp.sum(val * val, axis=-1, keepdims=True)
    if final_norm:
        inv = lax.rsqrt(ss * (1.0 / d) + RMS_EPS)
        o_ref[...] = o_ref[...] * inv * gfin_ref[...]


def _ple(un, w_gate, h, p2, w_proj, g_post, g_final, *, tm, tn, final_norm, stage_rows):
    t, d = un.shape
    pd = p2.shape[1]
    assert d % stage_rows == 0 and pd % stage_rows == 0
    return pl.pallas_call(
        functools.partial(_ple_kernel, tn=tn, final_norm=final_norm),
        grid=(t // tm,),
        in_specs=[
            pl.BlockSpec((tm, d), lambda i: (i, 0)),
            pl.BlockSpec(memory_space=pl.ANY),
            pl.BlockSpec((tm, d), lambda i: (i, 0)),
            pl.BlockSpec((tm, pd), lambda i: (i, 0)),
            pl.BlockSpec(memory_space=pl.ANY),
            pl.BlockSpec((1, d), lambda i: (0, 0)),
            pl.BlockSpec((1, d), lambda i: (0, 0)),
        ],
        out_specs=pl.BlockSpec((tm, d), lambda i: (i, 0)),
        out_shape=jax.ShapeDtypeStruct((t, d), F32),
        scratch_shapes=[pltpu.VMEM((d, d), BF16), pltpu.VMEM((pd, d), BF16),
                        pltpu.VMEM((2, stage_rows, d), F32), pltpu.SemaphoreType.DMA((2,))],
        compiler_params=pltpu.CompilerParams(
            dimension_semantics=("arbitrary",),
            vmem_limit_bytes=_vmem_limit(
                _nbytes((tm, d), BF16), 2 * _nbytes((tm, d), F32), _nbytes((tm, pd), F32),
                scratch=_nbytes((d + pd, d), BF16) + _nbytes((2, stage_rows, d), F32),
                temps=2 * _nbytes((tm, d), F32) + 4 * _nbytes((tm, tn), F32)),
        ),
        name="ple_final",
    )(un, w_gate, h, p2, w_proj, g_post, g_final)


def kernel(x, p, g_mix, w_in, q_norm_a, k_norm_a, q_a_norm, w_q_b, kv_a_norm, w_kv_b,
           w_a_out, w_b_out, w_o, g_ffn, w_gate_up, w_down, g_ple, w_ple_gate,
           w_ple_proj, g_ple_post, g_final):
    batch, seq, d = x.shape
    depth = w_in.shape[0]
    t = batch * seq
    q_lora = q_a_norm.shape[-1]
    kv_lora = kv_a_norm.shape[-1]
    assert q_lora == kv_lora
    n_qa = N_HEADS_A * HEAD_DIM_A
    n_kva = N_KV_A * HEAD_DIM_A
    lat0 = n_qa + 2 * n_kva
    kr0 = lat0 + q_lora + kv_lora
    gate0 = kr0 + ROPE_DIM_B

    cos_a, sin_a = _rope_tables(seq, HEAD_DIM_A)
    cos_b, sin_b = _rope_tables(seq, ROPE_DIM_B)

    h = x.reshape(t, d)
    for i in range(depth):
        wi_t = jnp.swapaxes(w_in[i], 0, 1)
        wq_pad = jnp.pad(
            w_q_b[i].reshape(q_lora, N_HEADS_B, QK_DIM_B),
            ((0, 0), (0, 0), (0, QK_PAD_B - QK_DIM_B))).reshape(q_lora, N_HEADS_B * QK_PAD_B)
        lat_gains = jnp.stack([q_a_norm[i], kv_a_norm[i]])[:, None, :]
        qt_a, k_a, vt_a, qt_b, k_b, vt_b, u = _inproj(
            h, g_mix[i][None], wi_t, wq_pad, w_kv_b[i], q_norm_a[i][None], k_norm_a[i][None],
            lat_gains, cos_a, sin_a, cos_b, sin_b, seq=seq, tm=ROW_TILE, kr_col0=kr0)

        groups_a = N_HEADS_A // N_KV_A
        oa_t = _attention(qt_a, k_a, vt_a, batch=batch, seq=seq, n_kv=N_KV_A,
                          groups=groups_a, dk=HEAD_DIM_A, dv=HEAD_DIM_A,
                          tq=ATTN_STREAMS * ATTN_Q_STREAM // groups_a, kc=ATTN_KV_CHUNK,
                          sq=ATTN_Q_STREAM, aug_lane=HEAD_DIM_A, q_gain=q_norm_a[i][None],
                          k_gain=k_norm_a[i][None], unit_rms_scale=(HEAD_DIM_A ** -0.5) * LOG2E)
        ob_t = _attention(qt_b, k_b, vt_b, batch=batch, seq=seq, n_kv=N_HEADS_B,
                          groups=1, dk=QK_PAD_B, dv=V_DIM_B,
                          tq=ATTN_STREAMS * ATTN_Q_STREAM, kc=ATTN_KV_CHUNK, sq=ATTN_Q_STREAM,
                          aug_lane=QK_DIM_B)

        merged = _merge(u, oa_t, ob_t, w_a_out[i], w_b_out[i], wi_t, tm=ROW_TILE, tn=COL_TILE,
                        gate_row0=gate0, stage_rows=STAGE_ROWS)
        h1, un1 = _resid_rows(merged, w_o[i], h, g_ffn[i][None], tm=ROW_TILE, tn=COL_TILE,
                              stage_rows=STAGE_ROWS)
        act = _ffn_up(un1, w_gate_up[i], tm=ROW_TILE_FFN_UP, tn=COL_TILE)
        h2, un2 = _resid_rows(act, w_down[i], h1, g_ple[i][None], tm=ROW_TILE_DOWN, tn=COL_TILE,
                              stage_rows=STAGE_ROWS)
        h = _ple(un2, w_ple_gate[i], h2, p[i].reshape(t, -1), w_ple_proj[i],
                 g_ple_post[i][None], g_final[None], tm=ROW_TILE, tn=COL_TILE,
                 final_norm=(i == depth - 1), stage_rows=STAGE_ROWS)
    return h.reshape(batch, seq, d)
```

```python
import functools
import math

import numpy as np
import jax
import jax.numpy as jnp
from jax import lax
from jax.experimental import pallas as pl
from jax.experimental.pallas import tpu as pltpu

F32 = jnp.float32
BF16 = jnp.bfloat16

GRID_W = 64
ROPE_THETA = 10000.0
RMS_EPS = 1e-6

N_HEADS_A = 8
N_KV_A = 2
HEAD_DIM_A = 128
N_HEADS_B = 8
NOPE_DIM_B = 128
ROPE_DIM_B = 64
V_DIM_B = 128
QK_DIM_B = NOPE_DIM_B + ROPE_DIM_B
QK_PAD_B = 256

LANES = 128
BF16_ROWS = 16
V7X_VMEM_BYTES = 64 * 1024 * 1024
VMEM_SLACK_BYTES = 4 << 20
VMEM_RESERVED_BYTES = 6 << 20
LOG2E = math.log2(math.e)

ROW_TILE = 512
ROW_TILE_DOWN = 256
ROW_TILE_FFN_UP = 1024
COL_TILE = 512
STAGE_ROWS = 256
ATTN_Q_STREAM = 512
ATTN_KV_CHUNK = 512
ATTN_STREAMS = 8


def _vmem_limit(*block_bytes, scratch=0, temps=0):
    need = 2 * sum(block_bytes) + scratch + temps + VMEM_SLACK_BYTES
    return int(min(need, V7X_VMEM_BYTES - VMEM_RESERVED_BYTES))


def _nbytes(shape, dtype):
    return math.prod(shape) * jnp.dtype(dtype).itemsize


def _rms(x32, gain_row):
    ms = jnp.mean(x32 * x32, axis=-1, keepdims=True)
    return x32 * lax.rsqrt(ms + RMS_EPS) * gain_row


def _swap_chunks(y, chunk):
    lane = lax.broadcasted_iota(jnp.int32, y.shape, 1)
    first = (lane % (2 * chunk)) < chunk
    return jnp.where(first, pltpu.roll(y, LANES - chunk, 1), pltpu.roll(y, chunk, 1))


def _rope(y, cos, sin_signed, chunk):
    return y * cos + _swap_chunks(y, chunk) * sin_signed


def _rope_tables(seq, rot_dim):
    rows = seq // GRID_W
    row = np.repeat(np.arange(rows, dtype=np.float64), GRID_W)
    col = np.tile(np.arange(GRID_W, dtype=np.float64), rows)
    half = rot_dim // 2
    inv = 1.0 / np.power(ROPE_THETA, np.arange(0, half, 2, dtype=np.float64) / half)
    ar = row[:, None] * inv[None, :]
    ac = col[:, None] * inv[None, :]
    cos = np.concatenate([np.cos(ar), np.cos(ar), np.cos(ac), np.cos(ac)], axis=-1)
    sin = np.concatenate([-np.sin(ar), np.sin(ar), -np.sin(ac), np.sin(ac)], axis=-1)
    pad = LANES - rot_dim
    if pad:
        cos = np.concatenate([cos, np.ones((seq, pad))], axis=-1)
        sin = np.concatenate([sin, np.zeros((seq, pad))], axis=-1)
    return jnp.asarray(cos, F32), jnp.asarray(sin, F32)


def _stage_bf16(w_hbm, dst_ref, stage_ref, sem_ref, *, row0=0, n_rows=None, transpose=False,
                keep_rows_last=None):
    rows = stage_ref.shape[1]
    n_rows = w_hbm.shape[0] if n_rows is None else n_rows
    n_chunks = n_rows // rows

    def chunk_copy(r, slot):
        return pltpu.make_async_copy(w_hbm.at[pl.ds(row0 + r * rows, rows), :],
                                     stage_ref.at[slot], sem_ref.at[slot])

    chunk_copy(0, 0).start()
    for r in range(n_chunks):
        slot = r % 2
        if r + 1 < n_chunks:
            chunk_copy(r + 1, 1 - slot).start()
        chunk_copy(r, slot).wait()
        blk = stage_ref[slot]
        if keep_rows_last is not None and r == n_chunks - 1:
            row = lax.broadcasted_iota(jnp.int32, blk.shape, 0)
            blk = jnp.where(row < keep_rows_last, blk, 0.0)
        if transpose:
            dst_ref[:, r * rows:(r + 1) * rows] = blk.astype(BF16).T
        else:
            dst_ref[r * rows:(r + 1) * rows, :] = blk.astype(BF16)


def _inproj_kernel(x_ref, g_ref, wt_hbm, wq_hbm, wkv_hbm, qn_ref, kn_ref, lg_ref,
                   cosa_ref, sina_ref, cosb_ref, sinb_ref,
                   qta_ref, ka_ref, vta_ref, qtb_ref, kb_ref, vtb_ref, u_ref,
                   wbf_ref, wqbf_ref, wkvbf_ref, stage_ref, sem_ref,
                   *, q_scale, qb_scale, n_q, n_kv, lora, kr_rows):
    hd = HEAD_DIM_A

    @pl.when(pl.program_id(0) == 0)
    def _():
        _stage_bf16(wt_hbm, wbf_ref, stage_ref, sem_ref, n_rows=wbf_ref.shape[1],
                    transpose=True, keep_rows_last=kr_rows)
        _stage_bf16(wq_hbm, wqbf_ref, stage_ref, sem_ref)
        _stage_bf16(wkv_hbm, wkvbf_ref, stage_ref, sem_ref)

    u = _rms(x_ref[...], g_ref[...]).astype(BF16)
    u_ref[...] = u
    cos = cosa_ref[...]
    sin = sina_ref[...]

    def proj(c0, width):
        return jnp.dot(u, wbf_ref[:, c0:c0 + width], preferred_element_type=F32)

    heads_per_dot = COL_TILE // hd
    for h0 in range(0, n_q, heads_per_dot):
        z = proj(h0 * hd, heads_per_dot * hd)
        for h in range(heads_per_dot):
            y = _rms(z[:, h * hd:(h + 1) * hd], qn_ref[...])
            qta_ref[(h0 + h) * hd:(h0 + h + 1) * hd, :] = (
                _rope(y, cos, sin, hd // 4) * q_scale).astype(BF16).T

    z = proj(n_q * hd, 2 * n_kv * hd)
    for h in range(n_kv):
        y = _rms(z[:, h * hd:(h + 1) * hd], kn_ref[...])
        ka_ref[:, h * hd:(h + 1) * hd] = _rope(y, cos, sin, hd // 4).astype(BF16)
        vta_ref[h * hd:(h + 1) * hd, :] = z[:, (n_kv + h) * hd:(n_kv + h + 1) * hd].astype(BF16).T

    lat0 = (n_q + 2 * n_kv) * hd
    cq = _rms(proj(lat0, lora), lg_ref[0]).astype(BF16)
    ckv = _rms(proj(lat0 + lora, lora), lg_ref[1]).astype(BF16)
    cosb = cosb_ref[...]
    sinb = sinb_ref[...]
    kpe = _rope(proj(lat0 + 2 * lora, LANES), cosb, sinb, ROPE_DIM_B // 4)
    lane = lax.broadcasted_iota(jnp.int32, kpe.shape, 1)
    kpe = jnp.where(lane == ROPE_DIM_B, 1.0, kpe).astype(BF16)
    zq = jnp.dot(cq, wqbf_ref[...], preferred_element_type=F32)
    zkv = jnp.dot(ckv, wkvbf_ref[...], preferred_element_type=F32)
    for h in range(N_HEADS_B):
        c0 = h * QK_PAD_B
        qtb_ref[c0:c0 + LANES, :] = (zq[:, c0:c0 + LANES] * qb_scale).astype(BF16).T
        pe = _rope(zq[:, c0 + LANES:c0 + 2 * LANES], cosb, sinb, ROPE_DIM_B // 4)
        qtb_ref[c0 + LANES:c0 + 2 * LANES, :] = (pe * qb_scale).astype(BF16).T
        kb_ref[:, c0:c0 + LANES] = zkv[:, c0:c0 + LANES].astype(BF16)
        kb_ref[:, c0 + LANES:c0 + 2 * LANES] = kpe
        vtb_ref[h * V_DIM_B:(h + 1) * V_DIM_B, :] = zkv[:, c0 + LANES:c0 + 2 * LANES].astype(BF16).T


def _inproj(x2, g_mix, w_in_t, wq_pad, w_kv_b, q_norm, k_norm, lat_gains,
            cos_a, sin_a, cos_b, sin_b, *, seq, tm, kr_col0):
    t, d = x2.shape
    lora = lat_gains.shape[-1]
    n_qa = N_HEADS_A * HEAD_DIM_A
    n_kva = N_KV_A * HEAD_DIM_A
    n_qkv = n_qa + 2 * n_kva
    n_cols = n_qkv + 2 * lora + LANES
    nqb = wq_pad.shape[1]
    nvb = N_HEADS_B * V_DIM_B
    assert kr_col0 == n_qkv + 2 * lora and n_cols % LANES == 0
    assert (N_HEADS_A * HEAD_DIM_A) % COL_TILE == 0 and w_in_t.shape[0] >= n_cols
    assert t % tm == 0 and seq % tm == 0 and lora % LANES == 0
    assert wq_pad.shape == (lora, N_HEADS_B * QK_PAD_B) and w_kv_b.shape == (lora, nqb)
    assert nqb == d
    sblocks = seq // tm
    q_scale = (HEAD_DIM_A ** -0.5) * LOG2E
    qb_scale = (QK_DIM_B ** -0.5) * LOG2E
    return pl.pallas_call(
        functools.partial(_inproj_kernel, q_scale=q_scale, qb_scale=qb_scale, n_q=N_HEADS_A,
                          n_kv=N_KV_A, lora=lora, kr_rows=ROPE_DIM_B),
        grid=(t // tm,),
        in_specs=[
            pl.BlockSpec((tm, d), lambda i: (i, 0)),
            pl.BlockSpec((1, d), lambda i: (0, 0)),
            pl.BlockSpec(memory_space=pl.ANY),
            pl.BlockSpec(memory_space=pl.ANY),
            pl.BlockSpec(memory_space=pl.ANY),
            pl.BlockSpec((1, HEAD_DIM_A), lambda i: (0, 0)),
            pl.BlockSpec((1, HEAD_DIM_A), lambda i: (0, 0)),
            pl.BlockSpec((2, 1, lora), lambda i: (0, 0, 0)),
            pl.BlockSpec((tm, LANES), lambda i: (i % sblocks, 0)),
            pl.BlockSpec((tm, LANES), lambda i: (i % sblocks, 0)),
            pl.BlockSpec((tm, LANES), lambda i: (i % sblocks, 0)),
            pl.BlockSpec((tm, LANES), lambda i: (i % sblocks, 0)),
        ],
        out_specs=[
            pl.BlockSpec((n_qa, tm), lambda i: (0, i)),
            pl.BlockSpec((tm, n_kva), lambda i: (i, 0)),
            pl.BlockSpec((n_kva, tm), lambda i: (0, i)),
            pl.BlockSpec((nqb, tm), lambda i: (0, i)),
            pl.BlockSpec((tm, nqb), lambda i: (i, 0)),
            pl.BlockSpec((nvb, tm), lambda i: (0, i)),
            pl.BlockSpec((tm, d), lambda i: (i, 0)),
        ],
        out_shape=[
            jax.ShapeDtypeStruct((n_qa, t), BF16),
            jax.ShapeDtypeStruct((t, n_kva), BF16),
            jax.ShapeDtypeStruct((n_kva, t), BF16),
            jax.ShapeDtypeStruct((nqb, t), BF16),
            jax.ShapeDtypeStruct((t, nqb), BF16),
            jax.ShapeDtypeStruct((nvb, t), BF16),
            jax.ShapeDtypeStruct((t, d), BF16),
        ],
        scratch_shapes=[pltpu.VMEM((d, n_cols), BF16), pltpu.VMEM((lora, nqb), BF16),
                        pltpu.VMEM((lora, nqb), BF16), pltpu.VMEM((2, LANES, d), F32),
                        pltpu.SemaphoreType.DMA((2,))],
        compiler_params=pltpu.CompilerParams(
            dimension_semantics=("arbitrary",),
            vmem_limit_bytes=_vmem_limit(
                _nbytes((tm, d), F32), _nbytes((tm, n_qkv), BF16), 2 * _nbytes((tm, nqb), BF16),
                _nbytes((tm, nvb), BF16), _nbytes((tm, d), BF16), 4 * _nbytes((tm, LANES), F32),
                scratch=_nbytes((d, n_cols), BF16) + 2 * _nbytes((lora, nqb), BF16)
                + _nbytes((2, LANES, d), F32),
                temps=_nbytes((tm, d), F32) + 4 * _nbytes((tm, nqb), F32)),
        ),
        name="inproj",
    )(x2, g_mix, w_in_t, wq_pad, w_kv_b, q_norm, k_norm, lat_gains, cos_a, sin_a, cos_b, sin_b)


_L_SAFE = 2.0 ** -64


def _attn_kernel(q_ref, k_ref, v_ref, qg_ref, kg_ref, o_ref, vt_ref, kmax_ref,
                 qa_ref, s_ref, acc_ref, *maybe_kaug, groups, dk, dv, kc, sq, aug_lane,
                 unit_rms_scale):
    seq = k_ref.shape[0]
    tq = q_ref.shape[1]
    append_tile = aug_lane == dk
    kaug_ref = maybe_kaug[0] if append_tile else k_ref
    dka = kaug_ref.shape[1]

    @pl.when(pl.program_id(2) == 0)
    def _():
        if unit_rms_scale is None:
            kt = k_ref[...].T.astype(F32)
            kmax2 = jnp.max(jnp.sum(kt * kt, axis=0, keepdims=True), axis=1, keepdims=True)
            kmax_ref[...] = jnp.broadcast_to(kmax2, kmax_ref.shape)
        if append_tile:
            kaug_ref[:, :dk] = k_ref[...]
            lane = lax.broadcasted_iota(jnp.int32, (seq, dka - dk), 1)
            kaug_ref[:, dk:] = jnp.where(lane == 0, 1.0, 0.0).astype(BF16)
        vt_ref[:dv, :] = v_ref[...]
        vt_ref[dv:, :] = jnp.ones((vt_ref.shape[0] - dv, seq), BF16)

    streams = [(r, g) for r in range(tq // sq) for g in range(groups)]

    n_str = len(streams)

    def stage_queries(shift_row_fn):
        for i, (r, g) in enumerate(streams):
            qt = q_ref[g * dk:(g + 1) * dk, r * sq:(r + 1) * sq].astype(F32)
            neg = -shift_row_fn(i, qt)
            if append_tile:
                row = lax.broadcasted_iota(jnp.int32, (dka - dk, sq), 0)
                qa = jnp.concatenate([qt, jnp.where(row == 0, neg, 0.0)], axis=0)
            else:
                row = lax.broadcasted_iota(jnp.int32, (dk, sq), 0)
                qa = jnp.where(row == aug_lane, neg, qt)
            qa_ref[i] = qa.astype(BF16)

    def run():
        n_chunks = seq // kc
        acc_ref[...] = jnp.zeros(acc_ref.shape, F32)

        def scores(i, c):
            start = c * kc if isinstance(c, int) else pl.multiple_of(c * kc, kc)
            s_ref[i % 2] = jnp.dot(kaug_ref[pl.ds(start, kc), :], qa_ref[i],
                                   preferred_element_type=F32)

        def accumulate(i, c):
            start = c * kc if isinstance(c, int) else pl.multiple_of(c * kc, kc)
            p = jnp.exp2(s_ref[i % 2]).astype(BF16)
            acc_ref[i] += jnp.dot(vt_ref[:, pl.ds(start, kc)], p, preferred_element_type=F32)

        def chunk(c, last):
            for i in range(n_str):
                if i + 1 < n_str:
                    scores(i + 1, c)
                elif not last:
                    scores(0, c + 1)
                accumulate(i, c)

        def body(c, carry):
            chunk(c, last=False)
            return carry

        scores(0, 0)
        lax.fori_loop(0, n_chunks - 1, body, 0, unroll=2)
        chunk(n_chunks - 1, last=True)
        return tuple((acc_ref[i, :dv, :], acc_ref[i, dv:dv + 1, :]) for i in range(n_str))

    def write_out(results):
        for (r, g), (acc, l) in zip(streams, results):
            o_ref[g * dv:(g + 1) * dv, r * sq:(r + 1) * sq] = (acc * (1.0 / l)).astype(BF16)

    if unit_rms_scale is None:
        kmax2 = kmax_ref[0:1, 0:1]
        stage_queries(
            lambda i, qt: jnp.sqrt(jnp.sum(qt * qt, axis=0, keepdims=True) * kmax2))
    else:
        bound = (jnp.max(jnp.abs(qg_ref[...]), axis=1, keepdims=True)
                 * jnp.max(jnp.abs(kg_ref[...]), axis=1, keepdims=True) * (dk * unit_rms_scale))
        stage_queries(lambda i, qt: jnp.broadcast_to(bound, (1, sq)))
    fast = run()
    write_out(fast)

    l_min = None
    for _, l in fast:
        cur = jnp.min(l)
        l_min = cur if l_min is None else jnp.minimum(l_min, cur)

    @pl.when(jnp.logical_not(l_min >= _L_SAFE))
    def _():
        stage_queries(lambda i, qt: jnp.zeros((1, sq), F32))

        def max_body(c, ms):
            start = pl.multiple_of(c * kc, kc)
            ks = kaug_ref[pl.ds(start, kc), :]
            return tuple(
                jnp.maximum(m, jnp.max(
                    jnp.dot(ks, qa_ref[i], preferred_element_type=F32), axis=0, keepdims=True))
                for i, m in enumerate(ms))

        ms = lax.fori_loop(0, seq // kc, max_body,
                           tuple(jnp.full((1, sq), -jnp.inf, F32) for _ in streams))
        stage_queries(lambda i, qt: ms[i])
        write_out(run())


def _attention(qt_arr, k_arr, vt_arr, *, batch, seq, n_kv, groups, dk, dv,
               tq, kc, sq, aug_lane, q_gain=None, k_gain=None, unit_rms_scale=None):
    t = batch * seq
    if unit_rms_scale is None:
        q_gain = k_gain = jnp.ones((1, dk), F32)
    nq = seq // tq
    dka = dk + LANES if aug_lane == dk else dk
    n_str = (tq // sq) * groups
    assert aug_lane <= dk and tq % sq == 0 and seq % kc == 0 and n_str % 2 == 0
    dvx = dv + BF16_ROWS
    scratch_shapes = [pltpu.VMEM((dvx, seq), BF16), pltpu.VMEM((8, LANES), F32),
                      pltpu.VMEM((n_str, dka, sq), BF16), pltpu.VMEM((2, kc, sq), F32),
                      pltpu.VMEM((n_str, dvx, sq), F32)]
    scratch_bytes = (_nbytes((dvx, seq), BF16) + _nbytes((n_str, dka, sq), BF16)
                     + _nbytes((2, kc, sq), F32) + _nbytes((n_str, dvx, sq), F32))
    if aug_lane == dk:
        scratch_shapes.append(pltpu.VMEM((seq, dka), BF16))
        scratch_bytes += _nbytes((seq, dka), BF16)
    return pl.pallas_call(
        functools.partial(_attn_kernel, groups=groups, dk=dk, dv=dv, kc=kc, sq=sq,
                          aug_lane=aug_lane, unit_rms_scale=unit_rms_scale),
        grid=(batch, n_kv, nq),
        in_specs=[
            pl.BlockSpec((groups * dk, tq), lambda b, h, i: (h, b * nq + i)),
            pl.BlockSpec((seq, dk), lambda b, h, i: (b, h)),
            pl.BlockSpec((dv, seq), lambda b, h, i: (h, b)),
            pl.BlockSpec((1, dk), lambda b, h, i: (0, 0)),
            pl.BlockSpec((1, dk), lambda b, h, i: (0, 0)),
        ],
        out_specs=pl.BlockSpec((groups * dv, tq), lambda b, h, i: (h, b * nq + i)),
        out_shape=jax.ShapeDtypeStruct((n_kv * groups * dv, t), BF16),
        scratch_shapes=scratch_shapes,
        compiler_params=pltpu.CompilerParams(
            dimension_semantics=("parallel", "parallel", "arbitrary"),
            vmem_limit_bytes=_vmem_limit(
                _nbytes((tq, groups * dk), BF16), _nbytes((seq, dk), BF16),
                _nbytes((seq, dv), BF16), _nbytes((tq, groups * dv), BF16),
                scratch=scratch_bytes,
                temps=2 * _nbytes((seq, dk), F32) + 4 * _nbytes((kc, sq), F32)),
        ),
        name="attention",
    )(qt_arr, k_arr, vt_arr, q_gain, k_gain)


def _merge_kernel(u_ref, oa_ref, ob_ref, wa_hbm, wb_hbm, wt_hbm, o_ref,
                  wabf_ref, wbbf_ref, wgbf_ref, stage_ref, sem_ref, *, tn, gate_row0):
    @pl.when(pl.program_id(0) == 0)
    def _():
        _stage_bf16(wa_hbm, wabf_ref, stage_ref, sem_ref)
        _stage_bf16(wb_hbm, wbbf_ref, stage_ref, sem_ref)
        _stage_bf16(wt_hbm, wgbf_ref, stage_ref, sem_ref, row0=gate_row0,
                    n_rows=wgbf_ref.shape[1], transpose=True)

    u = u_ref[...]
    d = o_ref.shape[1]
    tn_dims = (((0,), (0,)), ((), ()))
    for jj in range(d // tn):
        cols = slice(jj * tn, (jj + 1) * tn)
        gcols = slice(d + jj * tn, d + (jj + 1) * tn)
        ga = jnp.dot(u, wgbf_ref[:, cols], preferred_element_type=F32)
        gb = jnp.dot(u, wgbf_ref[:, gcols], preferred_element_type=F32)
        ya = lax.dot_general(oa_ref[...], wabf_ref[:, cols], tn_dims, preferred_element_type=F32)
        yb = lax.dot_general(ob_ref[...], wbbf_ref[:, cols], tn_dims, preferred_element_type=F32)
        o_ref[:, cols] = (jax.nn.sigmoid(ga) * ya + jax.nn.sigmoid(gb) * yb).astype(BF16)


def _merge(u, oa_t, ob_t, w_a_out, w_b_out, w_in_t, *, tm, tn, gate_row0, stage_rows):
    t, d = u.shape
    ka = oa_t.shape[0]
    kb = ob_t.shape[0]
    assert ka % stage_rows == 0 and kb % stage_rows == 0 and (2 * d) % stage_rows == 0
    assert w_a_out.shape[1] == d and w_in_t.shape[1] == d
    return pl.pallas_call(
        functools.partial(_merge_kernel, tn=tn, gate_row0=gate_row0),
        grid=(t // tm,),
        in_specs=[
            pl.BlockSpec((tm, d), lambda i: (i, 0)),
            pl.BlockSpec((ka, tm), lambda i: (0, i)),
            pl.BlockSpec((kb, tm), lambda i: (0, i)),
            pl.BlockSpec(memory_space=pl.ANY),
            pl.BlockSpec(memory_space=pl.ANY),
            pl.BlockSpec(memory_space=pl.ANY),
        ],
        out_specs=pl.BlockSpec((tm, d), lambda i: (i, 0)),
        out_shape=jax.ShapeDtypeStruct((t, d), BF16),
        scratch_shapes=[pltpu.VMEM((ka, d), BF16), pltpu.VMEM((kb, d), BF16),
                        pltpu.VMEM((d, 2 * d), BF16), pltpu.VMEM((2, stage_rows, d), F32),
                        pltpu.SemaphoreType.DMA((2,))],
        compiler_params=pltpu.CompilerParams(
            dimension_semantics=("arbitrary",),
            vmem_limit_bytes=_vmem_limit(
                _nbytes((tm, d), BF16), _nbytes((tm, ka), BF16), _nbytes((tm, kb), BF16),
                _nbytes((tm, d), BF16),
                scratch=_nbytes((ka + kb + 2 * d, d), BF16) + _nbytes((2, stage_rows, d), F32),
                temps=8 * _nbytes((tm, tn), F32)),
        ),
        name="merge",
    )(u, oa_t, ob_t, w_a_out, w_b_out, w_in_t)


def _resid_rows_kernel(lhs_ref, w_hbm, resid_ref, g_ref, h_ref, un_ref,
                       wbf_ref, stage_ref, sem_ref, *, tn):
    @pl.when(pl.program_id(0) == 0)
    def _():
        _stage_bf16(w_hbm, wbf_ref, stage_ref, sem_ref)

    lhs = lhs_ref[...]
    d = h_ref.shape[1]
    ss = jnp.zeros((h_ref.shape[0], 1), F32)
    for jj in range(d // tn):
        cols = slice(jj * tn, (jj + 1) * tn)
        val = resid_ref[:, cols] + jnp.dot(lhs, wbf_ref[:, cols], preferred_element_type=F32)
        h_ref[:, cols] = val
        ss = ss + jnp.sum(val * val, axis=-1, keepdims=True)
    inv = lax.rsqrt(ss * (1.0 / d) + RMS_EPS)
    un_ref[...] = (h_ref[...] * inv * g_ref[...]).astype(BF16)


def _resid_rows(lhs, w, resid, gain, *, tm, tn, stage_rows):
    t, k = lhs.shape
    d = w.shape[1]
    assert k % stage_rows == 0 and d % tn == 0
    return pl.pallas_call(
        functools.partial(_resid_rows_kernel, tn=tn),
        grid=(t // tm,),
        in_specs=[
            pl.BlockSpec((tm, k), lambda i: (i, 0)),
            pl.BlockSpec(memory_space=pl.ANY),
            pl.BlockSpec((tm, d), lambda i: (i, 0)),
            pl.BlockSpec((1, d), lambda i: (0, 0)),
        ],
        out_specs=[
            pl.BlockSpec((tm, d), lambda i: (i, 0)),
            pl.BlockSpec((tm, d), lambda i: (i, 0)),
        ],
        out_shape=[
            jax.ShapeDtypeStruct((t, d), F32),
            jax.ShapeDtypeStruct((t, d), BF16),
        ],
        scratch_shapes=[pltpu.VMEM((k, d), BF16), pltpu.VMEM((2, stage_rows, d), F32),
                        pltpu.SemaphoreType.DMA((2,))],
        compiler_params=pltpu.CompilerParams(
            dimension_semantics=("arbitrary",),
            vmem_limit_bytes=_vmem_limit(
                _nbytes((tm, k), BF16), 2 * _nbytes((tm, d), F32), _nbytes((tm, d), BF16),
                scratch=_nbytes((k, d), BF16) + _nbytes((2, stage_rows, d), F32),
                temps=4 * _nbytes((tm, tn), F32) + _nbytes((tm, d), F32)),
        ),
        name="resid_rows",
    )(lhs, w, resid, gain)


def _ffn_up_kernel(un_ref, wg_ref, wu_ref, o_ref):
    un = un_ref[...]
    g = jnp.dot(un, wg_ref[...].astype(BF16), preferred_element_type=F32)
    u = jnp.dot(un, wu_ref[...].astype(BF16), preferred_element_type=F32)
    o_ref[...] = (jax.nn.silu(g) * u).astype(BF16)


def _ffn_up(un, w_gate_up, *, tm, tn):
    t, d = un.shape
    d_ff = w_gate_up.shape[1] // 2
    nj = d_ff // tn
    return pl.pallas_call(
        _ffn_up_kernel,
        grid=(t // tm, nj),
        in_specs=[
            pl.BlockSpec((tm, d), lambda i, j: (i, 0)),
            pl.BlockSpec((d, tn), lambda i, j: (0, j)),
            pl.BlockSpec((d, tn), lambda i, j: (0, nj + j)),
        ],
        out_specs=pl.BlockSpec((tm, tn), lambda i, j: (i, j)),
        out_shape=jax.ShapeDtypeStruct((t, d_ff), BF16),
        compiler_params=pltpu.CompilerParams(
            dimension_semantics=("parallel", "arbitrary"),
            vmem_limit_bytes=_vmem_limit(
                _nbytes((tm, d), BF16), 2 * _nbytes((d, tn), F32), _nbytes((tm, tn), BF16),
                temps=2 * _nbytes((d, tn), BF16) + 4 * _nbytes((tm, tn), F32)),
        ),
        name="ffn_up",
    )(un, w_gate_up, w_gate_up)


def _ple_kernel(un_ref, w_hbm, h_ref, p_ref, wp_hbm, gpost_ref, gfin_ref,
                o_ref, wbf_ref, wpbf_ref, stage_ref, sem_ref, *, tn, final_norm):
    @pl.when(pl.program_id(0) == 0)
    def _():
        _stage_bf16(w_hbm, wbf_ref, stage_ref, sem_ref)
        _stage_bf16(wp_hbm, wpbf_ref, stage_ref, sem_ref)

    un = un_ref[...]
    d = o_ref.shape[1]
    e = _rms(jnp.dot(p_ref[...].astype(BF16), wpbf_ref[...], preferred_element_type=F32),
             gpost_ref[...])
    ss = jnp.zeros((o_ref.shape[0], 1), F32)
    for jj in range(d // tn):
        cols = slice(jj * tn, (jj + 1) * tn)
        gate = jax.nn.sigmoid(jnp.dot(un, wbf_ref[:, cols], preferred_element_type=F32))
        val = h_ref[:, cols] + gate * e[:, cols]
        o_ref[:, cols] = val
        ss = ss + jnp.sum(val * val, axis=-1, keepdims=True)
    if final_norm:
        inv = lax.rsqrt(ss * (1.0 / d) + RMS_EPS)
        o_ref[...] = o_ref[...] * inv * gfin_ref[...]


def _ple(un, w_gate, h, p2, w_proj, g_post, g_final, *, tm, tn, final_norm, stage_rows):
    t, d = un.shape
    pd = p2.shape[1]
    assert d % stage_rows == 0 and pd % stage_rows == 0
    return pl.pallas_call(
        functools.partial(_ple_kernel, tn=tn, final_norm=final_norm),
        grid=(t // tm,),
        in_specs=[
            pl.BlockSpec((tm, d), lambda i: (i, 0)),
            pl.BlockSpec(memory_space=pl.ANY),
            pl.BlockSpec((tm, d), lambda i: (i, 0)),
            pl.BlockSpec((tm, pd), lambda i: (i, 0)),
            pl.BlockSpec(memory_space=pl.ANY),
            pl.BlockSpec((1, d), lambda i: (0, 0)),
            pl.BlockSpec((1, d), lambda i: (0, 0)),
        ],
        out_specs=pl.BlockSpec((tm, d), lambda i: (i, 0)),
        out_shape=jax.ShapeDtypeStruct((t, d), F32),
        scratch_shapes=[pltpu.VMEM((d, d), BF16), pltpu.VMEM((pd, d), BF16),
                        pltpu.VMEM((2, stage_rows, d), F32), pltpu.SemaphoreType.DMA((2,))],
        compiler_params=pltpu.CompilerParams(
            dimension_semantics=("arbitrary",),
            vmem_limit_bytes=_vmem_limit(
                _nbytes((tm, d), BF16), 2 * _nbytes((tm, d), F32), _nbytes((tm, pd), F32),
                scratch=_nbytes((d + pd, d), BF16) + _nbytes((2, stage_rows, d), F32),
                temps=2 * _nbytes((tm, d), F32) + 4 * _nbytes((tm, tn), F32)),
        ),
        name="ple_final",
    )(un, w_gate, h, p2, w_proj, g_post, g_final)


def kernel(x, p, g_mix, w_in, q_norm_a, k_norm_a, q_a_norm, w_q_b, kv_a_norm, w_kv_b,
           w_a_out, w_b_out, w_o, g_ffn, w_gate_up, w_down, g_ple, w_ple_gate,
           w_ple_proj, g_ple_post, g_final):
    batch, seq, d = x.shape
    depth = w_in.shape[0]
    t = batch * seq
    q_lora = q_a_norm.shape[-1]
    kv_lora = kv_a_norm.shape[-1]
    assert q_lora == kv_lora
    n_qa = N_HEADS_A * HEAD_DIM_A
    n_kva = N_KV_A * HEAD_DIM_A
    lat0 = n_qa + 2 * n_kva
    kr0 = lat0 + q_lora + kv_lora
    gate0 = kr0 + ROPE_DIM_B

    cos_a, sin_a = _rope_tables(seq, HEAD_DIM_A)
    cos_b, sin_b = _rope_tables(seq, ROPE_DIM_B)

    h = x.reshape(t, d)
    for i in range(depth):
        wi_t = jnp.swapaxes(w_in[i], 0, 1)
        wq_pad = jnp.pad(
            w_q_b[i].reshape(q_lora, N_HEADS_B, QK_DIM_B),
            ((0, 0), (0, 0), (0, QK_PAD_B - QK_DIM_B))).reshape(q_lora, N_HEADS_B * QK_PAD_B)
        lat_gains = jnp.stack([q_a_norm[i], kv_a_norm[i]])[:, None, :]
        qt_a, k_a, vt_a, qt_b, k_b, vt_b, u = _inproj(
            h, g_mix[i][None], wi_t, wq_pad, w_kv_b[i], q_norm_a[i][None], k_norm_a[i][None],
            lat_gains, cos_a, sin_a, cos_b, sin_b, seq=seq, tm=ROW_TILE, kr_col0=kr0)

        groups_a = N_HEADS_A // N_KV_A
        oa_t = _attention(qt_a, k_a, vt_a, batch=batch, seq=seq, n_kv=N_KV_A,
                          groups=groups_a, dk=HEAD_DIM_A, dv=HEAD_DIM_A,
                          tq=2 * ATTN_STREAMS * ATTN_Q_STREAM // groups_a, kc=ATTN_KV_CHUNK,
                          sq=ATTN_Q_STREAM, aug_lane=HEAD_DIM_A, q_gain=q_norm_a[i][None],
                          k_gain=k_norm_a[i][None], unit_rms_scale=(HEAD_DIM_A ** -0.5) * LOG2E)
        ob_t = _attention(qt_b, k_b, vt_b, batch=batch, seq=seq, n_kv=N_HEADS_B,
                          groups=1, dk=QK_PAD_B, dv=V_DIM_B,
                          tq=ATTN_STREAMS * ATTN_Q_STREAM, kc=ATTN_KV_CHUNK, sq=ATTN_Q_STREAM,
                          aug_lane=QK_DIM_B)

        merged = _merge(u, oa_t, ob_t, w_a_out[i], w_b_out[i], wi_t, tm=ROW_TILE, tn=COL_TILE,
                        gate_row0=gate0, stage_rows=STAGE_ROWS)
        h1, un1 = _resid_rows(merged, w_o[i], h, g_ffn[i][None], tm=ROW_TILE, tn=COL_TILE,
                              stage_rows=STAGE_ROWS)
        act = _ffn_up(un1, w_gate_up[i], tm=ROW_TILE_FFN_UP, tn=COL_TILE)
        h2, un2 = _resid_rows(act, w_down[i], h1, g_ple[i][None], tm=ROW_TILE_DOWN, tn=COL_TILE,
                              stage_rows=STAGE_ROWS)
        h = _ple(un2, w_ple_gate[i], h2, p[i].reshape(t, -1), w_ple_proj[i],
                 g_ple_post[i][None], g_final[None], tm=ROW_TILE, tn=COL_TILE,
                 final_norm=(i == depth - 1), stage_rows=STAGE_ROWS)
    return h.reshape(batch, seq, d)
```
